```python
import jax
import jax.numpy as jnp
from jax import lax
import numpy as np


D_MODEL = 1024
BATCH = 8
SEQ = 2048
DEPTH = 1

M_WIDTH = D_MODEL
M_HEADS = 4
M_HEAD_DIM = M_WIDTH // M_HEADS
M_CHUNK = 128
CONV_WIDTH = 4
A_GROUPS = ((128, 1), (512, 4), (2048, 16))
N_GROUPS = 3
A_HEADS_PER_GROUP = 4
A_HEAD_DIM = 128
A_OUT_WIDTH = A_HEADS_PER_GROUP * A_HEAD_DIM
A_QKV_WIDTH = N_GROUPS * A_OUT_WIDTH
A_BLOCK = 128
ROPE_DIM = A_HEAD_DIM // 4
ROPE_THETA = 500000.0
NORM_EPS = 1e-6
NEG_INF = -1e30
IN_SPLITS = (M_WIDTH, M_WIDTH, M_WIDTH, M_WIDTH, M_WIDTH, 2 * M_HEADS, A_QKV_WIDTH, A_QKV_WIDTH, A_QKV_WIDTH, A_OUT_WIDTH, D_MODEL, D_MODEL)
IN_WIDTH = 5 * M_WIDTH + 2 * M_HEADS + 3 * A_QKV_WIDTH + A_OUT_WIDTH + 2 * D_MODEL

kernel_name = 'hybrid_mlstm_dilated_attn_block'


def rmsnorm(x, w):
    xf = x.astype(jnp.float32)
    y = xf * lax.rsqrt(jnp.mean(xf * xf, axis=-1, keepdims=True) + NORM_EPS) * w.astype(jnp.float32)
    return y.astype(x.dtype)


def causal_depthwise_conv(x, w, b):
    C = x.shape[-1]
    y = lax.conv_general_dilated(x, w.astype(x.dtype)[:, None, :], window_strides=(1,), padding=[(CONV_WIDTH - 1, 0)], dimension_numbers=('NWC', 'WIO', 'NWC'), feature_group_count=C)
    return y + b.astype(x.dtype)


def apply_partial_rope(x, cos, sin):
    half = ROPE_DIM // 2
    c = cos[None, :, None, :]
    s = sin[None, :, None, :]
    x1 = x[..., :half]
    x2 = x[..., half:ROPE_DIM]
    return jnp.concatenate([x1 * c - x2 * s, x2 * c + x1 * s, x[..., ROPE_DIM:]], axis=-1)


def mlstm_chunkwise(q, k, v, i_pre, log_f):
    B, H, S, Dk = q.shape
    Dv = v.shape[-1]
    L = M_CHUNK
    nc = S // L
    qc = q.reshape(B, H, nc, L, Dk)
    kc = k.reshape(B, H, nc, L, Dk)
    vc = v.reshape(B, H, nc, L, Dv)
    ic = i_pre.reshape(B, H, nc, L)
    b = jnp.cumsum(log_f.reshape(B, H, nc, L), axis=-1)
    b_tot = b[..., -1]
    a = b_tot[..., None] - b + ic

    def step(carry, inp):
        C, n, m = carry
        k_j, v_j, a_j, bt_j = inp
        m_new = jnp.maximum(bt_j + m, a_j.max(-1))
        decay = jnp.exp(bt_j + m - m_new)
        w = jnp.exp(a_j - m_new[..., None])
        C_new = decay[..., None, None] * C + jnp.einsum('bhl,bhlk,bhlv->bhkv', w, k_j, v_j)
        n_new = decay[..., None] * n + jnp.einsum('bhl,bhlk->bhk', w, k_j)
        return (C_new, n_new, m_new), (C, n, m)

    init = (jnp.zeros((B, H, Dk, Dv), jnp.float32), jnp.zeros((B, H, Dk), jnp.float32), jnp.zeros((B, H), jnp.float32))
    xs = (jnp.moveaxis(kc, 2, 0), jnp.moveaxis(vc, 2, 0), jnp.moveaxis(a, 2, 0), jnp.moveaxis(b_tot, 2, 0))
    _, (C_prev, n_prev, m_prev) = lax.scan(step, init, xs)
    C_prev = jnp.moveaxis(C_prev, 0, 2)
    n_prev = jnp.moveaxis(n_prev, 0, 2)
    m_prev = jnp.moveaxis(m_prev, 0, 2)

    causal = jnp.tril(jnp.ones((L, L), dtype=bool))
    log_d = jnp.where(causal, b[..., :, None] - b[..., None, :] + ic[..., None, :], NEG_INF)
    inter_log = b + m_prev[..., None]
    m_t = jnp.maximum(inter_log, log_d.max(-1))
    s = jnp.einsum('bhnld,bhnsd->bhnls', qc, kc) * jnp.exp(log_d - m_t[..., None])
    inter = jnp.exp(inter_log - m_t)
    num = inter[..., None] * jnp.einsum('bhnld,bhndv->bhnlv', qc, C_prev) + jnp.einsum('bhnls,bhnsv->bhnlv', s, vc)
    den = inter * jnp.einsum('bhnld,bhnd->bhnl', qc, n_prev) + s.sum(-1)
    h = num / jnp.maximum(jnp.abs(den), jnp.exp(-m_t))[..., None]
    return h.reshape(B, H, S, Dv)


def dilated_window_attention(q, k, v, window, dilation):
    B, S, H, Dh = q.shape
    band = window // dilation
    Ls = S // dilation
    nblk = -(-Ls // A_BLOCK)
    Lp = nblk * A_BLOCK

    def by_residue(t):
        return t.reshape(B, Ls, dilation, H, Dh).transpose(0, 2, 3, 1, 4)

    qs, ks, vs = by_residue(q), by_residue(k), by_residue(v)
    qb = jnp.pad(qs, ((0, 0), (0, 0), (0, 0), (0, Lp - Ls), (0, 0))).reshape(B, dilation, H, nblk, A_BLOCK, Dh)

    def key_blocks(t):
        tp = jnp.pad(t, ((0, 0), (0, 0), (0, 0), (A_BLOCK, Lp - Ls), (0, 0)))
        prev = tp[..., :Lp, :].reshape(B, dilation, H, nblk, A_BLOCK, Dh)
        cur = tp[..., A_BLOCK:, :].reshape(B, dilation, H, nblk, A_BLOCK, Dh)
        return jnp.concatenate([prev, cur], axis=-2)

    kb, vb = key_blocks(ks), key_blocks(vs)
    t_idx = jnp.arange(A_BLOCK)[:, None]
    j_idx = jnp.arange(2 * A_BLOCK)[None, :]
    dist = t_idx - j_idx + A_BLOCK
    kpos = jnp.arange(nblk)[:, None, None] * A_BLOCK - A_BLOCK + j_idx[None]
    valid = (dist >= 0) & (dist <= band) & (kpos >= 0)
    s = jnp.einsum('bdhnqe,bdhnke->bdhnqk', qb, kb) * (A_HEAD_DIM ** -0.5)
    s = jnp.where(valid, s, NEG_INF)
    m = s.max(-1)
    p = jnp.where(valid, jnp.exp(s - m[..., None]), 0.0)
    l = p.sum(-1)
    o = jnp.einsum('bdhnqk,bdhnke->bdhnqe', p, vb) / jnp.where(l > 0, l, 1.0)[..., None]
    o = o.reshape(B, dilation, H, Lp, Dh)[..., :Ls, :].transpose(0, 3, 1, 2, 4).reshape(B, S, H, Dh)
    m = m.reshape(B, dilation, H, Lp)[..., :Ls].transpose(0, 3, 1, 2).reshape(B, S, H)
    l = l.reshape(B, dilation, H, Lp)[..., :Ls].transpose(0, 3, 1, 2).reshape(B, S, H)
    return o, m, l


def hybrid_layer(x, pre_w, w_in, b_if, conv_w, conv_b, m_norm_w, w_pm, w_pa, w_out, post_w):
    B, S, _ = x.shape
    f32 = jnp.float32
    h = rmsnorm(x, pre_w)
    proj = h @ w_in.astype(x.dtype)
    idx = np.cumsum(np.array(IN_SPLITS))[:-1].tolist()
    q_m, k_m, v_m, o_m, z_m, if_pre, q_a, k_a, v_a, z_a, g_m, g_a = jnp.split(proj, idx, axis=-1)

    qk = jax.nn.silu(causal_depthwise_conv(jnp.concatenate([q_m, k_m], axis=-1), conv_w, conv_b))
    q_m, k_m = qk[..., :M_WIDTH], qk[..., M_WIDTH:]

    def to_heads(t):
        return t.astype(f32).reshape(B, S, M_HEADS, M_HEAD_DIM).transpose(0, 2, 1, 3)

    gates = if_pre.astype(f32) + b_if.astype(f32)
    i_pre = gates[..., :M_HEADS].transpose(0, 2, 1)
    log_f = jax.nn.log_sigmoid(gates[..., M_HEADS:]).transpose(0, 2, 1)
    hm = mlstm_chunkwise(to_heads(q_m), to_heads(k_m) * (M_HEAD_DIM ** -0.5), to_heads(v_m), i_pre, log_f)
    hm = hm.transpose(0, 2, 1, 3) * jax.nn.sigmoid(o_m.astype(f32)).reshape(B, S, M_HEADS, M_HEAD_DIM)
    hm = hm * lax.rsqrt(jnp.mean(hm * hm, axis=-1, keepdims=True) + NORM_EPS) * m_norm_w.astype(f32).reshape(M_HEADS, M_HEAD_DIM)
    hm = hm.reshape(B, S, M_WIDTH).astype(x.dtype) * jax.nn.silu(z_m)
    branch_m = hm @ w_pm.astype(x.dtype)

    pos = jnp.arange(S, dtype=f32)
    inv_freq = ROPE_THETA ** (-jnp.arange(0, ROPE_DIM, 2, dtype=f32) / ROPE_DIM)
    ang = pos[:, None] * inv_freq[None, :]
    cos, sin = jnp.cos(ang), jnp.sin(ang)
    qa = q_a.astype(f32).reshape(B, S, N_GROUPS, A_HEADS_PER_GROUP, A_HEAD_DIM)
    ka = k_a.astype(f32).reshape(B, S, N_GROUPS, A_HEADS_PER_GROUP, A_HEAD_DIM)
    va = v_a.astype(f32).reshape(B, S, N_GROUPS, A_HEADS_PER_GROUP, A_HEAD_DIM)
    outs, maxes, dens = [], [], []
    for g, (window, dilation) in enumerate(A_GROUPS):
        o, m, l = dilated_window_attention(apply_partial_rope(qa[:, :, g], cos, sin), apply_partial_rope(ka[:, :, g], cos, sin), va[:, :, g], window, dilation)
        outs.append(o)
        maxes.append(m)
        dens.append(l)
    outs = jnp.stack(outs)
    maxes = jnp.stack(maxes)
    dens = jnp.stack(dens)
    wts = jnp.exp(maxes - maxes.max(0, keepdims=True)) * dens
    att = (wts[..., None] * outs).sum(0) / wts.sum(0)[..., None]
    att = att.reshape(B, S, A_OUT_WIDTH).astype(x.dtype) * jax.nn.silu(z_a)
    branch_a = att @ w_pa.astype(x.dtype)

    merged = jax.nn.sigmoid(g_m) * branch_m + jax.nn.sigmoid(g_a) * branch_a
    y = merged @ w_out.astype(x.dtype)
    return x + rmsnorm(y, post_w)


def setup_inputs(seed: int = 0) -> dict:
    key = jax.random.key(seed)
    ks = jax.random.split(key, 12)
    f32 = jnp.float32
    nrm = jax.random.normal
    x = nrm(ks[0], (BATCH, SEQ, D_MODEL), f32)
    pre_norm_w = 1.0 + 0.05 * nrm(ks[1], (DEPTH, D_MODEL), f32)
    w_in = nrm(ks[2], (DEPTH, D_MODEL, IN_WIDTH), f32) * D_MODEL ** -0.5
    b_if = jnp.concatenate([0.1 * nrm(ks[3], (DEPTH, M_HEADS), f32), jnp.linspace(3.0, 6.0, M_HEADS, dtype=f32)[None, :] + 0.1 * nrm(ks[4], (DEPTH, M_HEADS), f32)], axis=-1)
    conv_w = nrm(ks[5], (DEPTH, CONV_WIDTH, 2 * M_WIDTH), f32) * CONV_WIDTH ** -0.5
    conv_b = 0.01 * nrm(ks[6], (DEPTH, 2 * M_WIDTH), f32)
    m_norm_w = 1.0 + 0.05 * nrm(ks[7], (DEPTH, M_WIDTH), f32)
    w_proj_m = nrm(ks[8], (DEPTH, M_WIDTH, D_MODEL), f32) * M_WIDTH ** -0.5
    w_proj_a = nrm(ks[9], (DEPTH, A_OUT_WIDTH, D_MODEL), f32) * A_OUT_WIDTH ** -0.5
    w_out = nrm(ks[10], (DEPTH, D_MODEL, D_MODEL), f32) * D_MODEL ** -0.5
    post_norm_w = 1.0 + 0.05 * nrm(ks[11], (DEPTH, D_MODEL), f32)
    return {'x': x, 'pre_norm_w': pre_norm_w, 'w_in': w_in, 'b_if': b_if, 'conv_w': conv_w, 'conv_b': conv_b, 'm_norm_w': m_norm_w, 'w_proj_m': w_proj_m, 'w_proj_a': w_proj_a, 'w_out': w_out, 'post_norm_w': post_norm_w}


def reference(x, pre_norm_w, w_in, b_if, conv_w, conv_b, m_norm_w, w_proj_m, w_proj_a, w_out, post_norm_w):
    for layer in range(DEPTH):
        x = hybrid_layer(x, pre_norm_w[layer], w_in[layer], b_if[layer], conv_w[layer], conv_b[layer], m_norm_w[layer], w_proj_m[layer], w_proj_a[layer], w_out[layer], post_norm_w[layer])
    return x
```

```python
import functools

import jax
import jax.numpy as jnp
from jax import lax
from jax.experimental import pallas as pl
from jax.experimental.pallas import tpu as pltpu

D_MODEL = 1024
BATCH = 8
SEQ = 2048
M_WIDTH = 1024
M_HEADS = 4
M_HEAD_DIM = 256
CONV_WIDTH = 4
A_GROUPS = ((128, 1), (512, 4), (2048, 16))
N_GROUPS = 3
A_HEADS = 4
A_HEAD_DIM = 128
A_OUT_WIDTH = 512
A_QKV_WIDTH = 1536
A_BLOCK = 128
ROPE_DIM = 32
ROPE_THETA = 500000.0
NORM_EPS = 1e-6
NEG_INF = -1e30

OFF_GATES = 5 * M_WIDTH
OFF_QA = OFF_GATES + 2 * M_HEADS
OFF_KA = OFF_QA + A_QKV_WIDTH
OFF_VA = OFF_KA + A_QKV_WIDTH
OFF_ZA = OFF_VA + A_QKV_WIDTH
OFF_GM = OFF_ZA + A_OUT_WIDTH
IN_WIDTH = OFF_GM + 2 * D_MODEL

LANES = 128
M_CHUNK = 256
N_CHUNKS = SEQ // M_CHUNK
VMEM_LIMIT = 48 * 1024 * 1024

F32 = jnp.float32
BF16 = jnp.bfloat16


def _sigmoid(x):
    return 1.0 / (1.0 + jnp.exp(-x))


def _silu(x):
    return x * _sigmoid(x)


def _norm_kernel(x_ref, w_ref, h_ref):
    x = x_ref[...]
    ms = jnp.mean(x * x, axis=-1, keepdims=True)
    h_ref[...] = (x * lax.rsqrt(ms + NORM_EPS) * w_ref[...]).astype(BF16)


def _norm(x2d, pre_w):
    n = x2d.shape[0]
    tm = 1024
    return pl.pallas_call(
        _norm_kernel,
        grid=(n // tm,),
        in_specs=[pl.BlockSpec((tm, D_MODEL), lambda i: (i, 0)),
                  pl.BlockSpec((1, D_MODEL), lambda i: (0, 0))],
        out_specs=pl.BlockSpec((tm, D_MODEL), lambda i: (i, 0)),
        out_shape=jax.ShapeDtypeStruct((n, D_MODEL), BF16),
        compiler_params=pltpu.CompilerParams(dimension_semantics=("parallel",),
                                             vmem_limit_bytes=VMEM_LIMIT),
        name="norm",
    )(x2d, pre_w.reshape(1, D_MODEL))


M_TN = 512


def _causal_conv(acc, cw, cb):
    rows = lax.broadcasted_iota(jnp.int32, acc.shape, 0)
    y = acc * cw[CONV_WIDTH - 1:CONV_WIDTH, :] + cb
    for s in range(1, CONV_WIDTH):
        sh = pltpu.roll(acc, s, axis=0)
        sh = jnp.where(rows >= s, sh, 0.0)
        y = y + sh * cw[CONV_WIDTH - 1 - s:CONV_WIDTH - s, :]
    return y


def _mproj_kernel(h_ref, w_ref, wg_ref, bif_ref, cw_ref, cb_ref, mall_ref, kt_ref, gt_ref):
    j = pl.program_id(1)
    h = h_ref[...]

    @pl.when(j == 0)
    def _():
        g = jnp.dot(h, wg_ref[...], preferred_element_type=F32)
        gt = g.T
        gt_ref[...] = gt[:2 * M_HEADS, :] + bif_ref[...]

    acc = jnp.dot(h, w_ref[...].astype(BF16), preferred_element_type=F32)

    @pl.when(j < 2)
    def _():
        y = _causal_conv(acc, cw_ref[...], cb_ref[...])
        mall_ref[...] = _silu(y).astype(BF16)

    @pl.when(jnp.logical_and(j >= 2, j < 4))
    def _():
        y = _causal_conv(acc, cw_ref[...], cb_ref[...])
        y = _silu(y) * (M_HEAD_DIM ** -0.5)
        mall_ref[...] = y.astype(BF16)
        for hh in range(M_TN // M_HEAD_DIM):
            for c in range(N_CHUNKS):
                blk = y[c * M_CHUNK:(c + 1) * M_CHUNK, hh * M_HEAD_DIM:(hh + 1) * M_HEAD_DIM]
                kt_ref[0, hh, c] = blk.T.astype(BF16)

    @pl.when(jnp.logical_and(j >= 4, j < 6))
    def _():
        mall_ref[...] = acc.astype(BF16)

    @pl.when(jnp.logical_and(j >= 6, j < 8))
    def _():
        mall_ref[...] = _sigmoid(acc).astype(BF16)

    @pl.when(j >= 8)
    def _():
        mall_ref[...] = _silu(acc).astype(BF16)


def _mproj(h, w_in, wg, bif, conv_w, conv_b):
    n_tiles = 5 * M_WIDTH // M_TN
    kt_tiles_lo = 2
    return pl.pallas_call(
        _mproj_kernel,
        grid=(BATCH, n_tiles),
        in_specs=[
            pl.BlockSpec((SEQ, D_MODEL), lambda i, j: (i, 0)),
            pl.BlockSpec((D_MODEL, M_TN), lambda i, j: (0, j)),
            pl.BlockSpec((D_MODEL, LANES), lambda i, j: (0, 0)),
            pl.BlockSpec((2 * M_HEADS, 1), lambda i, j: (0, 0)),
            pl.BlockSpec((CONV_WIDTH, M_TN), lambda i, j: (0, jnp.minimum(j, 3))),
            pl.BlockSpec((1, M_TN), lambda i, j: (0, jnp.minimum(j, 3))),
        ],
        out_specs=[
            pl.BlockSpec((SEQ, M_TN), lambda i, j: (i, j)),
            pl.BlockSpec((1, M_TN // M_HEAD_DIM, N_CHUNKS, M_HEAD_DIM, M_CHUNK),
                         lambda i, j: (i, jnp.clip(j - kt_tiles_lo, 0, 1), 0, 0, 0)),
            pl.BlockSpec((2 * M_HEADS, SEQ), lambda i, j: (0, i)),
        ],
        out_shape=[
            jax.ShapeDtypeStruct((BATCH * SEQ, 5 * M_WIDTH), BF16),
            jax.ShapeDtypeStruct((BATCH, M_HEADS, N_CHUNKS, M_HEAD_DIM, M_CHUNK), BF16),
            jax.ShapeDtypeStruct((2 * M_HEADS, BATCH * SEQ), F32),
        ],
        compiler_params=pltpu.CompilerParams(dimension_semantics=("parallel", "arbitrary"),
                                             vmem_limit_bytes=VMEM_LIMIT),
        name="mproj",
    )(h, w_in, wg, bif, conv_w, conv_b)


def _aproj_kernel(dil, h_ref, w_ref, tab_ref, o_ref, hperm_ref):
    t = pl.program_id(1)
    ls = SEQ // dil

    if dil > 1:
        @pl.when(t == 0)
        def _():
            for r in range(dil):
                hperm_ref[r * ls:(r + 1) * ls, :] = h_ref[0, :, r * D_MODEL:(r + 1) * D_MODEL]
        hmat = hperm_ref[...]
    else:
        hmat = h_ref[0]

    acc = jnp.dot(hmat, w_ref[0], preferred_element_type=F32)

    @pl.when(t < 2)
    def _():
        cos_t = tab_ref[0]
        sin_up = tab_ref[1]
        sin_dn = tab_ref[2]
        scale = jnp.where(t == 0, A_HEAD_DIM ** -0.5, 1.0).astype(F32)
        half = ROPE_DIM // 2
        for hd in range(A_HEADS):
            xh = acc[:, hd * A_HEAD_DIM:(hd + 1) * A_HEAD_DIM]
            y = (xh * cos_t + pltpu.roll(xh, half, axis=1) * sin_up
                 + pltpu.roll(xh, A_HEAD_DIM - half, axis=1) * sin_dn)
            o_ref[0, 0, :, hd * A_HEAD_DIM:(hd + 1) * A_HEAD_DIM] = (y * scale).astype(BF16)

    @pl.when(t == 2)
    def _():
        o_ref[0, 0] = acc.astype(BF16)


def _aproj(h, w_g, tabs_g, dil):
    ls = SEQ // dil
    h_view = h.reshape(BATCH, ls, dil * D_MODEL)
    scratch = [pltpu.VMEM((SEQ, D_MODEL), BF16)] if dil > 1 else [pltpu.VMEM((8, LANES), BF16)]
    return pl.pallas_call(
        functools.partial(_aproj_kernel, dil),
        grid=(BATCH, 3),
        in_specs=[
            pl.BlockSpec((1, ls, dil * D_MODEL), lambda b, t: (b, 0, 0)),
            pl.BlockSpec((1, D_MODEL, A_OUT_WIDTH), lambda b, t: (t, 0, 0)),
            pl.BlockSpec((3, SEQ, LANES), lambda b, t: (0, 0, 0)),
        ],
        out_specs=pl.BlockSpec((1, 1, SEQ, A_OUT_WIDTH), lambda b, t: (b, t, 0, 0)),
        out_shape=jax.ShapeDtypeStruct((BATCH, 3, SEQ, A_OUT_WIDTH), BF16),
        scratch_shapes=scratch,
        compiler_params=pltpu.CompilerParams(dimension_semantics=("parallel", "arbitrary"),
                                             vmem_limit_bytes=VMEM_LIMIT),
        name="aproj_d%d" % dil,
    )(h_view, w_g, tabs_g)


def _lane_scan(x, op, fill):
    lane = lax.broadcasted_iota(jnp.int32, x.shape, 1)
    sh = 1
    while sh < x.shape[1]:
        shifted = jnp.where(lane >= sh, pltpu.roll(x, sh, axis=1), fill)
        x = op(x, shifted)
        sh *= 2
    return x


def _mlstm_kernel(q_ref, k_ref, kt_ref, v_ref, o_ref, z_ref, ig_ref, fg_ref, nw_ref, out_ref,
                  c_ref, n_ref, u_ref, g_ref, mt_ref, mp_ref, w_ref, dec_ref, bt_ref, um_ref):
    L = M_CHUNK
    ig = ig_ref[0, 0]
    fg = fg_ref[0, 0]
    logf = -(jnp.maximum(-fg, 0.0) + jnp.log1p(jnp.exp(-jnp.abs(fg))))
    b = _lane_scan(logf, jnp.add, 0.0)
    u = ig - b
    cm = _lane_scan(u, jnp.maximum, NEG_INF)
    lane = lax.broadcasted_iota(jnp.int32, (N_CHUNKS, L), 1)
    btot = jnp.sum(jnp.where(lane == L - 1, b, 0.0), axis=1, keepdims=True)
    umax = jnp.max(cm, axis=1, keepdims=True)
    bt_ref[...] = jnp.broadcast_to(btot, (N_CHUNKS, L))
    um_ref[...] = jnp.broadcast_to(umax, (N_CHUNKS, L))
    m = jnp.zeros((1, L), F32)
    for c in range(N_CHUNKS):
        mp_ref[c:c + 1, :] = m
        m = bt_ref[c:c + 1, :] + jnp.maximum(m, um_ref[c:c + 1, :])
    mprev = mp_ref[...]
    g = jnp.maximum(mprev, cm)
    glast = jnp.max(g, axis=1, keepdims=True)
    u_ref[...] = u
    g_ref[...] = g
    mt_ref[...] = b + g
    w_ref[...] = jnp.exp(u - glast)
    dec_ref[...] = jnp.broadcast_to(jnp.exp(mprev - glast), (N_CHUNKS, L))

    c_ref[...] = jnp.zeros_like(c_ref)
    n_ref[...] = jnp.zeros_like(n_ref)
    ti = lax.broadcasted_iota(jnp.int32, (L, L), 0)
    si = lax.broadcasted_iota(jnp.int32, (L, L), 1)
    causal = ti >= si
    nw = nw_ref[...]

    def chunk(c, carry):
        r0 = pl.multiple_of(c * L, L)
        qc = q_ref[pl.ds(r0, L), :]
        kc = k_ref[pl.ds(r0, L), :]
        vc = v_ref[pl.ds(r0, L), :]
        ktc = kt_ref[0, 0, c]
        u_r = u_ref[pl.ds(c, 1), :]
        g_r = g_ref[pl.ds(c, 1), :]
        mt_r = mt_ref[pl.ds(c, 1), :]
        mp_r = mp_ref[pl.ds(c, 1), :]
        w_r = w_ref[pl.ds(c, 1), :]
        dec_r = dec_ref[pl.ds(c, 1), :]
        gcol = jnp.broadcast_to(g_r, (LANES, L)).T
        mcol = jnp.broadcast_to(mt_r, (LANES, L)).T
        gcol2 = jnp.concatenate([gcol, gcol], axis=1)
        dmat = jnp.where(causal, jnp.exp(u_r - gcol2), 0.0)
        inter = jnp.exp(mp_r[:, :LANES] - gcol)
        inter2 = jnp.concatenate([inter, inter], axis=1)

        s = jnp.dot(qc, ktc, preferred_element_type=F32) * dmat
        rowsum = jnp.sum(s, axis=1, keepdims=True)
        cb = c_ref[...].astype(BF16)
        num = inter2 * jnp.dot(qc, cb, preferred_element_type=F32) \
            + jnp.dot(s.astype(BF16), vc, preferred_element_type=F32)
        n_row = n_ref[0:1, :]
        qn = jnp.sum(qc.astype(F32) * n_row, axis=1, keepdims=True)
        den = inter * qn + rowsum
        denom = jnp.maximum(jnp.abs(den), jnp.exp(-mcol))
        rinv = 1.0 / denom
        hval = num * jnp.concatenate([rinv, rinv], axis=1)

        hg = hval * o_ref[pl.ds(r0, L), :].astype(F32)
        ms = jnp.mean(hg * hg, axis=1, keepdims=True)
        hn = hg * lax.rsqrt(ms + NORM_EPS) * nw
        out_ref[pl.ds(r0, L), :] = (hn * z_ref[pl.ds(r0, L), :].astype(F32)).astype(BF16)

        ktw = (ktc.astype(F32) * w_r).astype(BF16)
        c_ref[...] = dec_r * c_ref[...] + jnp.dot(ktw, vc, preferred_element_type=F32)
        w8 = jnp.broadcast_to(w_r, (8, L)).astype(BF16)
        n_ref[...] = dec_r * n_ref[...] + jnp.dot(w8, kc, preferred_element_type=F32)
        return carry

    lax.fori_loop(0, N_CHUNKS, chunk, 0)


def _mlstm(mall, kt, gt, m_norm_w):
    gt4 = gt.reshape(2 * M_HEADS, BATCH, N_CHUNKS, M_CHUNK)
    blk = (SEQ, M_HEAD_DIM)

    def seg(k):
        return pl.BlockSpec(blk, lambda b, h, k=k: (b, k * M_HEADS + h))

    return pl.pallas_call(
        _mlstm_kernel,
        grid=(BATCH, M_HEADS),
        in_specs=[
            seg(0), seg(1),
            pl.BlockSpec((1, 1, N_CHUNKS, M_HEAD_DIM, M_CHUNK), lambda b, h: (b, h, 0, 0, 0)),
            seg(2), seg(3), seg(4),
            pl.BlockSpec((1, 1, N_CHUNKS, M_CHUNK), lambda b, h: (h, b, 0, 0)),
            pl.BlockSpec((1, 1, N_CHUNKS, M_CHUNK), lambda b, h: (M_HEADS + h, b, 0, 0)),
            pl.BlockSpec((1, M_HEAD_DIM), lambda b, h: (0, h)),
        ],
        out_specs=pl.BlockSpec(blk, lambda b, h: (b, h)),
        out_shape=jax.ShapeDtypeStruct((BATCH * SEQ, M_WIDTH), BF16),
        scratch_shapes=[pltpu.VMEM((M_HEAD_DIM, M_HEAD_DIM), F32),
                        pltpu.VMEM((8, M_HEAD_DIM), F32)]
        + [pltpu.VMEM((N_CHUNKS, M_CHUNK), F32) for _ in range(8)],
        compiler_params=pltpu.CompilerParams(dimension_semantics=("parallel", "parallel"),
                                             vmem_limit_bytes=VMEM_LIMIT),
        name="mlstm",
    )(mall, mall, kt, mall, mall, mall, gt4, gt4, m_norm_w.reshape(1, M_WIDTH))


def _attn_block(q, kcat, vcat, valid):
    s = lax.dot_general(q, kcat, (((1,), (1,)), ((), ())), preferred_element_type=F32)
    s = jnp.where(valid, s, NEG_INF)
    m = jnp.max(s, axis=1, keepdims=True)
    p = jnp.where(valid, jnp.exp(s - m), 0.0)
    l = jnp.sum(p, axis=1, keepdims=True)
    acc = jnp.dot(p.astype(BF16), vcat, preferred_element_type=F32)
    return acc, m, l


def _attn_kernel(q0, k0, v0, q1, k1, v1, q2, k2, v2, out_ref,
                 a0, m0, l0, a1, m1, l1, a2, m2, l2):
    BLK = A_BLOCK
    ti = lax.broadcasted_iota(jnp.int32, (BLK, BLK), 0)
    ji = lax.broadcasted_iota(jnp.int32, (BLK, BLK), 1)
    cur_ok = ji <= ti
    ti2 = lax.broadcasted_iota(jnp.int32, (BLK, 2 * BLK), 0)
    krel = lax.broadcasted_iota(jnp.int32, (BLK, 2 * BLK), 1) - BLK

    qs = (q0, q1, q2)
    ks = (k0, k1, k2)
    vs = (v0, v1, v2)
    accs = (a0, a1, a2)
    ms = (m0, m1, m2)
    ls = (l0, l1, l2)

    def body(i, carry):
        for g, (_, dil) in enumerate(A_GROUPS):
            nblk = SEQ // dil // BLK
            if nblk > 1:
                r = i // nblk
                n = i % nblk
            else:
                r = i
                n = 0
            row0 = pl.multiple_of(i * BLK, BLK)
            q = qs[g][0, 0, pl.ds(row0, BLK), :]
            kc = ks[g][0, 0, pl.ds(row0, BLK), :]
            vc = vs[g][0, 0, pl.ds(row0, BLK), :]
            if nblk > 1:
                prow = pl.multiple_of(jnp.maximum(i - 1, 0) * BLK, BLK)
                kp = ks[g][0, 0, pl.ds(prow, BLK), :]
                vp = vs[g][0, 0, pl.ds(prow, BLK), :]
                kcat = jnp.concatenate([kp, kc], axis=0)
                vcat = jnp.concatenate([vp, vc], axis=0)
                lo = (ti2 - BLK) * (n > 0).astype(jnp.int32)
                valid = jnp.logical_and(krel <= ti2, krel >= lo)
            else:
                kcat, vcat, valid = kc, vc, cur_ok
            acc, m, l = _attn_block(q, kcat, vcat, valid)
            tok0 = n * (BLK * dil) + r
            if dil > 1:
                idx = pl.ds(tok0, BLK, stride=dil)
            else:
                idx = pl.ds(pl.multiple_of(tok0, BLK), BLK)
            accs[g][idx, :] = acc
            ms[g][idx, :] = jnp.broadcast_to(m, (BLK, LANES))
            ls[g][idx, :] = jnp.broadcast_to(l, (BLK, LANES))
        return carry

    lax.fori_loop(0, SEQ // BLK, body, 0)

    RB = 256
    for rb in range(SEQ // RB):
        sl = slice(rb * RB, (rb + 1) * RB)
        mg = [ms[g][sl, :] for g in range(N_GROUPS)]
        lg = [ls[g][sl, :] for g in range(N_GROUPS)]
        mmax = jnp.maximum(jnp.maximum(mg[0], mg[1]), mg[2])
        num = jnp.zeros((RB, LANES), F32)
        den = jnp.zeros((RB, LANES), F32)
        for g in range(N_GROUPS):
            wt = jnp.exp(mg[g] - mmax) * lg[g]
            o = accs[g][sl, :] / jnp.where(lg[g] > 0, lg[g], 1.0)
            num = num + wt * o
            den = den + wt
        out_ref[sl, :] = (num / den).astype(BF16)


def _attn(qkv):
    in_specs = []
    args = []
    for g in range(N_GROUPS):
        for t in range(3):
            in_specs.append(pl.BlockSpec((1, 1, SEQ, A_HEAD_DIM), lambda b, h, t=t: (b, t, 0, h)))
            args.append(qkv[g])
    return pl.pallas_call(
        _attn_kernel,
        grid=(BATCH, A_HEADS),
        in_specs=in_specs,
        out_specs=pl.BlockSpec((SEQ, A_HEAD_DIM), lambda b, h: (b, h)),
        out_shape=jax.ShapeDtypeStruct((BATCH * SEQ, A_OUT_WIDTH), BF16),
        scratch_shapes=[pltpu.VMEM((SEQ, LANES), F32) for _ in range(9)],
        compiler_params=pltpu.CompilerParams(dimension_semantics=("parallel", "parallel"),
                                             vmem_limit_bytes=VMEM_LIMIT),
        name="attn",
    )(*args)


O_TM = 512


def _out_kernel(x_ref, prew_ref, hm_ref, att_ref, wg_ref, wpm_ref, wpa_ref, wout_ref, postw_ref,
                y_ref):
    x = x_ref[...]
    ms = jnp.mean(x * x, axis=-1, keepdims=True)
    h = (x * lax.rsqrt(ms + NORM_EPS) * prew_ref[...]).astype(BF16)
    gates = jnp.dot(h, wg_ref[...], preferred_element_type=F32)
    za = _silu(gates[:, :A_OUT_WIDTH])
    sgm = _sigmoid(gates[:, A_OUT_WIDTH:A_OUT_WIDTH + D_MODEL])
    sga = _sigmoid(gates[:, A_OUT_WIDTH + D_MODEL:])
    a = (att_ref[...].astype(F32) * za).astype(BF16)
    bm = jnp.dot(hm_ref[...], wpm_ref[...], preferred_element_type=F32)
    ba = jnp.dot(a, wpa_ref[...], preferred_element_type=F32)
    merged = (sgm * bm + sga * ba).astype(BF16)
    y = jnp.dot(merged, wout_ref[...], preferred_element_type=F32)
    ms2 = jnp.mean(y * y, axis=-1, keepdims=True)
    y_ref[...] = x + y * lax.rsqrt(ms2 + NORM_EPS) * postw_ref[...]


def _out(x2d, pre_w, hm, att, wg4, wpm, wpa, wout, post_w):
    n = x2d.shape[0]
    const = lambda i: (0, 0)
    return pl.pallas_call(
        _out_kernel,
        grid=(n // O_TM,),
        in_specs=[
            pl.BlockSpec((O_TM, D_MODEL), lambda i: (i, 0)),
            pl.BlockSpec((1, D_MODEL), const),
            pl.BlockSpec((O_TM, M_WIDTH), lambda i: (i, 0)),
            pl.BlockSpec((O_TM, A_OUT_WIDTH), lambda i: (i, 0)),
            pl.BlockSpec((D_MODEL, A_OUT_WIDTH + 2 * D_MODEL), const),
            pl.BlockSpec((M_WIDTH, D_MODEL), const),
            pl.BlockSpec((A_OUT_WIDTH, D_MODEL), const),
            pl.BlockSpec((D_MODEL, D_MODEL), const),
            pl.BlockSpec((1, D_MODEL), const),
        ],
        out_specs=pl.BlockSpec((O_TM, D_MODEL), lambda i: (i, 0)),
        out_shape=jax.ShapeDtypeStruct((n, D_MODEL), F32),
        compiler_params=pltpu.CompilerParams(dimension_semantics=("parallel",),
                                             vmem_limit_bytes=VMEM_LIMIT),
        name="outproj",
    )(x2d, pre_w.reshape(1, D_MODEL), hm, att, wg4, wpm, wpa, wout, post_w.reshape(1, D_MODEL))


def _rope_tables():
    pos = jnp.arange(SEQ, dtype=F32)
    inv_freq = ROPE_THETA ** (-jnp.arange(0, ROPE_DIM, 2, dtype=F32) / ROPE_DIM)
    ang = pos[:, None] * inv_freq[None, :]
    cos, sin = jnp.cos(ang), jnp.sin(ang)
    half = ROPE_DIM // 2
    pad = A_HEAD_DIM - ROPE_DIM
    cos_t = jnp.concatenate([cos, cos, jnp.ones((SEQ, pad), F32)], axis=1)
    sin_up = jnp.concatenate([jnp.zeros((SEQ, half), F32), sin, jnp.zeros((SEQ, pad), F32)], axis=1)
    sin_dn = jnp.concatenate([-sin, jnp.zeros((SEQ, half + pad), F32)], axis=1)
    tab = jnp.stack([cos_t, sin_up, sin_dn])
    out = []
    for _, dil in A_GROUPS:
        ls = SEQ // dil
        out.append(tab.reshape(3, ls, dil, A_HEAD_DIM).transpose(0, 2, 1, 3).reshape(3, SEQ, A_HEAD_DIM))
    return out


def _layer(x, pre_w, w_in, b_if, conv_w, conv_b, m_norm_w, w_pm, w_pa, w_out, post_w):
    x2d = x.reshape(BATCH * SEQ, D_MODEL)
    wg = jnp.pad(w_in[:, OFF_GATES:OFF_QA], ((0, 0), (0, LANES - 2 * M_HEADS))).astype(BF16)
    w_att = []
    for g in range(N_GROUPS):
        cols = [w_in[:, off + g * A_OUT_WIDTH: off + (g + 1) * A_OUT_WIDTH] for off in (OFF_QA, OFF_KA, OFF_VA)]
        w_att.append(jnp.stack(cols).astype(BF16))
    wg4 = w_in[:, OFF_ZA:].astype(BF16)
    tabs = _rope_tables()

    h = _norm(x2d, pre_w)
    mall, kt, gt = _mproj(h, w_in, wg, b_if.reshape(2 * M_HEADS, 1), conv_w, conv_b.reshape(1, 2 * M_WIDTH))
    qkv = [_aproj(h, w_att[g], tabs[g], A_GROUPS[g][1]) for g in range(N_GROUPS)]
    hm = _mlstm(mall, kt, gt, m_norm_w)
    att = _attn(qkv)
    y = _out(x2d, pre_w, hm, att, wg4, w_pm.astype(BF16), w_pa.astype(BF16), w_out.astype(BF16), post_w)
    return y.reshape(BATCH, SEQ, D_MODEL)


@jax.jit
def kernel(x, pre_norm_w, w_in, b_if, conv_w, conv_b, m_norm_w, w_proj_m, w_proj_a, w_out, post_norm_w):
    for layer in range(pre_norm_w.shape[0]):
        x = _layer(x, pre_norm_w[layer], w_in[layer], b_if[layer], conv_w[layer], conv_b[layer],
                   m_norm_w[layer], w_proj_m[layer], w_proj_a[layer], w_out[layer], post_norm_w[layer])
    return x
```

```python
import functools

import jax
import jax.numpy as jnp
from jax import lax
from jax.experimental import pallas as pl
from jax.experimental.pallas import tpu as pltpu

D_MODEL = 1024
BATCH = 8
SEQ = 2048
M_WIDTH = 1024
M_HEADS = 4
M_HEAD_DIM = 256
CONV_WIDTH = 4
A_GROUPS = ((128, 1), (512, 4), (2048, 16))
N_GROUPS = 3
A_HEADS = 4
A_HEAD_DIM = 128
A_OUT_WIDTH = 512
A_QKV_WIDTH = 1536
A_BLOCK = 128
ROPE_DIM = 32
ROPE_THETA = 500000.0
NORM_EPS = 1e-6
NEG_INF = -1e30

OFF_GATES = 5 * M_WIDTH
OFF_QA = OFF_GATES + 2 * M_HEADS
OFF_KA = OFF_QA + A_QKV_WIDTH
OFF_VA = OFF_KA + A_QKV_WIDTH
OFF_ZA = OFF_VA + A_QKV_WIDTH
OFF_GM = OFF_ZA + A_OUT_WIDTH
IN_WIDTH = OFF_GM + 2 * D_MODEL

LANES = 128
M_CHUNK = 256
N_CHUNKS = SEQ // M_CHUNK
VMEM_LIMIT = 48 * 1024 * 1024

F32 = jnp.float32
BF16 = jnp.bfloat16


def _sigmoid(x):
    return 1.0 / (1.0 + jnp.exp(-x))


def _silu(x):
    return x * _sigmoid(x)


NORM_TM = 1024
DILATED = tuple(d for _, d in A_GROUPS if d > 1)


def _norm_kernel(x_ref, w_ref, h_ref, h4_ref, h16_ref, hs_ref):
    x = x_ref[...]
    ms = jnp.mean(x * x, axis=-1, keepdims=True)
    hf = x * lax.rsqrt(ms + NORM_EPS) * w_ref[...]
    h_ref[...] = hf.astype(BF16)
    nslab = D_MODEL // LANES
    for c in range(nslab):
        hs_ref[c] = hf[:, c * LANES:(c + 1) * LANES]
    for dil, o_ref in zip(DILATED, (h4_ref, h16_ref)):
        rows = NORM_TM // dil
        for r in range(dil):
            for c in range(nslab):
                o_ref[0, r, :, c * LANES:(c + 1) * LANES] = \
                    hs_ref[c, pl.ds(r, rows, stride=dil), :].astype(BF16)


def _norm(x2d, pre_w):
    halves = SEQ // NORM_TM
    out_shape = [jax.ShapeDtypeStruct((BATCH * SEQ, D_MODEL), BF16)]
    out_specs = [pl.BlockSpec((NORM_TM, D_MODEL), lambda b, s: (b * halves + s, 0))]
    for dil in DILATED:
        ls = SEQ // dil
        out_shape.append(jax.ShapeDtypeStruct((BATCH, dil, ls, D_MODEL), BF16))
        out_specs.append(pl.BlockSpec((1, dil, ls // halves, D_MODEL), lambda b, s: (b, 0, s, 0)))
    return pl.pallas_call(
        _norm_kernel,
        grid=(BATCH, halves),
        in_specs=[pl.BlockSpec((NORM_TM, D_MODEL), lambda b, s: (b * halves + s, 0)),
                  pl.BlockSpec((1, D_MODEL), lambda b, s: (0, 0))],
        out_specs=out_specs,
        out_shape=out_shape,
        scratch_shapes=[pltpu.VMEM((D_MODEL // LANES, NORM_TM, LANES), F32)],
        compiler_params=pltpu.CompilerParams(dimension_semantics=("parallel", "parallel"),
                                             vmem_limit_bytes=VMEM_LIMIT),
        name="norm",
    )(x2d, pre_w.reshape(1, D_MODEL))


M_TN = 512


SUBLANES = 8


def _causal_conv(acc, cw, cb):
    def taps(x, shifted):
        y = x * cw[CONV_WIDTH - 1:CONV_WIDTH, :] + cb
        for s in range(1, CONV_WIDTH):
            y = y + shifted(x, s) * cw[CONV_WIDTH - 1 - s:CONV_WIDTH - s, :]
        return y

    body = taps(acc, lambda x, s: pltpu.roll(x, s, axis=0))
    top = acc[:SUBLANES, :]
    rows = lax.broadcasted_iota(jnp.int32, top.shape, 0)
    head = taps(top, lambda x, s: jnp.where(rows >= s, pltpu.roll(x, s, axis=0), 0.0))
    return jnp.concatenate([head, body[SUBLANES:, :]], axis=0)


def _mproj_kernel(h_ref, w_ref, wg_ref, bif_ref, cw_ref, cb_ref, mall_ref, kt_ref, gt_ref):
    j = pl.program_id(1)

    def matmul():
        return jnp.dot(h_ref[...], w_ref[...].astype(BF16), preferred_element_type=F32)

    @pl.when(j == 0)
    def _():
        g = jnp.dot(h_ref[...], wg_ref[...], preferred_element_type=F32)
        gt = g.T
        gt_ref[...] = gt[:2 * M_HEADS, :] + bif_ref[...]

    @pl.when(j < 2)
    def _():
        y = _causal_conv(matmul(), cw_ref[...], cb_ref[...])
        mall_ref[...] = _silu(y).astype(BF16)

    @pl.when(jnp.logical_and(j >= 2, j < 4))
    def _():
        y = _causal_conv(matmul(), cw_ref[...], cb_ref[...])
        y = _silu(y) * (M_HEAD_DIM ** -0.5)
        mall_ref[...] = y.astype(BF16)
        for hh in range(M_TN // M_HEAD_DIM):
            for c in range(N_CHUNKS):
                blk = y[c * M_CHUNK:(c + 1) * M_CHUNK, hh * M_HEAD_DIM:(hh + 1) * M_HEAD_DIM]
                kt_ref[0, hh, c] = blk.T.astype(BF16)

    @pl.when(jnp.logical_and(j >= 4, j < 6))
    def _():
        mall_ref[...] = matmul().astype(BF16)

    @pl.when(jnp.logical_and(j >= 6, j < 8))
    def _():
        mall_ref[...] = _sigmoid(matmul()).astype(BF16)

    @pl.when(j >= 8)
    def _():
        mall_ref[...] = _silu(matmul()).astype(BF16)


def _mproj(h, w_in, wg, bif, conv_w, conv_b):
    n_tiles = 5 * M_WIDTH // M_TN
    kt_tiles_lo = 2
    return pl.pallas_call(
        _mproj_kernel,
        grid=(BATCH, n_tiles),
        in_specs=[
            pl.BlockSpec((SEQ, D_MODEL), lambda i, j: (i, 0)),
            pl.BlockSpec((D_MODEL, M_TN), lambda i, j: (0, j)),
            pl.BlockSpec((D_MODEL, LANES), lambda i, j: (0, 0)),
            pl.BlockSpec((2 * M_HEADS, 1), lambda i, j: (0, 0)),
            pl.BlockSpec((CONV_WIDTH, M_TN), lambda i, j: (0, jnp.minimum(j, 3))),
            pl.BlockSpec((1, M_TN), lambda i, j: (0, jnp.minimum(j, 3))),
        ],
        out_specs=[
            pl.BlockSpec((SEQ, M_TN), lambda i, j: (i, j)),
            pl.BlockSpec((1, M_TN // M_HEAD_DIM, N_CHUNKS, M_HEAD_DIM, M_CHUNK),
                         lambda i, j: (i, jnp.clip(j - kt_tiles_lo, 0, 1), 0, 0, 0)),
            pl.BlockSpec((2 * M_HEADS, SEQ), lambda i, j: (0, i)),
        ],
        out_shape=[
            jax.ShapeDtypeStruct((BATCH * SEQ, 5 * M_WIDTH), BF16),
            jax.ShapeDtypeStruct((BATCH, M_HEADS, N_CHUNKS, M_HEAD_DIM, M_CHUNK), BF16),
            jax.ShapeDtypeStruct((2 * M_HEADS, BATCH * SEQ), F32),
        ],
        compiler_params=pltpu.CompilerParams(dimension_semantics=("parallel", "arbitrary"),
                                             vmem_limit_bytes=VMEM_LIMIT),
        name="mproj",
    )(h, w_in, wg, bif, conv_w, conv_b)


def _aproj_kernel(h_ref, w_ref, tab_ref, o_ref):
    t = pl.program_id(1)

    def matmul():
        return jnp.dot(h_ref[...], w_ref[0], preferred_element_type=F32)

    @pl.when(t < 2)
    def _():
        acc = matmul()
        cos_t = tab_ref[0]
        sin_up = tab_ref[1]
        sin_dn = tab_ref[2]
        scale = jnp.where(t == 0, A_HEAD_DIM ** -0.5, 1.0).astype(F32)
        half = ROPE_DIM // 2
        for hd in range(A_HEADS):
            xh = acc[:, hd * A_HEAD_DIM:(hd + 1) * A_HEAD_DIM]
            y = (xh * cos_t + pltpu.roll(xh, half, axis=1) * sin_up
                 + pltpu.roll(xh, A_HEAD_DIM - half, axis=1) * sin_dn)
            o_ref[0, 0, :, hd * A_HEAD_DIM:(hd + 1) * A_HEAD_DIM] = (y * scale).astype(BF16)

    @pl.when(t == 2)
    def _():
        o_ref[0, 0] = matmul().astype(BF16)


def _aproj(h, w_g, tabs_g, name):
    return pl.pallas_call(
        _aproj_kernel,
        grid=(BATCH, 3),
        in_specs=[
            pl.BlockSpec((SEQ, D_MODEL), lambda b, t: (b, 0)),
            pl.BlockSpec((1, D_MODEL, A_OUT_WIDTH), lambda b, t: (t, 0, 0)),
            pl.BlockSpec((3, SEQ, LANES), lambda b, t: (0, 0, 0)),
        ],
        out_specs=pl.BlockSpec((1, 1, SEQ, A_OUT_WIDTH), lambda b, t: (b, t, 0, 0)),
        out_shape=jax.ShapeDtypeStruct((BATCH, 3, SEQ, A_OUT_WIDTH), BF16),
        compiler_params=pltpu.CompilerParams(dimension_semantics=("parallel", "arbitrary"),
                                             vmem_limit_bytes=VMEM_LIMIT),
        name=name,
    )(h, w_g, tabs_g)


def _lane_scan(x, op, fill):
    lane = lax.broadcasted_iota(jnp.int32, x.shape, 1)
    sh = 1
    while sh < x.shape[1]:
        shifted = jnp.where(lane >= sh, pltpu.roll(x, sh, axis=1), fill)
        x = op(x, shifted)
        sh *= 2
    return x


def _mlstm_kernel(q_ref, k_ref, kt_ref, v_ref, o_ref, z_ref, ig_ref, fg_ref, nw_ref, out_ref,
                  u_ref, g_ref, mt_ref, mp_ref, w_ref, dec_ref, bt_ref, um_ref):
    L = M_CHUNK
    ig = ig_ref[0, 0]
    fg = fg_ref[0, 0]
    logf = -(jnp.maximum(-fg, 0.0) + jnp.log1p(jnp.exp(-jnp.abs(fg))))
    b = _lane_scan(logf, jnp.add, 0.0)
    u = ig - b
    cm = _lane_scan(u, jnp.maximum, NEG_INF)
    lane = lax.broadcasted_iota(jnp.int32, (N_CHUNKS, L), 1)
    btot = jnp.sum(jnp.where(lane == L - 1, b, 0.0), axis=1, keepdims=True)
    umax = jnp.max(cm, axis=1, keepdims=True)
    bt_ref[...] = jnp.broadcast_to(btot, (N_CHUNKS, L))
    um_ref[...] = jnp.broadcast_to(umax, (N_CHUNKS, L))
    m = jnp.zeros((1, L), F32)
    for c in range(N_CHUNKS):
        mp_ref[c:c + 1, :] = m
        m = bt_ref[c:c + 1, :] + jnp.maximum(m, um_ref[c:c + 1, :])
    mprev = mp_ref[...]
    g = jnp.maximum(mprev, cm)
    glast = jnp.max(g, axis=1, keepdims=True)
    u_ref[...] = u
    g_ref[...] = g
    mt_ref[...] = b + g
    w_ref[...] = jnp.exp(u - glast)
    dec_ref[...] = jnp.broadcast_to(jnp.exp(mprev - glast), (N_CHUNKS, L))

    ti = lax.broadcasted_iota(jnp.int32, (L, L), 0)
    si = lax.broadcasted_iota(jnp.int32, (L, L), 1)
    causal = ti >= si
    nw = nw_ref[...]

    cmat = jnp.zeros((M_HEAD_DIM, M_HEAD_DIM), F32)
    nvec = jnp.zeros((SUBLANES, M_HEAD_DIM), F32)
    for c in range(N_CHUNKS):
        rs = slice(c * L, (c + 1) * L)
        cs = slice(c, c + 1)
        qc = q_ref[rs, :]
        kc = k_ref[rs, :]
        vc = v_ref[rs, :]
        ktc = kt_ref[0, 0, c]
        u_r = u_ref[cs, :]
        g_r = g_ref[cs, :]
        mt_r = mt_ref[cs, :]
        mp_r = mp_ref[cs, :]
        w_r = w_ref[cs, :]
        dec_r = dec_ref[cs, :]
        gcol = jnp.broadcast_to(g_r, (LANES, L)).T
        mcol = jnp.broadcast_to(mt_r, (LANES, L)).T
        gcol2 = jnp.concatenate([gcol, gcol], axis=1)
        dmat = jnp.where(causal, jnp.exp(u_r - gcol2), 0.0)
        inter = jnp.exp(mp_r[:, :LANES] - gcol)
        inter2 = jnp.concatenate([inter, inter], axis=1)

        s = jnp.dot(qc, ktc, preferred_element_type=F32) * dmat
        rowsum = jnp.sum(s, axis=1, keepdims=True)
        num = jnp.dot(s.astype(BF16), vc, preferred_element_type=F32)
        den = rowsum
        if c > 0:
            num = num + inter2 * jnp.dot(qc, cmat.astype(BF16), preferred_element_type=F32)
            qn = jnp.sum(qc.astype(F32) * nvec[0:1, :], axis=1, keepdims=True)
            den = den + inter * qn
        denom = jnp.maximum(jnp.abs(den), jnp.exp(-mcol))
        rinv = 1.0 / denom
        hval = num * jnp.concatenate([rinv, rinv], axis=1)

        hg = hval * o_ref[rs, :].astype(F32)
        ms = jnp.mean(hg * hg, axis=1, keepdims=True)
        hn = hg * lax.rsqrt(ms + NORM_EPS) * nw
        out_ref[rs, :] = (hn * z_ref[rs, :].astype(F32)).astype(BF16)

        if c + 1 < N_CHUNKS:
            ktw = (ktc.astype(F32) * w_r).astype(BF16)
            cmat = dec_r * cmat + jnp.dot(ktw, vc, preferred_element_type=F32)
            w8 = jnp.broadcast_to(w_r, (SUBLANES, L)).astype(BF16)
            nvec = dec_r * nvec + jnp.dot(w8, kc, preferred_element_type=F32)


def _mlstm(mall, kt, gt, m_norm_w):
    gt4 = gt.reshape(2 * M_HEADS, BATCH, N_CHUNKS, M_CHUNK)
    blk = (SEQ, M_HEAD_DIM)

    def seg(k):
        return pl.BlockSpec(blk, lambda b, h, k=k: (b, k * M_HEADS + h))

    return pl.pallas_call(
        _mlstm_kernel,
        grid=(BATCH, M_HEADS),
        in_specs=[
            seg(0), seg(1),
            pl.BlockSpec((1, 1, N_CHUNKS, M_HEAD_DIM, M_CHUNK), lambda b, h: (b, h, 0, 0, 0)),
            seg(2), seg(3), seg(4),
            pl.BlockSpec((1, 1, N_CHUNKS, M_CHUNK), lambda b, h: (h, b, 0, 0)),
            pl.BlockSpec((1, 1, N_CHUNKS, M_CHUNK), lambda b, h: (M_HEADS + h, b, 0, 0)),
            pl.BlockSpec((1, M_HEAD_DIM), lambda b, h: (0, h)),
        ],
        out_specs=pl.BlockSpec(blk, lambda b, h: (b, h)),
        out_shape=jax.ShapeDtypeStruct((BATCH * SEQ, M_WIDTH), BF16),
        scratch_shapes=[pltpu.VMEM((N_CHUNKS, M_CHUNK), F32) for _ in range(8)],
        compiler_params=pltpu.CompilerParams(dimension_semantics=("parallel", "parallel"),
                                             vmem_limit_bytes=VMEM_LIMIT),
        name="mlstm",
    )(mall, mall, kt, mall, mall, mall, gt4, gt4, m_norm_w.reshape(1, M_WIDTH))


ATTN_WAVE = 12


def _attn_kernel(q0, k0, v0, q1, k1, v1, q2, k2, v2, out_ref,
                 a0, m0, l0, a1, m1, l1, a2, m2, l2):
    BLK = A_BLOCK
    ti = lax.broadcasted_iota(jnp.int32, (BLK, BLK), 0)
    ji = lax.broadcasted_iota(jnp.int32, (BLK, BLK), 1)
    cur_ok = ji <= ti
    ti2 = lax.broadcasted_iota(jnp.int32, (BLK, 2 * BLK), 0)
    krel = lax.broadcasted_iota(jnp.int32, (BLK, 2 * BLK), 1) - BLK
    band_ok = jnp.logical_and(krel <= ti2, krel >= ti2 - BLK)

    qs = (q0, q1, q2)
    ks = (k0, k1, k2)
    vs = (v0, v1, v2)
    outs = (a0, a1, a2)
    ms = (m0, m1, m2)
    ls = (l0, l1, l2)

    blocks = [(i, g) for i in range(SEQ // BLK) for g in range(N_GROUPS)]
    for w0 in range(0, len(blocks), ATTN_WAVE):
        wave = []
        for i, g in blocks[w0:w0 + ATTN_WAVE]:
            dil = A_GROUPS[g][1]
            r, n = divmod(i, SEQ // dil // BLK)
            rows = slice(i * BLK, (i + 1) * BLK)
            keys = slice((i - 1) * BLK, (i + 1) * BLK) if n > 0 else rows
            valid = band_ok if n > 0 else cur_ok
            s = lax.dot_general(qs[g][0, 0, rows, :], ks[g][0, 0, keys, :], (((1,), (1,)), ((), ())),
                                preferred_element_type=F32)
            s = jnp.where(valid, s, NEG_INF)
            m = jnp.max(s, axis=1, keepdims=True)
            tok0 = n * BLK * dil + r
            idx = pl.ds(tok0, BLK, stride=dil) if dil > 1 else pl.ds(tok0, BLK)
            wave.append((g, keys, valid, idx, s, m))
        probs = []
        for g, keys, valid, idx, s, m in wave:
            p = jnp.where(valid, jnp.exp(s - m), 0.0)
            probs.append((p, jnp.sum(p, axis=1, keepdims=True)))
        for (g, keys, valid, idx, s, m), (p, l) in zip(wave, probs):
            acc = jnp.dot(p.astype(BF16), vs[g][0, 0, keys, :], preferred_element_type=F32)
            outs[g][idx, :] = acc * (1.0 / jnp.where(l > 0, l, 1.0))
            ms[g][idx, :] = jnp.broadcast_to(m, (BLK, LANES))
            ls[g][idx, :] = jnp.broadcast_to(l, (BLK, LANES))

    RB = 256
    for rb in range(SEQ // RB):
        sl = slice(rb * RB, (rb + 1) * RB)
        mg = [ms[g][sl, :] for g in range(N_GROUPS)]
        mmax = jnp.maximum(jnp.maximum(mg[0], mg[1]), mg[2])
        num = None
        den = None
        for g in range(N_GROUPS):
            wt = jnp.exp(mg[g] - mmax) * ls[g][sl, :]
            num = wt * outs[g][sl, :] if num is None else num + wt * outs[g][sl, :]
            den = wt if den is None else den + wt
        out_ref[sl, :] = (num * (1.0 / den)).astype(BF16)


def _attn(qkv):
    in_specs = []
    args = []
    for g in range(N_GROUPS):
        for t in range(3):
            in_specs.append(pl.BlockSpec((1, 1, SEQ, A_HEAD_DIM), lambda b, h, t=t: (b, t, 0, h)))
            args.append(qkv[g])
    return pl.pallas_call(
        _attn_kernel,
        grid=(BATCH, A_HEADS),
        in_specs=in_specs,
        out_specs=pl.BlockSpec((SEQ, A_HEAD_DIM), lambda b, h: (b, h)),
        out_shape=jax.ShapeDtypeStruct((BATCH * SEQ, A_OUT_WIDTH), BF16),
        scratch_shapes=[pltpu.VMEM((SEQ, LANES), F32) for _ in range(9)],
        compiler_params=pltpu.CompilerParams(dimension_semantics=("parallel", "parallel"),
                                             vmem_limit_bytes=VMEM_LIMIT),
        name="attn",
    )(*args)


O_TM = 512


def _out_kernel(x_ref, prew_ref, hm_ref, att_ref, wg_ref, wpm_ref, wpa_ref, wout_ref, postw_ref,
                y_ref):
    x = x_ref[...]
    ms = jnp.mean(x * x, axis=-1, keepdims=True)
    h = (x * lax.rsqrt(ms + NORM_EPS) * prew_ref[...]).astype(BF16)
    gates = jnp.dot(h, wg_ref[...], preferred_element_type=F32)
    za = _silu(gates[:, :A_OUT_WIDTH])
    sgm = _sigmoid(gates[:, A_OUT_WIDTH:A_OUT_WIDTH + D_MODEL])
    sga = _sigmoid(gates[:, A_OUT_WIDTH + D_MODEL:])
    a = (att_ref[...].astype(F32) * za).astype(BF16)
    bm = jnp.dot(hm_ref[...], wpm_ref[...], preferred_element_type=F32)
    ba = jnp.dot(a, wpa_ref[...], preferred_element_type=F32)
    merged = (sgm * bm + sga * ba).astype(BF16)
    y = jnp.dot(merged, wout_ref[...], preferred_element_type=F32)
    ms2 = jnp.mean(y * y, axis=-1, keepdims=True)
    y_ref[...] = x + y * lax.rsqrt(ms2 + NORM_EPS) * postw_ref[...]


def _out(x2d, pre_w, hm, att, wg4, wpm, wpa, wout, post_w):
    n = x2d.shape[0]
    const = lambda i: (0, 0)
    return pl.pallas_call(
        _out_kernel,
        grid=(n // O_TM,),
        in_specs=[
            pl.BlockSpec((O_TM, D_MODEL), lambda i: (i, 0)),
            pl.BlockSpec((1, D_MODEL), const),
            pl.BlockSpec((O_TM, M_WIDTH), lambda i: (i, 0)),
            pl.BlockSpec((O_TM, A_OUT_WIDTH), lambda i: (i, 0)),
            pl.BlockSpec((D_MODEL, A_OUT_WIDTH + 2 * D_MODEL), const),
            pl.BlockSpec((M_WIDTH, D_MODEL), const),
            pl.BlockSpec((A_OUT_WIDTH, D_MODEL), const),
            pl.BlockSpec((D_MODEL, D_MODEL), const),
            pl.BlockSpec((1, D_MODEL), const),
        ],
        out_specs=pl.BlockSpec((O_TM, D_MODEL), lambda i: (i, 0)),
        out_shape=jax.ShapeDtypeStruct((n, D_MODEL), F32),
        compiler_params=pltpu.CompilerParams(dimension_semantics=("parallel",),
                                             vmem_limit_bytes=VMEM_LIMIT),
        name="outproj",
    )(x2d, pre_w.reshape(1, D_MODEL), hm, att, wg4, wpm, wpa, wout, post_w.reshape(1, D_MODEL))


def _rope_tables():
    pos = jnp.arange(SEQ, dtype=F32)
    inv_freq = ROPE_THETA ** (-jnp.arange(0, ROPE_DIM, 2, dtype=F32) / ROPE_DIM)
    ang = pos[:, None] * inv_freq[None, :]
    cos, sin = jnp.cos(ang), jnp.sin(ang)
    half = ROPE_DIM // 2
    pad = A_HEAD_DIM - ROPE_DIM
    cos_t = jnp.concatenate([cos, cos, jnp.ones((SEQ, pad), F32)], axis=1)
    sin_up = jnp.concatenate([jnp.zeros((SEQ, half), F32), sin, jnp.zeros((SEQ, pad), F32)], axis=1)
    sin_dn = jnp.concatenate([-sin, jnp.zeros((SEQ, half + pad), F32)], axis=1)
    tab = jnp.stack([cos_t, sin_up, sin_dn])
    out = []
    for _, dil in A_GROUPS:
        ls = SEQ // dil
        out.append(tab.reshape(3, ls, dil, A_HEAD_DIM).transpose(0, 2, 1, 3).reshape(3, SEQ, A_HEAD_DIM))
    return out


def _layer(x, pre_w, w_in, b_if, conv_w, conv_b, m_norm_w, w_pm, w_pa, w_out, post_w):
    x2d = x.reshape(BATCH * SEQ, D_MODEL)
    wg = jnp.pad(w_in[:, OFF_GATES:OFF_QA], ((0, 0), (0, LANES - 2 * M_HEADS))).astype(BF16)
    w_att = []
    for g in range(N_GROUPS):
        cols = [w_in[:, off + g * A_OUT_WIDTH: off + (g + 1) * A_OUT_WIDTH] for off in (OFF_QA, OFF_KA, OFF_VA)]
        w_att.append(jnp.stack(cols).astype(BF16))
    wg4 = w_in[:, OFF_ZA:].astype(BF16)
    tabs = _rope_tables()

    h, h4, h16 = _norm(x2d, pre_w)
    mall, kt, gt = _mproj(h, w_in, wg, b_if.reshape(2 * M_HEADS, 1), conv_w, conv_b.reshape(1, 2 * M_WIDTH))
    h_by_group = (h, h4.reshape(BATCH * SEQ, D_MODEL), h16.reshape(BATCH * SEQ, D_MODEL))
    qkv = [_aproj(h_by_group[g], w_att[g], tabs[g], "aproj_d%d" % A_GROUPS[g][1]) for g in range(N_GROUPS)]
    hm = _mlstm(mall, kt, gt, m_norm_w)
    att = _attn(qkv)
    y = _out(x2d, pre_w, hm, att, wg4, w_pm.astype(BF16), w_pa.astype(BF16), w_out.astype(BF16), post_w)
    return y.reshape(BATCH, SEQ, D_MODEL)


@jax.jit
def kernel(x, pre_norm_w, w_in, b_if, conv_w, conv_b, m_norm_w, w_proj_m, w_proj_a, w_out, post_norm_w):
    for layer in range(pre_norm_w.shape[0]):
        x = _layer(x, pre_norm_w[layer], w_in[layer], b_if[layer], conv_w[layer], conv_b[layer],
                   m_norm_w[layer], w_proj_m[layer], w_proj_a[layer], w_out[layer], post_norm_w[layer])
    return x
```

```python
import functools

import jax
import jax.numpy as jnp
from jax import lax
from jax.experimental import pallas as pl
from jax.experimental.pallas import tpu as pltpu

D_MODEL = 1024
BATCH = 8
SEQ = 2048
M_WIDTH = 1024
M_HEADS = 4
M_HEAD_DIM = 256
CONV_WIDTH = 4
A_GROUPS = ((128, 1), (512, 4), (2048, 16))
N_GROUPS = 3
A_HEADS = 4
A_HEAD_DIM = 128
A_OUT_WIDTH = 512
A_QKV_WIDTH = 1536
A_BLOCK = 128
ROPE_DIM = 32
ROPE_THETA = 500000.0
NORM_EPS = 1e-6
NEG_INF = -1e30

OFF_GATES = 5 * M_WIDTH
OFF_QA = OFF_GATES + 2 * M_HEADS
OFF_KA = OFF_QA + A_QKV_WIDTH
OFF_VA = OFF_KA + A_QKV_WIDTH
OFF_ZA = OFF_VA + A_QKV_WIDTH
OFF_GM = OFF_ZA + A_OUT_WIDTH
IN_WIDTH = OFF_GM + 2 * D_MODEL

LANES = 128
M_CHUNK = 256
N_CHUNKS = SEQ // M_CHUNK
VMEM_LIMIT = 48 * 1024 * 1024

F32 = jnp.float32
BF16 = jnp.bfloat16


def _dot_nt(a, bt):
    return lax.dot_general(a, bt, (((1,), (1,)), ((), ())), preferred_element_type=F32)


def _sigmoid(x):
    return 1.0 / (1.0 + jnp.exp(-x))


def _silu(x):
    return x * _sigmoid(x)


NORM_TM = 1024
DILATED = tuple(d for _, d in A_GROUPS if d > 1)


def _norm_kernel(x_ref, w_ref, h_ref, h4_ref, h16_ref, hs_ref):
    x = x_ref[...]
    ms = jnp.mean(x * x, axis=-1, keepdims=True)
    hf = x * lax.rsqrt(ms + NORM_EPS) * w_ref[...]
    h_ref[...] = hf.astype(BF16)
    nslab = D_MODEL // LANES
    for c in range(nslab):
        hs_ref[c] = hf[:, c * LANES:(c + 1) * LANES]
    for dil, o_ref in zip(DILATED, (h4_ref, h16_ref)):
        rows = NORM_TM // dil
        for r in range(dil):
            for c in range(nslab):
                o_ref[0, r, :, c * LANES:(c + 1) * LANES] = \
                    hs_ref[c, pl.ds(r, rows, stride=dil), :].astype(BF16)


def _norm(x2d, pre_w):
    halves = SEQ // NORM_TM
    out_shape = [jax.ShapeDtypeStruct((BATCH * SEQ, D_MODEL), BF16)]
    out_specs = [pl.BlockSpec((NORM_TM, D_MODEL), lambda b, s: (b * halves + s, 0))]
    for dil in DILATED:
        ls = SEQ // dil
        out_shape.append(jax.ShapeDtypeStruct((BATCH, dil, ls, D_MODEL), BF16))
        out_specs.append(pl.BlockSpec((1, dil, ls // halves, D_MODEL), lambda b, s: (b, 0, s, 0)))
    return pl.pallas_call(
        _norm_kernel,
        grid=(BATCH, halves),
        in_specs=[pl.BlockSpec((NORM_TM, D_MODEL), lambda b, s: (b * halves + s, 0)),
                  pl.BlockSpec((1, D_MODEL), lambda b, s: (0, 0))],
        out_specs=out_specs,
        out_shape=out_shape,
        scratch_shapes=[pltpu.VMEM((D_MODEL // LANES, NORM_TM, LANES), F32)],
        compiler_params=pltpu.CompilerParams(dimension_semantics=("parallel", "parallel"),
                                             vmem_limit_bytes=VMEM_LIMIT),
        name="norm",
    )(x2d, pre_w.reshape(1, D_MODEL))


M_TN = 512


SUBLANES = 8


def _causal_conv(acc, cw, cb):
    def taps(x, shifted):
        y = x * cw[CONV_WIDTH - 1:CONV_WIDTH, :] + cb
        for s in range(1, CONV_WIDTH):
            y = y + shifted(x, s) * cw[CONV_WIDTH - 1 - s:CONV_WIDTH - s, :]
        return y

    body = taps(acc, lambda x, s: pltpu.roll(x, s, axis=0))
    top = acc[:SUBLANES, :]
    rows = lax.broadcasted_iota(jnp.int32, top.shape, 0)
    head = taps(top, lambda x, s: jnp.where(rows >= s, pltpu.roll(x, s, axis=0), 0.0))
    return jnp.concatenate([head, body[SUBLANES:, :]], axis=0)


def _mproj_kernel(h_ref, w_ref, wg_ref, bif_ref, cw_ref, cb_ref, mall_ref, kt_ref, gt_ref):
    j = pl.program_id(1)

    def matmul():
        return _dot_nt(h_ref[...], w_ref[...])

    @pl.when(j == 0)
    def _():
        g = _dot_nt(h_ref[...], wg_ref[...])
        gt = g.T
        gt_ref[...] = gt[:2 * M_HEADS, :] + bif_ref[...]

    @pl.when(j < 2)
    def _():
        y = _causal_conv(matmul(), cw_ref[...], cb_ref[...])
        mall_ref[...] = _silu(y).astype(BF16)

    @pl.when(jnp.logical_and(j >= 2, j < 4))
    def _():
        y = _causal_conv(matmul(), cw_ref[...], cb_ref[...])
        y = _silu(y) * (M_HEAD_DIM ** -0.5)
        mall_ref[...] = y.astype(BF16)
        for hh in range(M_TN // M_HEAD_DIM):
            for c in range(N_CHUNKS):
                blk = y[c * M_CHUNK:(c + 1) * M_CHUNK, hh * M_HEAD_DIM:(hh + 1) * M_HEAD_DIM]
                kt_ref[0, hh, c] = blk.T.astype(BF16)

    @pl.when(jnp.logical_and(j >= 4, j < 6))
    def _():
        mall_ref[...] = matmul().astype(BF16)

    @pl.when(jnp.logical_and(j >= 6, j < 8))
    def _():
        mall_ref[...] = _sigmoid(matmul()).astype(BF16)

    @pl.when(j >= 8)
    def _():
        mall_ref[...] = _silu(matmul()).astype(BF16)


def _mproj(h, wt_m, wg, bif, conv_w, conv_b):
    n_tiles = 5 * M_WIDTH // M_TN
    kt_tiles_lo = 2
    return pl.pallas_call(
        _mproj_kernel,
        grid=(BATCH, n_tiles),
        in_specs=[
            pl.BlockSpec((SEQ, D_MODEL), lambda i, j: (i, 0)),
            pl.BlockSpec((M_TN, D_MODEL), lambda i, j: (j, 0)),
            pl.BlockSpec((LANES, D_MODEL), lambda i, j: (0, 0)),
            pl.BlockSpec((2 * M_HEADS, 1), lambda i, j: (0, 0)),
            pl.BlockSpec((CONV_WIDTH, M_TN), lambda i, j: (0, jnp.minimum(j, 3))),
            pl.BlockSpec((1, M_TN), lambda i, j: (0, jnp.minimum(j, 3))),
        ],
        out_specs=[
            pl.BlockSpec((SEQ, M_TN), lambda i, j: (i, j)),
            pl.BlockSpec((1, M_TN // M_HEAD_DIM, N_CHUNKS, M_HEAD_DIM, M_CHUNK),
                         lambda i, j: (i, jnp.clip(j - kt_tiles_lo, 0, 1), 0, 0, 0)),
            pl.BlockSpec((2 * M_HEADS, SEQ), lambda i, j: (0, i)),
        ],
        out_shape=[
            jax.ShapeDtypeStruct((BATCH * SEQ, 5 * M_WIDTH), BF16),
            jax.ShapeDtypeStruct((BATCH, M_HEADS, N_CHUNKS, M_HEAD_DIM, M_CHUNK), BF16),
            jax.ShapeDtypeStruct((2 * M_HEADS, BATCH * SEQ), F32),
        ],
        compiler_params=pltpu.CompilerParams(dimension_semantics=("parallel", "arbitrary"),
                                             vmem_limit_bytes=VMEM_LIMIT),
        name="mproj",
    )(h, wt_m, wg, bif, conv_w, conv_b)


def _aproj_kernel(h_ref, w_ref, tab_ref, o_ref):
    t = pl.program_id(1)

    def matmul():
        return _dot_nt(h_ref[...], w_ref[0])

    @pl.when(t < 2)
    def _():
        acc = matmul()
        cos_t = tab_ref[0]
        sin_up = tab_ref[1]
        sin_dn = tab_ref[2]
        scale = jnp.where(t == 0, A_HEAD_DIM ** -0.5, 1.0).astype(F32)
        half = ROPE_DIM // 2
        for hd in range(A_HEADS):
            xh = acc[:, hd * A_HEAD_DIM:(hd + 1) * A_HEAD_DIM]
            y = (xh * cos_t + pltpu.roll(xh, half, axis=1) * sin_up
                 + pltpu.roll(xh, A_HEAD_DIM - half, axis=1) * sin_dn)
            o_ref[0, 0, :, hd * A_HEAD_DIM:(hd + 1) * A_HEAD_DIM] = (y * scale).astype(BF16)

    @pl.when(t == 2)
    def _():
        o_ref[0, 0] = matmul().astype(BF16)


def _aproj(h, w_g, tabs_g, name):
    return pl.pallas_call(
        _aproj_kernel,
        grid=(BATCH, 3),
        in_specs=[
            pl.BlockSpec((SEQ, D_MODEL), lambda b, t: (b, 0)),
            pl.BlockSpec((1, A_OUT_WIDTH, D_MODEL), lambda b, t: (t, 0, 0)),
            pl.BlockSpec((3, SEQ, LANES), lambda b, t: (0, 0, 0)),
        ],
        out_specs=pl.BlockSpec((1, 1, SEQ, A_OUT_WIDTH), lambda b, t: (b, t, 0, 0)),
        out_shape=jax.ShapeDtypeStruct((BATCH, 3, SEQ, A_OUT_WIDTH), BF16),
        compiler_params=pltpu.CompilerParams(dimension_semantics=("parallel", "arbitrary"),
                                             vmem_limit_bytes=VMEM_LIMIT),
        name=name,
    )(h, w_g, tabs_g)


def _lane_scan(x, op, fill):
    lane = lax.broadcasted_iota(jnp.int32, x.shape, 1)
    sh = 1
    while sh < x.shape[1]:
        shifted = jnp.where(lane >= sh, pltpu.roll(x, sh, axis=1), fill)
        x = op(x, shifted)
        sh *= 2
    return x


def _mlstm_kernel(q_ref, k_ref, kt_ref, v_ref, o_ref, z_ref, ig_ref, fg_ref, nw_ref, out_ref,
                  u_ref, g_ref, mt_ref, mp_ref, w_ref, dec_ref, bt_ref, um_ref):
    L = M_CHUNK
    ig = ig_ref[0, 0]
    fg = fg_ref[0, 0]
    logf = -(jnp.maximum(-fg, 0.0) + jnp.log1p(jnp.exp(-jnp.abs(fg))))
    b = _lane_scan(logf, jnp.add, 0.0)
    u = ig - b
    cm = _lane_scan(u, jnp.maximum, NEG_INF)
    lane = lax.broadcasted_iota(jnp.int32, (N_CHUNKS, L), 1)
    btot = jnp.sum(jnp.where(lane == L - 1, b, 0.0), axis=1, keepdims=True)
    umax = jnp.max(cm, axis=1, keepdims=True)
    bt_ref[...] = jnp.broadcast_to(btot, (N_CHUNKS, L))
    um_ref[...] = jnp.broadcast_to(umax, (N_CHUNKS, L))
    m = jnp.zeros((1, L), F32)
    for c in range(N_CHUNKS):
        mp_ref[c:c + 1, :] = m
        m = bt_ref[c:c + 1, :] + jnp.maximum(m, um_ref[c:c + 1, :])
    mprev = mp_ref[...]
    g = jnp.maximum(mprev, cm)
    glast = jnp.max(g, axis=1, keepdims=True)
    u_ref[...] = u
    g_ref[...] = g
    mt_ref[...] = b + g
    w_ref[...] = jnp.exp(u - glast)
    dec_ref[...] = jnp.broadcast_to(jnp.exp(mprev - glast), (N_CHUNKS, L))

    ti = lax.broadcasted_iota(jnp.int32, (L, L), 0)
    si = lax.broadcasted_iota(jnp.int32, (L, L), 1)
    causal = ti >= si
    nw = nw_ref[...]

    cmat = jnp.zeros((M_HEAD_DIM, M_HEAD_DIM), F32)
    nvec = jnp.zeros((SUBLANES, M_HEAD_DIM), F32)
    for c in range(N_CHUNKS):
        rs = slice(c * L, (c + 1) * L)
        cs = slice(c, c + 1)
        qc = q_ref[rs, :]
        kc = k_ref[rs, :]
        vc = v_ref[rs, :]
        ktc = kt_ref[0, 0, c]
        u_r = u_ref[cs, :]
        g_r = g_ref[cs, :]
        mt_r = mt_ref[cs, :]
        mp_r = mp_ref[cs, :]
        w_r = w_ref[cs, :]
        dec_r = dec_ref[cs, :]
        gcol = jnp.broadcast_to(g_r, (LANES, L)).T
        mcol = jnp.broadcast_to(mt_r, (LANES, L)).T
        gcol2 = jnp.concatenate([gcol, gcol], axis=1)
        dmat = jnp.where(causal, jnp.exp(u_r - gcol2), 0.0)
        inter = jnp.exp(mp_r[:, :LANES] - gcol)
        inter2 = jnp.concatenate([inter, inter], axis=1)

        s = jnp.dot(qc, ktc, preferred_element_type=F32) * dmat
        rowsum = jnp.sum(s, axis=1, keepdims=True)
        num = jnp.dot(s.astype(BF16), vc, preferred_element_type=F32)
        den = rowsum
        if c > 0:
            num = num + inter2 * jnp.dot(qc, cmat.astype(BF16), preferred_element_type=F32)
            qn = jnp.sum(qc.astype(F32) * nvec[0:1, :], axis=1, keepdims=True)
            den = den + inter * qn
        denom = jnp.maximum(jnp.abs(den), jnp.exp(-mcol))
        rinv = 1.0 / denom
        hval = num * jnp.concatenate([rinv, rinv], axis=1)

        hg = hval * o_ref[rs, :].astype(F32)
        ms = jnp.mean(hg * hg, axis=1, keepdims=True)
        hn = hg * lax.rsqrt(ms + NORM_EPS) * nw
        out_ref[rs, :] = (hn * z_ref[rs, :].astype(F32)).astype(BF16)

        if c + 1 < N_CHUNKS:
            ktw = (ktc.astype(F32) * w_r).astype(BF16)
            cmat = dec_r * cmat + jnp.dot(ktw, vc, preferred_element_type=F32)
            w8 = jnp.broadcast_to(w_r, (SUBLANES, L)).astype(BF16)
            nvec = dec_r * nvec + jnp.dot(w8, kc, preferred_element_type=F32)


def _mlstm(mall, kt, gt, m_norm_w):
    gt4 = gt.reshape(2 * M_HEADS, BATCH, N_CHUNKS, M_CHUNK)
    blk = (SEQ, M_HEAD_DIM)

    def seg(k):
        return pl.BlockSpec(blk, lambda b, h, k=k: (b, k * M_HEADS + h))

    return pl.pallas_call(
        _mlstm_kernel,
        grid=(BATCH, M_HEADS),
        in_specs=[
            seg(0), seg(1),
            pl.BlockSpec((1, 1, N_CHUNKS, M_HEAD_DIM, M_CHUNK), lambda b, h: (b, h, 0, 0, 0)),
            seg(2), seg(3), seg(4),
            pl.BlockSpec((1, 1, N_CHUNKS, M_CHUNK), lambda b, h: (h, b, 0, 0)),
            pl.BlockSpec((1, 1, N_CHUNKS, M_CHUNK), lambda b, h: (M_HEADS + h, b, 0, 0)),
            pl.BlockSpec((1, M_HEAD_DIM), lambda b, h: (0, h)),
        ],
        out_specs=pl.BlockSpec(blk, lambda b, h: (b, h)),
        out_shape=jax.ShapeDtypeStruct((BATCH * SEQ, M_WIDTH), BF16),
        scratch_shapes=[pltpu.VMEM((N_CHUNKS, M_CHUNK), F32) for _ in range(8)],
        compiler_params=pltpu.CompilerParams(dimension_semantics=("parallel", "parallel"),
                                             vmem_limit_bytes=VMEM_LIMIT),
        name="mlstm",
    )(mall, mall, kt, mall, mall, mall, gt4, gt4, m_norm_w.reshape(1, M_WIDTH))


ATTN_WAVE = 12


def _attn_kernel(q0, k0, v0, q1, k1, v1, q2, k2, v2, out_ref,
                 a0, m0, l0, a1, m1, l1, a2, m2, l2):
    BLK = A_BLOCK
    ti = lax.broadcasted_iota(jnp.int32, (BLK, BLK), 0)
    ji = lax.broadcasted_iota(jnp.int32, (BLK, BLK), 1)
    cur_ok = ji <= ti
    ti2 = lax.broadcasted_iota(jnp.int32, (BLK, 2 * BLK), 0)
    krel = lax.broadcasted_iota(jnp.int32, (BLK, 2 * BLK), 1) - BLK
    band_ok = jnp.logical_and(krel <= ti2, krel >= ti2 - BLK)

    qs = (q0, q1, q2)
    ks = (k0, k1, k2)
    vs = (v0, v1, v2)
    outs = (a0, a1, a2)
    ms = (m0, m1, m2)
    ls = (l0, l1, l2)

    blocks = [(i, g) for i in range(SEQ // BLK) for g in range(N_GROUPS)]
    for w0 in range(0, len(blocks), ATTN_WAVE):
        wave = []
        for i, g in blocks[w0:w0 + ATTN_WAVE]:
            dil = A_GROUPS[g][1]
            r, n = divmod(i, SEQ // dil // BLK)
            rows = slice(i * BLK, (i + 1) * BLK)
            keys = slice((i - 1) * BLK, (i + 1) * BLK) if n > 0 else rows
            valid = band_ok if n > 0 else cur_ok
            s = lax.dot_general(qs[g][0, 0, rows, :], ks[g][0, 0, keys, :], (((1,), (1,)), ((), ())),
                                preferred_element_type=F32)
            s = jnp.where(valid, s, NEG_INF)
            m = jnp.max(s, axis=1, keepdims=True)
            tok0 = n * BLK * dil + r
            idx = pl.ds(tok0, BLK, stride=dil) if dil > 1 else pl.ds(tok0, BLK)
            wave.append((g, keys, valid, idx, s, m))
        probs = []
        for g, keys, valid, idx, s, m in wave:
            p = jnp.where(valid, jnp.exp(s - m), 0.0)
            probs.append((p, jnp.sum(p, axis=1, keepdims=True)))
        for (g, keys, valid, idx, s, m), (p, l) in zip(wave, probs):
            acc = jnp.dot(p.astype(BF16), vs[g][0, 0, keys, :], preferred_element_type=F32)
            outs[g][idx, :] = acc * (1.0 / jnp.where(l > 0, l, 1.0))
            ms[g][idx, :] = jnp.broadcast_to(m, (BLK, LANES))
            ls[g][idx, :] = jnp.broadcast_to(l, (BLK, LANES))

    RB = 256
    for rb in range(SEQ // RB):
        sl = slice(rb * RB, (rb + 1) * RB)
        mg = [ms[g][sl, :] for g in range(N_GROUPS)]
        mmax = jnp.maximum(jnp.maximum(mg[0], mg[1]), mg[2])
        num = None
        den = None
        for g in range(N_GROUPS):
            wt = jnp.exp(mg[g] - mmax) * ls[g][sl, :]
            num = wt * outs[g][sl, :] if num is None else num + wt * outs[g][sl, :]
            den = wt if den is None else den + wt
        out_ref[sl, :] = (num * (1.0 / den)).astype(BF16)


def _attn(qkv):
    in_specs = []
    args = []
    for g in range(N_GROUPS):
        for t in range(3):
            in_specs.append(pl.BlockSpec((1, 1, SEQ, A_HEAD_DIM), lambda b, h, t=t: (b, t, 0, h)))
            args.append(qkv[g])
    return pl.pallas_call(
        _attn_kernel,
        grid=(BATCH, A_HEADS),
        in_specs=in_specs,
        out_specs=pl.BlockSpec((SEQ, A_HEAD_DIM), lambda b, h: (b, h)),
        out_shape=jax.ShapeDtypeStruct((BATCH * SEQ, A_OUT_WIDTH), BF16),
        scratch_shapes=[pltpu.VMEM((SEQ, LANES), F32) for _ in range(9)],
        compiler_params=pltpu.CompilerParams(dimension_semantics=("parallel", "parallel"),
                                             vmem_limit_bytes=VMEM_LIMIT),
        name="attn",
    )(*args)


O_TM = 512


def _out_kernel(x_ref, prew_ref, hm_ref, att_ref, wg_ref, wpm_ref, wpa_ref, wout_ref, postw_ref,
                y_ref):
    x = x_ref[...]
    ms = jnp.mean(x * x, axis=-1, keepdims=True)
    h = (x * lax.rsqrt(ms + NORM_EPS) * prew_ref[...]).astype(BF16)
    gates = _dot_nt(h, wg_ref[...])
    za = _silu(gates[:, :A_OUT_WIDTH])
    sgm = _sigmoid(gates[:, A_OUT_WIDTH:A_OUT_WIDTH + D_MODEL])
    sga = _sigmoid(gates[:, A_OUT_WIDTH + D_MODEL:])
    a = (att_ref[...].astype(F32) * za).astype(BF16)
    bm = jnp.dot(hm_ref[...], wpm_ref[...], preferred_element_type=F32)
    ba = jnp.dot(a, wpa_ref[...], preferred_element_type=F32)
    merged = (sgm * bm + sga * ba).astype(BF16)
    y = jnp.dot(merged, wout_ref[...], preferred_element_type=F32)
    ms2 = jnp.mean(y * y, axis=-1, keepdims=True)
    y_ref[...] = x + y * lax.rsqrt(ms2 + NORM_EPS) * postw_ref[...]


def _out(x2d, pre_w, hm, att, wg4, wpm, wpa, wout, post_w):
    n = x2d.shape[0]
    const = lambda i: (0, 0)
    return pl.pallas_call(
        _out_kernel,
        grid=(n // O_TM,),
        in_specs=[
            pl.BlockSpec((O_TM, D_MODEL), lambda i: (i, 0)),
            pl.BlockSpec((1, D_MODEL), const),
            pl.BlockSpec((O_TM, M_WIDTH), lambda i: (i, 0)),
            pl.BlockSpec((O_TM, A_OUT_WIDTH), lambda i: (i, 0)),
            pl.BlockSpec((A_OUT_WIDTH + 2 * D_MODEL, D_MODEL), const),
            pl.BlockSpec((M_WIDTH, D_MODEL), const),
            pl.BlockSpec((A_OUT_WIDTH, D_MODEL), const),
            pl.BlockSpec((D_MODEL, D_MODEL), const),
            pl.BlockSpec((1, D_MODEL), const),
        ],
        out_specs=pl.BlockSpec((O_TM, D_MODEL), lambda i: (i, 0)),
        out_shape=jax.ShapeDtypeStruct((n, D_MODEL), F32),
        compiler_params=pltpu.CompilerParams(dimension_semantics=("parallel",),
                                             vmem_limit_bytes=VMEM_LIMIT),
        name="outproj",
    )(x2d, pre_w.reshape(1, D_MODEL), hm, att, wg4, wpm, wpa, wout, post_w.reshape(1, D_MODEL))


def _rope_tables():
    pos = jnp.arange(SEQ, dtype=F32)
    inv_freq = ROPE_THETA ** (-jnp.arange(0, ROPE_DIM, 2, dtype=F32) / ROPE_DIM)
    ang = pos[:, None] * inv_freq[None, :]
    cos, sin = jnp.cos(ang), jnp.sin(ang)
    half = ROPE_DIM // 2
    pad = A_HEAD_DIM - ROPE_DIM
    cos_t = jnp.concatenate([cos, cos, jnp.ones((SEQ, pad), F32)], axis=1)
    sin_up = jnp.concatenate([jnp.zeros((SEQ, half), F32), sin, jnp.zeros((SEQ, pad), F32)], axis=1)
    sin_dn = jnp.concatenate([-sin, jnp.zeros((SEQ, half + pad), F32)], axis=1)
    tab = jnp.stack([cos_t, sin_up, sin_dn])
    out = []
    for _, dil in A_GROUPS:
        ls = SEQ // dil
        out.append(tab.reshape(3, ls, dil, A_HEAD_DIM).transpose(0, 2, 1, 3).reshape(3, SEQ, A_HEAD_DIM))
    return out


def _layer(x, pre_w, w_in, b_if, conv_w, conv_b, m_norm_w, w_pm, w_pa, w_out, post_w):
    x2d = x.reshape(BATCH * SEQ, D_MODEL)
    wt = w_in.T
    wt_m = wt[:OFF_GATES].astype(BF16)
    wg = jnp.pad(wt[OFF_GATES:OFF_QA], ((0, LANES - 2 * M_HEADS), (0, 0))).astype(BF16)
    w_att = []
    for g in range(N_GROUPS):
        rows = [wt[off + g * A_OUT_WIDTH: off + (g + 1) * A_OUT_WIDTH] for off in (OFF_QA, OFF_KA, OFF_VA)]
        w_att.append(jnp.stack(rows).astype(BF16))
    wg4 = wt[OFF_ZA:].astype(BF16)
    tabs = _rope_tables()

    h, h4, h16 = _norm(x2d, pre_w)
    mall, kt, gt = _mproj(h, wt_m, wg, b_if.reshape(2 * M_HEADS, 1), conv_w, conv_b.reshape(1, 2 * M_WIDTH))
    h_by_group = (h, h4.reshape(BATCH * SEQ, D_MODEL), h16.reshape(BATCH * SEQ, D_MODEL))
    qkv = [_aproj(h_by_group[g], w_att[g], tabs[g], "aproj_d%d" % A_GROUPS[g][1]) for g in range(N_GROUPS)]
    hm = _mlstm(mall, kt, gt, m_norm_w)
    att = _attn(qkv)
    y = _out(x2d, pre_w, hm, att, wg4, w_pm.astype(BF16), w_pa.astype(BF16), w_out.astype(BF16), post_w)
    return y.reshape(BATCH, SEQ, D_MODEL)


@jax.jit
def kernel(x, pre_norm_w, w_in, b_if, conv_w, conv_b, m_norm_w, w_proj_m, w_proj_a, w_out, post_norm_w):
    for layer in range(pre_norm_w.shape[0]):
        x = _layer(x, pre_norm_w[layer], w_in[layer], b_if[layer], conv_w[layer], conv_b[layer],
                   m_norm_w[layer], w_proj_m[layer], w_proj_a[layer], w_out[layer], post_norm_w[layer])
    return x
```

```python
import jax
import jax.numpy as jnp
import numpy as np
from jax import lax
from jax.experimental import pallas as pl
from jax.experimental.pallas import tpu as pltpu

D_MODEL = 1024
BATCH = 8
SEQ = 2048
M_WIDTH = 1024
M_HEADS = 4
M_HEAD_DIM = 256
CONV_WIDTH = 4
A_GROUPS = ((128, 1), (512, 4), (2048, 16))
N_GROUPS = 3
A_HEADS = 4
A_HEAD_DIM = 128
A_OUT_WIDTH = 512
A_QKV_WIDTH = 1536
A_BLOCK = 128
ROPE_DIM = 32
ROPE_THETA = 500000.0
NORM_EPS = 1e-6
NEG_INF = -1e30

OFF_GATES = 5 * M_WIDTH
OFF_QA = OFF_GATES + 2 * M_HEADS
OFF_KA = OFF_QA + A_QKV_WIDTH
OFF_VA = OFF_KA + A_QKV_WIDTH
OFF_ZA = OFF_VA + A_QKV_WIDTH
OFF_GM = OFF_ZA + A_OUT_WIDTH
IN_WIDTH = OFF_GM + 2 * D_MODEL

LANES = 128
M_CHUNK = 256
N_CHUNKS = SEQ // M_CHUNK
VMEM_LIMIT = 48 * 1024 * 1024

F32 = jnp.float32
BF16 = jnp.bfloat16


def _dot_nt(a, bt):
    return lax.dot_general(a, bt, (((1,), (1,)), ((), ())), preferred_element_type=F32)


def _sigmoid(x):
    return 1.0 / (1.0 + jnp.exp(-x))


def _silu(x):
    return x * _sigmoid(x)


NORM_TM = 1024
DILATED = tuple(d for _, d in A_GROUPS if d > 1)


def _norm_kernel(x_ref, w_ref, h_ref, h4_ref, h16_ref, hs_ref):
    x = x_ref[...]
    ms = jnp.mean(x * x, axis=-1, keepdims=True)
    hf = x * lax.rsqrt(ms + NORM_EPS) * w_ref[...]
    h_ref[...] = hf.astype(BF16)
    nslab = D_MODEL // LANES
    for c in range(nslab):
        hs_ref[c] = hf[:, c * LANES:(c + 1) * LANES]
    for dil, o_ref in zip(DILATED, (h4_ref, h16_ref)):
        rows = NORM_TM // dil
        for r in range(dil):
            for c in range(nslab):
                o_ref[0, r, :, c * LANES:(c + 1) * LANES] = \
                    hs_ref[c, pl.ds(r, rows, stride=dil), :].astype(BF16)


def _norm(x2d, pre_w):
    halves = SEQ // NORM_TM
    out_shape = [jax.ShapeDtypeStruct((BATCH * SEQ, D_MODEL), BF16)]
    out_specs = [pl.BlockSpec((NORM_TM, D_MODEL), lambda b, s: (b * halves + s, 0))]
    for dil in DILATED:
        ls = SEQ // dil
        out_shape.append(jax.ShapeDtypeStruct((BATCH, dil, ls, D_MODEL), BF16))
        out_specs.append(pl.BlockSpec((1, dil, ls // halves, D_MODEL), lambda b, s: (b, 0, s, 0)))
    return pl.pallas_call(
        _norm_kernel,
        grid=(BATCH, halves),
        in_specs=[pl.BlockSpec((NORM_TM, D_MODEL), lambda b, s: (b * halves + s, 0)),
                  pl.BlockSpec((1, D_MODEL), lambda b, s: (0, 0))],
        out_specs=out_specs,
        out_shape=out_shape,
        scratch_shapes=[pltpu.VMEM((D_MODEL // LANES, NORM_TM, LANES), F32)],
        compiler_params=pltpu.CompilerParams(dimension_semantics=("parallel", "parallel"),
                                             vmem_limit_bytes=VMEM_LIMIT),
        name="norm",
    )(x2d, pre_w.reshape(1, D_MODEL))


M_TN = 512


SUBLANES = 8


def _causal_conv(acc, cw, cb):
    def taps(x, shifted):
        y = x * cw[CONV_WIDTH - 1:CONV_WIDTH, :] + cb
        for s in range(1, CONV_WIDTH):
            y = y + shifted(x, s) * cw[CONV_WIDTH - 1 - s:CONV_WIDTH - s, :]
        return y

    body = taps(acc, lambda x, s: pltpu.roll(x, s, axis=0))
    top = acc[:SUBLANES, :]
    rows = lax.broadcasted_iota(jnp.int32, top.shape, 0)
    head = taps(top, lambda x, s: jnp.where(rows >= s, pltpu.roll(x, s, axis=0), 0.0))
    return jnp.concatenate([head, body[SUBLANES:, :]], axis=0)


M_TILES = 5 * M_WIDTH // M_TN


def _mproj_kernel(h_ref, w_ref, wg_ref, bif_ref, cw_ref, cb_ref, mall_ref, kt_ref, gt_ref):
    j = pl.program_id(1)

    def matmul():
        return _dot_nt(h_ref[...], w_ref[...])

    @pl.when(j == 0)
    def _():
        g = _dot_nt(h_ref[...], wg_ref[...])
        gt = g.T
        gt_ref[...] = gt[:2 * M_HEADS, :] + bif_ref[...]

    @pl.when(j < 2)
    def _():
        y = _causal_conv(matmul(), cw_ref[...], cb_ref[...])
        mall_ref[...] = _silu(y).astype(BF16)

    @pl.when(jnp.logical_and(j >= 2, j < 4))
    def _():
        y = _causal_conv(matmul(), cw_ref[...], cb_ref[...])
        y = _silu(y) * (M_HEAD_DIM ** -0.5)
        mall_ref[...] = y.astype(BF16)
        for hh in range(M_TN // M_HEAD_DIM):
            for c in range(N_CHUNKS):
                blk = y[c * M_CHUNK:(c + 1) * M_CHUNK, hh * M_HEAD_DIM:(hh + 1) * M_HEAD_DIM]
                kt_ref[0, hh, c] = blk.T.astype(BF16)

    @pl.when(jnp.logical_and(j >= 4, j < 6))
    def _():
        mall_ref[...] = matmul().astype(BF16)

    @pl.when(jnp.logical_and(j >= 6, j < 8))
    def _():
        mall_ref[...] = _sigmoid(matmul()).astype(BF16)

    @pl.when(j >= 8)
    def _():
        mall_ref[...] = _silu(matmul()).astype(BF16)


def _mproj(h, wt_m, wg, bif, conv_w, conv_b):
    return pl.pallas_call(
        _mproj_kernel,
        grid=(BATCH, M_TILES),
        in_specs=[
            pl.BlockSpec((SEQ, D_MODEL), lambda i, j: (i, 0)),
            pl.BlockSpec((M_TN, D_MODEL), lambda i, j: (j, 0)),
            pl.BlockSpec((LANES, D_MODEL), lambda i, j: (0, 0)),
            pl.BlockSpec((2 * M_HEADS, 1), lambda i, j: (0, 0)),
            pl.BlockSpec((CONV_WIDTH, M_TN), lambda i, j: (0, jnp.minimum(j, 3))),
            pl.BlockSpec((1, M_TN), lambda i, j: (0, jnp.minimum(j, 3))),
        ],
        out_specs=[
            pl.BlockSpec((SEQ, M_TN), lambda i, j: (i, j)),
            pl.BlockSpec((1, M_TN // M_HEAD_DIM, N_CHUNKS, M_HEAD_DIM, M_CHUNK),
                         lambda i, j: (i, jnp.clip(j - 2, 0, 1), 0, 0, 0)),
            pl.BlockSpec((2 * M_HEADS, SEQ), lambda i, j: (0, i)),
        ],
        out_shape=[
            jax.ShapeDtypeStruct((BATCH * SEQ, 5 * M_WIDTH), BF16),
            jax.ShapeDtypeStruct((BATCH, M_HEADS, N_CHUNKS, M_HEAD_DIM, M_CHUNK), BF16),
            jax.ShapeDtypeStruct((2 * M_HEADS, BATCH * SEQ), F32),
        ],
        compiler_params=pltpu.CompilerParams(dimension_semantics=("parallel", "arbitrary"),
                                             vmem_limit_bytes=VMEM_LIMIT),
        name="mproj",
    )(h, wt_m, wg, bif, conv_w, conv_b)


A_TILES = 3


def _aproj_kernel(h_ref, w_ref, tab_ref, o_ref, acc0_ref, acc1_ref):
    s = pl.program_id(0)
    last = BATCH * A_TILES
    pt = (jnp.maximum(s, 1) - 1) % A_TILES

    accs = (acc0_ref, acc1_ref)

    def matmul(slot):
        accs[slot][...] = _dot_nt(h_ref[...], w_ref[0])

    def rope_epilogue(slot):
        cos_t = tab_ref[0]
        sin_up = tab_ref[1]
        sin_dn = tab_ref[2]
        scale = jnp.where(pt == 0, A_HEAD_DIM ** -0.5, 1.0).astype(F32)
        half = ROPE_DIM // 2
        for hd in range(A_HEADS):
            xh = accs[slot][:, hd * A_HEAD_DIM:(hd + 1) * A_HEAD_DIM]
            y = (xh * cos_t + pltpu.roll(xh, half, axis=1) * sin_up
                 + pltpu.roll(xh, A_HEAD_DIM - half, axis=1) * sin_dn)
            o_ref[0, 0, :, hd * A_HEAD_DIM:(hd + 1) * A_HEAD_DIM] = (y * scale).astype(BF16)

    def cast_epilogue(slot):
        o_ref[0, 0] = accs[slot][...].astype(BF16)

    mid = jnp.logical_and(s > 0, s < last)

    @pl.when(s == 0)
    def _():
        matmul(0)

    for slot in range(2):
        mine = jnp.logical_and(mid, s % 2 == slot)

        @pl.when(jnp.logical_and(mine, pt < 2))
        def _(slot=slot):
            rope_epilogue(1 - slot)
            matmul(slot)

        @pl.when(jnp.logical_and(mine, pt == 2))
        def _(slot=slot):
            cast_epilogue(1 - slot)
            matmul(slot)

    @pl.when(s == last)
    def _():
        cast_epilogue((last - 1) % 2)


def _aproj(h, w_g, tabs_g, name):
    last = BATCH * A_TILES

    def cur(s):
        return jnp.minimum(s, last - 1)

    def prev(s):
        return jnp.maximum(s, 1) - 1

    return pl.pallas_call(
        _aproj_kernel,
        grid=(last + 1,),
        in_specs=[
            pl.BlockSpec((SEQ, D_MODEL), lambda s: (cur(s) // A_TILES, 0)),
            pl.BlockSpec((1, A_OUT_WIDTH, D_MODEL), lambda s: (cur(s) % A_TILES, 0, 0)),
            pl.BlockSpec((3, SEQ, LANES), lambda s: (0, 0, 0)),
        ],
        out_specs=pl.BlockSpec((1, 1, SEQ, A_OUT_WIDTH),
                               lambda s: (prev(s) // A_TILES, prev(s) % A_TILES, 0, 0)),
        out_shape=jax.ShapeDtypeStruct((BATCH, A_TILES, SEQ, A_OUT_WIDTH), BF16),
        scratch_shapes=[pltpu.VMEM((SEQ, A_OUT_WIDTH), F32), pltpu.VMEM((SEQ, A_OUT_WIDTH), F32)],
        compiler_params=pltpu.CompilerParams(dimension_semantics=("arbitrary",),
                                             vmem_limit_bytes=VMEM_LIMIT),
        name=name,
    )(h, w_g, tabs_g)


def _lane_scan(x, op, fill):
    lane = lax.broadcasted_iota(jnp.int32, x.shape, 1)
    sh = 1
    while sh < x.shape[1]:
        shifted = jnp.where(lane >= sh, pltpu.roll(x, sh, axis=1), fill)
        x = op(x, shifted)
        sh *= 2
    return x


def _mlstm_kernel(q_ref, k_ref, kt_ref, v_ref, o_ref, z_ref, ig_ref, fg_ref, nw_ref, out_ref,
                  u_ref, g_ref, mt_ref, mp_ref, w_ref, dec_ref, bt_ref, um_ref):
    L = M_CHUNK
    ig = ig_ref[0, 0]
    fg = fg_ref[0, 0]
    logf = -(jnp.maximum(-fg, 0.0) + jnp.log1p(jnp.exp(-jnp.abs(fg))))
    b = _lane_scan(logf, jnp.add, 0.0)
    u = ig - b
    cm = _lane_scan(u, jnp.maximum, NEG_INF)
    lane = lax.broadcasted_iota(jnp.int32, (N_CHUNKS, L), 1)
    btot = jnp.sum(jnp.where(lane == L - 1, b, 0.0), axis=1, keepdims=True)
    umax = jnp.max(cm, axis=1, keepdims=True)
    bt_ref[...] = jnp.broadcast_to(btot, (N_CHUNKS, L))
    um_ref[...] = jnp.broadcast_to(umax, (N_CHUNKS, L))
    m = jnp.zeros((1, L), F32)
    for c in range(N_CHUNKS):
        mp_ref[c:c + 1, :] = m
        m = bt_ref[c:c + 1, :] + jnp.maximum(m, um_ref[c:c + 1, :])
    mprev = mp_ref[...]
    g = jnp.maximum(mprev, cm)
    glast = jnp.max(g, axis=1, keepdims=True)
    u_ref[...] = u
    g_ref[...] = g
    mt_ref[...] = b + g
    w_ref[...] = jnp.exp(u - glast)
    dec_ref[...] = jnp.broadcast_to(jnp.exp(mprev - glast), (N_CHUNKS, L))

    ti = lax.broadcasted_iota(jnp.int32, (L, L), 0)
    si = lax.broadcasted_iota(jnp.int32, (L, L), 1)
    causal = ti >= si
    nw = nw_ref[...]

    cmat = jnp.zeros((M_HEAD_DIM, M_HEAD_DIM), F32)
    nvec = jnp.zeros((SUBLANES, M_HEAD_DIM), F32)
    for c in range(N_CHUNKS):
        rs = slice(c * L, (c + 1) * L)
        cs = slice(c, c + 1)
        qc = q_ref[rs, :]
        kc = k_ref[rs, :]
        vc = v_ref[rs, :]
        ktc = kt_ref[0, 0, c]
        u_r = u_ref[cs, :]
        g_r = g_ref[cs, :]
        mt_r = mt_ref[cs, :]
        mp_r = mp_ref[cs, :]
        w_r = w_ref[cs, :]
        dec_r = dec_ref[cs, :]
        gcol = jnp.broadcast_to(g_r, (LANES, L)).T
        mcol = jnp.broadcast_to(mt_r, (LANES, L)).T
        gcol2 = jnp.concatenate([gcol, gcol], axis=1)
        dmat = jnp.where(causal, jnp.exp(u_r - gcol2), 0.0)
        inter = jnp.exp(mp_r[:, :LANES] - gcol)
        inter2 = jnp.concatenate([inter, inter], axis=1)

        s = jnp.dot(qc, ktc, preferred_element_type=F32) * dmat
        rowsum = jnp.sum(s, axis=1, keepdims=True)
        num = jnp.dot(s.astype(BF16), vc, preferred_element_type=F32)
        den = rowsum
        if c > 0:
            num = num + inter2 * jnp.dot(qc, cmat.astype(BF16), preferred_element_type=F32)
            qn = jnp.sum(qc.astype(F32) * nvec[0:1, :], axis=1, keepdims=True)
            den = den + inter * qn
        denom = jnp.maximum(jnp.abs(den), jnp.exp(-mcol))
        rinv = 1.0 / denom
        hval = num * jnp.concatenate([rinv, rinv], axis=1)

        hg = hval * o_ref[rs, :].astype(F32)
        ms = jnp.mean(hg * hg, axis=1, keepdims=True)
        hn = hg * lax.rsqrt(ms + NORM_EPS) * nw
        out_ref[rs, :] = (hn * z_ref[rs, :].astype(F32)).astype(BF16)

        if c + 1 < N_CHUNKS:
            ktw = (ktc.astype(F32) * w_r).astype(BF16)
            cmat = dec_r * cmat + jnp.dot(ktw, vc, preferred_element_type=F32)
            w8 = jnp.broadcast_to(w_r, (SUBLANES, L)).astype(BF16)
            nvec = dec_r * nvec + jnp.dot(w8, kc, preferred_element_type=F32)


def _mlstm(mall, kt, gt, m_norm_w):
    gt4 = gt.reshape(2 * M_HEADS, BATCH, N_CHUNKS, M_CHUNK)
    blk = (SEQ, M_HEAD_DIM)

    def seg(k):
        return pl.BlockSpec(blk, lambda b, h, k=k: (b, k * M_HEADS + h))

    return pl.pallas_call(
        _mlstm_kernel,
        grid=(BATCH, M_HEADS),
        in_specs=[
            seg(0), seg(1),
            pl.BlockSpec((1, 1, N_CHUNKS, M_HEAD_DIM, M_CHUNK), lambda b, h: (b, h, 0, 0, 0)),
            seg(2), seg(3), seg(4),
            pl.BlockSpec((1, 1, N_CHUNKS, M_CHUNK), lambda b, h: (h, b, 0, 0)),
            pl.BlockSpec((1, 1, N_CHUNKS, M_CHUNK), lambda b, h: (M_HEADS + h, b, 0, 0)),
            pl.BlockSpec((1, M_HEAD_DIM), lambda b, h: (0, h)),
        ],
        out_specs=pl.BlockSpec(blk, lambda b, h: (b, h)),
        out_shape=jax.ShapeDtypeStruct((BATCH * SEQ, M_WIDTH), BF16),
        scratch_shapes=[pltpu.VMEM((N_CHUNKS, M_CHUNK), F32) for _ in range(8)],
        compiler_params=pltpu.CompilerParams(dimension_semantics=("parallel", "parallel"),
                                             vmem_limit_bytes=VMEM_LIMIT),
        name="mlstm",
    )(mall, mall, kt, mall, mall, mall, gt4, gt4, m_norm_w.reshape(1, M_WIDTH))


ATTN_WAVE = 12


def _attn_kernel(q0, k0, v0, q1, k1, v1, q2, k2, v2, out_ref,
                 a0, m0, l0, a1, m1, l1, a2, m2, l2):
    BLK = A_BLOCK
    ti = lax.broadcasted_iota(jnp.int32, (BLK, BLK), 0)
    ji = lax.broadcasted_iota(jnp.int32, (BLK, BLK), 1)
    cur_ok = ji <= ti
    ti2 = lax.broadcasted_iota(jnp.int32, (BLK, 2 * BLK), 0)
    krel = lax.broadcasted_iota(jnp.int32, (BLK, 2 * BLK), 1) - BLK
    band_ok = jnp.logical_and(krel <= ti2, krel >= ti2 - BLK)

    qs = (q0, q1, q2)
    ks = (k0, k1, k2)
    vs = (v0, v1, v2)
    outs = (a0, a1, a2)
    ms = (m0, m1, m2)
    ls = (l0, l1, l2)

    blocks = [(i, g) for i in range(SEQ // BLK) for g in range(N_GROUPS)]
    for w0 in range(0, len(blocks), ATTN_WAVE):
        wave = []
        for i, g in blocks[w0:w0 + ATTN_WAVE]:
            dil = A_GROUPS[g][1]
            r, n = divmod(i, SEQ // dil // BLK)
            rows = slice(i * BLK, (i + 1) * BLK)
            keys = slice((i - 1) * BLK, (i + 1) * BLK) if n > 0 else rows
            valid = band_ok if n > 0 else cur_ok
            s = lax.dot_general(qs[g][0, 0, rows, :], ks[g][0, 0, keys, :], (((1,), (1,)), ((), ())),
                                preferred_element_type=F32)
            s = jnp.where(valid, s, NEG_INF)
            m = jnp.max(s, axis=1, keepdims=True)
            tok0 = n * BLK * dil + r
            idx = pl.ds(tok0, BLK, stride=dil) if dil > 1 else pl.ds(tok0, BLK)
            wave.append((g, keys, valid, idx, s, m))
        probs = []
        for g, keys, valid, idx, s, m in wave:
            p = jnp.where(valid, jnp.exp(s - m), 0.0)
            probs.append((p, jnp.sum(p, axis=1, keepdims=True)))
        for (g, keys, valid, idx, s, m), (p, l) in zip(wave, probs):
            acc = jnp.dot(p.astype(BF16), vs[g][0, 0, keys, :], preferred_element_type=F32)
            outs[g][idx, :] = acc * (1.0 / jnp.where(l > 0, l, 1.0))
            ms[g][idx, :] = jnp.broadcast_to(m, (BLK, LANES))
            ls[g][idx, :] = jnp.broadcast_to(l, (BLK, LANES))

    RB = 256
    for rb in range(SEQ // RB):
        sl = slice(rb * RB, (rb + 1) * RB)
        mg = [ms[g][sl, :] for g in range(N_GROUPS)]
        mmax = jnp.maximum(jnp.maximum(mg[0], mg[1]), mg[2])
        num = None
        den = None
        for g in range(N_GROUPS):
            wt = jnp.exp(mg[g] - mmax) * ls[g][sl, :]
            num = wt * outs[g][sl, :] if num is None else num + wt * outs[g][sl, :]
            den = wt if den is None else den + wt
        out_ref[sl, :] = (num * (1.0 / den)).astype(BF16)


def _attn(qkv):
    in_specs = []
    args = []
    for g in range(N_GROUPS):
        for t in range(3):
            in_specs.append(pl.BlockSpec((1, 1, SEQ, A_HEAD_DIM), lambda b, h, t=t: (b, t, 0, h)))
            args.append(qkv[g])
    return pl.pallas_call(
        _attn_kernel,
        grid=(BATCH, A_HEADS),
        in_specs=in_specs,
        out_specs=pl.BlockSpec((SEQ, A_HEAD_DIM), lambda b, h: (b, h)),
        out_shape=jax.ShapeDtypeStruct((BATCH * SEQ, A_OUT_WIDTH), BF16),
        scratch_shapes=[pltpu.VMEM((SEQ, LANES), F32) for _ in range(9)],
        compiler_params=pltpu.CompilerParams(dimension_semantics=("parallel", "parallel"),
                                             vmem_limit_bytes=VMEM_LIMIT),
        name="attn",
    )(*args)


O_TM = 512


def _out_kernel(x_ref, prew_ref, hm_ref, att_ref, wg_ref, wpm_ref, wpa_ref, wout_ref, postw_ref,
                y_ref):
    x = x_ref[...]
    ms = jnp.mean(x * x, axis=-1, keepdims=True)
    h = (x * lax.rsqrt(ms + NORM_EPS) * prew_ref[...]).astype(BF16)
    gates = _dot_nt(h, wg_ref[...])
    za = _silu(gates[:, :A_OUT_WIDTH])
    sgm = _sigmoid(gates[:, A_OUT_WIDTH:A_OUT_WIDTH + D_MODEL])
    sga = _sigmoid(gates[:, A_OUT_WIDTH + D_MODEL:])
    a = (att_ref[...].astype(F32) * za).astype(BF16)
    bm = jnp.dot(hm_ref[...], wpm_ref[...], preferred_element_type=F32)
    ba = jnp.dot(a, wpa_ref[...], preferred_element_type=F32)
    merged = (sgm * bm + sga * ba).astype(BF16)
    y = jnp.dot(merged, wout_ref[...], preferred_element_type=F32)
    ms2 = jnp.mean(y * y, axis=-1, keepdims=True)
    y_ref[...] = x + y * lax.rsqrt(ms2 + NORM_EPS) * postw_ref[...]


def _out(x2d, pre_w, hm, att, wg4, wpm, wpa, wout, post_w):
    n = x2d.shape[0]
    const = lambda i: (0, 0)
    return pl.pallas_call(
        _out_kernel,
        grid=(n // O_TM,),
        in_specs=[
            pl.BlockSpec((O_TM, D_MODEL), lambda i: (i, 0)),
            pl.BlockSpec((1, D_MODEL), const),
            pl.BlockSpec((O_TM, M_WIDTH), lambda i: (i, 0)),
            pl.BlockSpec((O_TM, A_OUT_WIDTH), lambda i: (i, 0)),
            pl.BlockSpec((A_OUT_WIDTH + 2 * D_MODEL, D_MODEL), const),
            pl.BlockSpec((M_WIDTH, D_MODEL), const),
            pl.BlockSpec((A_OUT_WIDTH, D_MODEL), const),
            pl.BlockSpec((D_MODEL, D_MODEL), const),
            pl.BlockSpec((1, D_MODEL), const),
        ],
        out_specs=pl.BlockSpec((O_TM, D_MODEL), lambda i: (i, 0)),
        out_shape=jax.ShapeDtypeStruct((n, D_MODEL), F32),
        compiler_params=pltpu.CompilerParams(dimension_semantics=("parallel",),
                                             vmem_limit_bytes=VMEM_LIMIT),
        name="outproj",
    )(x2d, pre_w.reshape(1, D_MODEL), hm, att, wg4, wpm, wpa, wout, post_w.reshape(1, D_MODEL))


def _rope_tables():
    pos = np.arange(SEQ, dtype=np.float64)
    inv_freq = ROPE_THETA ** (-np.arange(0, ROPE_DIM, 2, dtype=np.float64) / ROPE_DIM)
    ang = pos[:, None] * inv_freq[None, :]
    cos, sin = np.cos(ang), np.sin(ang)
    half = ROPE_DIM // 2
    pad = A_HEAD_DIM - ROPE_DIM
    cos_t = np.concatenate([cos, cos, np.ones((SEQ, pad))], axis=1)
    sin_up = np.concatenate([np.zeros((SEQ, half)), sin, np.zeros((SEQ, pad))], axis=1)
    sin_dn = np.concatenate([-sin, np.zeros((SEQ, half + pad))], axis=1)
    tab = np.stack([cos_t, sin_up, sin_dn]).astype(np.float32)
    out = []
    for _, dil in A_GROUPS:
        ls = SEQ // dil
        out.append(jnp.asarray(
            tab.reshape(3, ls, dil, A_HEAD_DIM).transpose(0, 2, 1, 3).reshape(3, SEQ, A_HEAD_DIM)))
    return out


def _layer(x, pre_w, w_in, b_if, conv_w, conv_b, m_norm_w, w_pm, w_pa, w_out, post_w):
    x2d = x.reshape(BATCH * SEQ, D_MODEL)
    wt = w_in.T
    wt_m = wt[:OFF_GATES].astype(BF16)
    wg = jnp.pad(wt[OFF_GATES:OFF_QA], ((0, LANES - 2 * M_HEADS), (0, 0))).astype(BF16)
    w_att = []
    for g in range(N_GROUPS):
        rows = [wt[off + g * A_OUT_WIDTH: off + (g + 1) * A_OUT_WIDTH] for off in (OFF_QA, OFF_KA, OFF_VA)]
        w_att.append(jnp.stack(rows).astype(BF16))
    wg4 = wt[OFF_ZA:].astype(BF16)
    tabs = _rope_tables()

    h, h4, h16 = _norm(x2d, pre_w)
    mall, kt, gt = _mproj(h, wt_m, wg, b_if.reshape(2 * M_HEADS, 1), conv_w, conv_b.reshape(1, 2 * M_WIDTH))
    h_by_group = (h, h4.reshape(BATCH * SEQ, D_MODEL), h16.reshape(BATCH * SEQ, D_MODEL))
    qkv = [_aproj(h_by_group[g], w_att[g], tabs[g], "aproj_d%d" % A_GROUPS[g][1]) for g in range(N_GROUPS)]
    hm = _mlstm(mall, kt, gt, m_norm_w)
    att = _attn(qkv)
    y = _out(x2d, pre_w, hm, att, wg4, w_pm.astype(BF16), w_pa.astype(BF16), w_out.astype(BF16), post_w)
    return y.reshape(BATCH, SEQ, D_MODEL)


@jax.jit
def kernel(x, pre_norm_w, w_in, b_if, conv_w, conv_b, m_norm_w, w_proj_m, w_proj_a, w_out, post_norm_w):
    for layer in range(pre_norm_w.shape[0]):
        x = _layer(x, pre_norm_w[layer], w_in[layer], b_if[layer], conv_w[layer], conv_b[layer],
                   m_norm_w[layer], w_proj_m[layer], w_proj_a[layer], w_out[layer], post_norm_w[layer])
    return x
```

```python
import jax
import jax.numpy as jnp
import numpy as np
from jax import lax
from jax.experimental import pallas as pl
from jax.experimental.pallas import tpu as pltpu

D_MODEL = 1024
BATCH = 8
SEQ = 2048
M_WIDTH = 1024
M_HEADS = 4
M_HEAD_DIM = 256
CONV_WIDTH = 4
A_GROUPS = ((128, 1), (512, 4), (2048, 16))
N_GROUPS = 3
A_HEADS = 4
A_HEAD_DIM = 128
A_OUT_WIDTH = 512
A_QKV_WIDTH = 1536
A_BLOCK = 128
ROPE_DIM = 32
ROPE_THETA = 500000.0
NORM_EPS = 1e-6
NEG_INF = -1e30
LOG2E = 1.4426950408889634

OFF_GATES = 5 * M_WIDTH
OFF_QA = OFF_GATES + 2 * M_HEADS
OFF_KA = OFF_QA + A_QKV_WIDTH
OFF_VA = OFF_KA + A_QKV_WIDTH
OFF_ZA = OFF_VA + A_QKV_WIDTH
OFF_GM = OFF_ZA + A_OUT_WIDTH
IN_WIDTH = OFF_GM + 2 * D_MODEL

LANES = 128
M_CHUNK = 256
N_CHUNKS = SEQ // M_CHUNK
VMEM_LIMIT = 48 * 1024 * 1024

F32 = jnp.float32
BF16 = jnp.bfloat16


def _dot_nt(a, bt):
    return lax.dot_general(a, bt, (((1,), (1,)), ((), ())), preferred_element_type=F32)


def _sigmoid(x):
    return 1.0 / (1.0 + jnp.exp(-x))


def _silu(x):
    return x * _sigmoid(x)


NORM_TM = 1024
DILATED = tuple(d for _, d in A_GROUPS if d > 1)


def _norm_kernel(x_ref, w_ref, h_ref, h4_ref, h16_ref, hs_ref):
    x = x_ref[...]
    ms = jnp.mean(x * x, axis=-1, keepdims=True)
    hf = x * lax.rsqrt(ms + NORM_EPS) * w_ref[...]
    h_ref[...] = hf.astype(BF16)
    nslab = D_MODEL // LANES
    for c in range(nslab):
        hs_ref[c] = hf[:, c * LANES:(c + 1) * LANES]
    for dil, o_ref in zip(DILATED, (h4_ref, h16_ref)):
        rows = NORM_TM // dil
        for r in range(dil):
            for c in range(nslab):
                o_ref[0, r, :, c * LANES:(c + 1) * LANES] = \
                    hs_ref[c, pl.ds(r, rows, stride=dil), :].astype(BF16)


def _norm(x2d, pre_w):
    halves = SEQ // NORM_TM
    out_shape = [jax.ShapeDtypeStruct((BATCH * SEQ, D_MODEL), BF16)]
    out_specs = [pl.BlockSpec((NORM_TM, D_MODEL), lambda b, s: (b * halves + s, 0))]
    for dil in DILATED:
        ls = SEQ // dil
        out_shape.append(jax.ShapeDtypeStruct((BATCH, dil, ls, D_MODEL), BF16))
        out_specs.append(pl.BlockSpec((1, dil, ls // halves, D_MODEL), lambda b, s: (b, 0, s, 0)))
    return pl.pallas_call(
        _norm_kernel,
        grid=(BATCH, halves),
        in_specs=[pl.BlockSpec((NORM_TM, D_MODEL), lambda b, s: (b * halves + s, 0)),
                  pl.BlockSpec((1, D_MODEL), lambda b, s: (0, 0))],
        out_specs=out_specs,
        out_shape=out_shape,
        scratch_shapes=[pltpu.VMEM((D_MODEL // LANES, NORM_TM, LANES), F32)],
        compiler_params=pltpu.CompilerParams(dimension_semantics=("parallel", "parallel"),
                                             vmem_limit_bytes=VMEM_LIMIT),
        name="norm",
    )(x2d, pre_w.reshape(1, D_MODEL))


M_TN = 512


SUBLANES = 8


def _causal_conv(acc, cw, cb):
    def taps(x, shifted):
        y = x * cw[CONV_WIDTH - 1:CONV_WIDTH, :] + cb
        for s in range(1, CONV_WIDTH):
            y = y + shifted(x, s) * cw[CONV_WIDTH - 1 - s:CONV_WIDTH - s, :]
        return y

    body = taps(acc, lambda x, s: pltpu.roll(x, s, axis=0))
    top = acc[:SUBLANES, :]
    rows = lax.broadcasted_iota(jnp.int32, top.shape, 0)
    head = taps(top, lambda x, s: jnp.where(rows >= s, pltpu.roll(x, s, axis=0), 0.0))
    return jnp.concatenate([head, body[SUBLANES:, :]], axis=0)


M_TILES = 5 * M_WIDTH // M_TN


def _mproj_kernel(h_ref, w_ref, wg_ref, bif_ref, cw_ref, cb_ref, mall_ref, kt_ref, gt_ref):
    j = pl.program_id(1)

    def matmul():
        return _dot_nt(h_ref[...], w_ref[...])

    @pl.when(j == 0)
    def _():
        g = _dot_nt(h_ref[...], wg_ref[...])
        gt = g.T[:2 * M_HEADS, :] + bif_ref[...]
        for c in range(N_CHUNKS):
            gt_ref[0, :, c, :] = gt[:, c * M_CHUNK:(c + 1) * M_CHUNK]

    @pl.when(j < 2)
    def _():
        y = _causal_conv(matmul(), cw_ref[...], cb_ref[...])
        mall_ref[...] = _silu(y).astype(BF16)

    @pl.when(jnp.logical_and(j >= 2, j < 4))
    def _():
        y = _causal_conv(matmul(), cw_ref[...], cb_ref[...])
        y = _silu(y) * (M_HEAD_DIM ** -0.5)
        mall_ref[...] = y.astype(BF16)
        for hh in range(M_TN // M_HEAD_DIM):
            for c in range(N_CHUNKS):
                blk = y[c * M_CHUNK:(c + 1) * M_CHUNK, hh * M_HEAD_DIM:(hh + 1) * M_HEAD_DIM]
                kt_ref[0, hh, c] = blk.T.astype(BF16)

    @pl.when(jnp.logical_and(j >= 4, j < 6))
    def _():
        mall_ref[...] = matmul().astype(BF16)

    @pl.when(jnp.logical_and(j >= 6, j < 8))
    def _():
        mall_ref[...] = _sigmoid(matmul()).astype(BF16)

    @pl.when(j >= 8)
    def _():
        mall_ref[...] = _silu(matmul()).astype(BF16)


def _mproj(h, wt_m, wg, bif, conv_w, conv_b):
    return pl.pallas_call(
        _mproj_kernel,
        grid=(BATCH, M_TILES),
        in_specs=[
            pl.BlockSpec((SEQ, D_MODEL), lambda i, j: (i, 0)),
            pl.BlockSpec((M_TN, D_MODEL), lambda i, j: (j, 0)),
            pl.BlockSpec((LANES, D_MODEL), lambda i, j: (0, 0)),
            pl.BlockSpec((2 * M_HEADS, 1), lambda i, j: (0, 0)),
            pl.BlockSpec((CONV_WIDTH, M_TN), lambda i, j: (0, jnp.minimum(j, 3))),
            pl.BlockSpec((1, M_TN), lambda i, j: (0, jnp.minimum(j, 3))),
        ],
        out_specs=[
            pl.BlockSpec((SEQ, M_TN), lambda i, j: (i, j)),
            pl.BlockSpec((1, M_TN // M_HEAD_DIM, N_CHUNKS, M_HEAD_DIM, M_CHUNK),
                         lambda i, j: (i, jnp.clip(j - 2, 0, 1), 0, 0, 0)),
            pl.BlockSpec((1, 2 * M_HEADS, N_CHUNKS, M_CHUNK), lambda i, j: (i, 0, 0, 0)),
        ],
        out_shape=[
            jax.ShapeDtypeStruct((BATCH * SEQ, 5 * M_WIDTH), BF16),
            jax.ShapeDtypeStruct((BATCH, M_HEADS, N_CHUNKS, M_HEAD_DIM, M_CHUNK), BF16),
            jax.ShapeDtypeStruct((BATCH, 2 * M_HEADS, N_CHUNKS, M_CHUNK), F32),
        ],
        compiler_params=pltpu.CompilerParams(dimension_semantics=("parallel", "arbitrary"),
                                             vmem_limit_bytes=VMEM_LIMIT),
        name="mproj",
    )(h, wt_m, wg, bif, conv_w, conv_b)


A_TILES = 3


def _aproj_kernel(h_ref, w_ref, tab_ref, o_ref, acc0_ref, acc1_ref):
    s = pl.program_id(0)
    last = BATCH * A_TILES
    pt = (jnp.maximum(s, 1) - 1) % A_TILES

    accs = (acc0_ref, acc1_ref)

    def matmul(slot):
        accs[slot][...] = _dot_nt(h_ref[...], w_ref[0])

    def rope_epilogue(slot):
        cos_t = tab_ref[0]
        sin_up = tab_ref[1]
        sin_dn = tab_ref[2]
        scale = jnp.where(pt == 0, A_HEAD_DIM ** -0.5 * LOG2E, 1.0).astype(F32)
        half = ROPE_DIM // 2
        for hd in range(A_HEADS):
            xh = accs[slot][:, hd * A_HEAD_DIM:(hd + 1) * A_HEAD_DIM]
            y = (xh * cos_t + pltpu.roll(xh, half, axis=1) * sin_up
                 + pltpu.roll(xh, A_HEAD_DIM - half, axis=1) * sin_dn)
            o_ref[0, 0, :, hd * A_HEAD_DIM:(hd + 1) * A_HEAD_DIM] = (y * scale).astype(BF16)

    def cast_epilogue(slot):
        o_ref[0, 0] = accs[slot][...].astype(BF16)

    mid = jnp.logical_and(s > 0, s < last)

    @pl.when(s == 0)
    def _():
        matmul(0)

    for slot in range(2):
        mine = jnp.logical_and(mid, s % 2 == slot)

        @pl.when(jnp.logical_and(mine, pt < 2))
        def _(slot=slot):
            rope_epilogue(1 - slot)
            matmul(slot)

        @pl.when(jnp.logical_and(mine, pt == 2))
        def _(slot=slot):
            cast_epilogue(1 - slot)
            matmul(slot)

    @pl.when(s == last)
    def _():
        cast_epilogue((last - 1) % 2)


def _aproj(h, w_g, tabs_g, name):
    last = BATCH * A_TILES

    def cur(s):
        return jnp.minimum(s, last - 1)

    def prev(s):
        return jnp.maximum(s, 1) - 1

    return pl.pallas_call(
        _aproj_kernel,
        grid=(last + 1,),
        in_specs=[
            pl.BlockSpec((SEQ, D_MODEL), lambda s: (cur(s) // A_TILES, 0)),
            pl.BlockSpec((1, A_OUT_WIDTH, D_MODEL), lambda s: (cur(s) % A_TILES, 0, 0)),
            pl.BlockSpec((3, SEQ, LANES), lambda s: (0, 0, 0)),
        ],
        out_specs=pl.BlockSpec((1, 1, SEQ, A_OUT_WIDTH),
                               lambda s: (prev(s) // A_TILES, prev(s) % A_TILES, 0, 0)),
        out_shape=jax.ShapeDtypeStruct((BATCH, A_TILES, SEQ, A_OUT_WIDTH), BF16),
        scratch_shapes=[pltpu.VMEM((SEQ, A_OUT_WIDTH), F32), pltpu.VMEM((SEQ, A_OUT_WIDTH), F32)],
        compiler_params=pltpu.CompilerParams(dimension_semantics=("arbitrary",),
                                             vmem_limit_bytes=VMEM_LIMIT),
        name=name,
    )(h, w_g, tabs_g)


def _lane_scan(x, op, fill):
    lane = lax.broadcasted_iota(jnp.int32, x.shape, 1)
    sh = 1
    while sh < x.shape[1]:
        shifted = jnp.where(lane >= sh, pltpu.roll(x, sh, axis=1), fill)
        x = op(x, shifted)
        sh *= 2
    return x


def _gates_kernel(ig_ref, fg_ref, u_ref, g_ref, mt_ref, mp_ref, w_ref, dec_ref):
    L = M_CHUNK
    ig = ig_ref[...]
    fg = fg_ref[...]
    logf = -(jnp.maximum(-fg, 0.0) + jnp.log1p(jnp.exp(-jnp.abs(fg))))
    b = _lane_scan(logf, jnp.add, 0.0)
    u = ig - b
    cm = _lane_scan(u, jnp.maximum, NEG_INF)
    lane = lax.broadcasted_iota(jnp.int32, ig.shape, 1)
    btot = jnp.broadcast_to(jnp.sum(jnp.where(lane == L - 1, b, 0.0), axis=1, keepdims=True), ig.shape)
    umax = jnp.broadcast_to(jnp.max(cm, axis=1, keepdims=True), ig.shape)
    chunk = lax.broadcasted_iota(jnp.int32, ig.shape, 0) % N_CHUNKS
    m = jnp.zeros(ig.shape, F32)
    for c in range(N_CHUNKS - 1):
        nxt = pltpu.roll(btot + jnp.maximum(m, umax), 1, axis=0)
        m = jnp.where(chunk == c + 1, nxt, m)
    g = jnp.maximum(m, cm)
    glast = jnp.max(g, axis=1, keepdims=True)
    u_ref[...] = u * LOG2E
    g_ref[...] = g * LOG2E
    mt_ref[...] = (b + g) * LOG2E
    mp_ref[...] = m * LOG2E
    w_ref[...] = jnp.exp(u - glast)
    dec_ref[...] = jnp.broadcast_to(jnp.exp(m - glast), ig.shape)


def _gates(gt):
    rows = BATCH * M_HEADS * N_CHUNKS
    ig = gt[:, :M_HEADS].reshape(rows, M_CHUNK)
    fg = gt[:, M_HEADS:].reshape(rows, M_CHUNK)
    spec = pl.BlockSpec((rows, M_CHUNK), lambda: (0, 0))
    return pl.pallas_call(
        _gates_kernel,
        in_specs=[spec, spec],
        out_specs=[spec] * 6,
        out_shape=[jax.ShapeDtypeStruct((rows, M_CHUNK), F32)] * 6,
        name="gates",
    )(ig, fg)


def _mlstm_kernel(q_ref, kt_ref, v_ref, o_ref, z_ref, u_ref, g_ref, mt_ref, mp_ref, w_ref, dec_ref,
                  out_ref):
    L = M_CHUNK
    ti = lax.broadcasted_iota(jnp.int32, (L, L), 0)
    si = lax.broadcasted_iota(jnp.int32, (L, L), 1)
    causal = ti >= si
    ones = jnp.ones((L, LANES), BF16)

    state = jnp.zeros((M_HEAD_DIM, M_HEAD_DIM + LANES), F32)
    for c in range(N_CHUNKS):
        rs = slice(c * L, (c + 1) * L)
        cs = slice(c, c + 1)
        qc = q_ref[rs, :]
        v1 = jnp.concatenate([v_ref[rs, :], ones], axis=1)
        ktc = kt_ref[0, 0, c]
        u_r = u_ref[cs, :]
        gcol = jnp.broadcast_to(g_ref[cs, :], (LANES, L)).T
        mcol = jnp.broadcast_to(mt_ref[cs, :], (LANES, L)).T
        dmat = jnp.where(causal, jnp.exp2(u_r - jnp.concatenate([gcol, gcol], axis=1)), 0.0)

        s = jnp.dot(qc, ktc, preferred_element_type=F32) * dmat
        tot = jnp.dot(s.astype(BF16), v1, preferred_element_type=F32)
        if c > 0:
            inter = jnp.exp2(mp_ref[cs, :LANES] - gcol)
            tot = tot + jnp.concatenate([inter, inter, inter], axis=1) \
                * jnp.dot(qc, state.astype(BF16), preferred_element_type=F32)
        den = tot[:, M_HEAD_DIM:]
        rinv = 1.0 / jnp.maximum(jnp.abs(den), jnp.exp2(-mcol))
        hval = tot[:, :M_HEAD_DIM] * jnp.concatenate([rinv, rinv], axis=1)

        hg = hval * o_ref[rs, :].astype(F32)
        ms = jnp.mean(hg * hg, axis=1, keepdims=True)
        hn = hg * lax.rsqrt(ms + NORM_EPS)
        out_ref[rs, :] = (hn * z_ref[rs, :].astype(F32)).astype(BF16)

        if c + 1 < N_CHUNKS:
            ktw = (ktc.astype(F32) * w_ref[cs, :]).astype(BF16)
            dec = dec_ref[cs, :]
            dec = jnp.concatenate([dec, dec[:, :LANES]], axis=1)
            state = dec * state + jnp.dot(ktw, v1, preferred_element_type=F32)


def _mlstm(mall, kt, gate_stats):
    blk = (SEQ, M_HEAD_DIM)

    def seg(k):
        return pl.BlockSpec(blk, lambda b, h, k=k: (b, k * M_HEADS + h))

    stat = pl.BlockSpec((N_CHUNKS, M_CHUNK), lambda b, h: (b * M_HEADS + h, 0))
    return pl.pallas_call(
        _mlstm_kernel,
        grid=(BATCH, M_HEADS),
        in_specs=[
            seg(0),
            pl.BlockSpec((1, 1, N_CHUNKS, M_HEAD_DIM, M_CHUNK), lambda b, h: (b, h, 0, 0, 0)),
            seg(2), seg(3), seg(4),
        ] + [stat] * 6,
        out_specs=pl.BlockSpec(blk, lambda b, h: (b, h)),
        out_shape=jax.ShapeDtypeStruct((BATCH * SEQ, M_WIDTH), BF16),
        compiler_params=pltpu.CompilerParams(dimension_semantics=("parallel", "parallel"),
                                             vmem_limit_bytes=VMEM_LIMIT),
        name="mlstm",
    )(mall, kt, mall, mall, mall, *gate_stats)


GATHERED = (2,)
GATHER_PITCH = A_BLOCK + 8
ATTN_WAVE = 12


def _attn_block_order():
    order = []
    for g in reversed(range(1, N_GROUPS)):
        nblk = SEQ // A_GROUPS[g][1] // A_BLOCK
        order += [(r * nblk + n, g) for n in range(nblk) for r in range(A_GROUPS[g][1])]
    order += [(i, 0) for i in range(SEQ // A_BLOCK)]
    return order


def _attn_kernel(q0, k0, v0, q1, k1, v1, q2, k2, v2, out_ref,
                 a0, a1, a2, m0, m1, m2, l0, l1, l2):
    BLK = A_BLOCK
    ti = lax.broadcasted_iota(jnp.int32, (BLK, BLK), 0)
    ji = lax.broadcasted_iota(jnp.int32, (BLK, BLK), 1)
    cur_ok = ji <= ti
    ti2 = lax.broadcasted_iota(jnp.int32, (BLK, 2 * BLK), 0)
    krel = lax.broadcasted_iota(jnp.int32, (BLK, 2 * BLK), 1) - BLK
    band_ok = jnp.logical_and(krel <= ti2, krel >= ti2 - BLK)

    qs = (q0, q1, q2)
    ks = (k0, k1, k2)
    vs = (v0, v1, v2)
    accs = (a0, a1, a2)
    ms = (m0, m1, m2)
    ls = (l0, l1, l2)

    RB = 256

    def merge(rb):
        sl = slice(rb * RB, (rb + 1) * RB)

        def rows(refs, g):
            if g in GATHERED:
                dil = A_GROUPS[g][1]
                j0 = rb * RB // dil
                return jnp.concatenate([refs[g][pl.ds(j0 + jj, dil, stride=GATHER_PITCH), :]
                                        for jj in range(RB // dil)], axis=0)
            return refs[g][sl, :]

        mg = [rows(ms, g) for g in range(N_GROUPS)]
        mmax = jnp.maximum(jnp.maximum(mg[0], mg[1]), mg[2])
        num = None
        den = None
        for g in range(N_GROUPS):
            e = jnp.exp2(mg[g] - mmax)
            num = e * rows(accs, g) if num is None else num + e * rows(accs, g)
            den = e * rows(ls, g) if den is None else den + e * rows(ls, g)
        out_ref[sl, :] = (num * (1.0 / den)).astype(BF16)

    blocks = _attn_block_order()
    done_rows = [set() for _ in range(N_GROUPS)]
    merged = 0
    for w0 in range(0, len(blocks), ATTN_WAVE):
        wave = []
        for i, g in blocks[w0:w0 + ATTN_WAVE]:
            dil = A_GROUPS[g][1]
            r, n = divmod(i, SEQ // dil // BLK)
            rows = slice(i * BLK, (i + 1) * BLK)
            keys = slice((i - 1) * BLK, (i + 1) * BLK) if n > 0 else rows
            valid = band_ok if n > 0 else cur_ok
            s = lax.dot_general(qs[g][0, 0, rows, :], ks[g][0, 0, keys, :], (((1,), (1,)), ((), ())),
                                preferred_element_type=F32)
            s = jnp.where(valid, s, NEG_INF)
            m = jnp.max(s, axis=1, keepdims=True)
            tok0 = n * BLK * dil + r
            if g in GATHERED:
                idx = pl.ds(r * GATHER_PITCH, BLK)
            else:
                idx = pl.ds(tok0, BLK, stride=dil) if dil > 1 else pl.ds(tok0, BLK)
            wave.append((g, keys, valid, idx, s, m))
        for g, keys, valid, idx, s, m in wave:
            p = jnp.where(valid, jnp.exp2(s - m), 0.0)
            nkeys = keys.stop - keys.start
            v1 = jnp.concatenate([vs[g][0, 0, keys, :], jnp.ones((nkeys, LANES), BF16)], axis=1)
            acc = jnp.dot(p.astype(BF16), v1, preferred_element_type=F32)
            accs[g][idx, :] = acc[:, :LANES]
            ms[g][idx, :] = jnp.broadcast_to(m, (BLK, LANES))
            ls[g][idx, :] = acc[:, LANES:]
        for i, g in blocks[w0:w0 + ATTN_WAVE]:
            dil = A_GROUPS[g][1]
            r, n = divmod(i, SEQ // dil // BLK)
            span = BLK * dil
            if r == dil - 1:
                done_rows[g].update(range(n * span // RB, (n + 1) * span // RB))
        while merged < SEQ // RB and all(merged in d for d in done_rows):
            merge(merged)
            merged += 1


def _attn(qkv):
    in_specs = []
    args = []
    for g in range(N_GROUPS):
        for t in range(3):
            in_specs.append(pl.BlockSpec((1, 1, SEQ, A_HEAD_DIM), lambda b, h, t=t: (b, t, 0, h)))
            args.append(qkv[g])
    return pl.pallas_call(
        _attn_kernel,
        grid=(BATCH, A_HEADS),
        in_specs=in_specs,
        out_specs=pl.BlockSpec((SEQ, A_HEAD_DIM), lambda b, h: (b, h)),
        out_shape=jax.ShapeDtypeStruct((BATCH * SEQ, A_OUT_WIDTH), BF16),
        scratch_shapes=[pltpu.VMEM((A_GROUPS[g][1] * GATHER_PITCH if g in GATHERED else SEQ, LANES), F32)
                        for _ in range(3) for g in range(N_GROUPS)],
        compiler_params=pltpu.CompilerParams(dimension_semantics=("parallel", "parallel"),
                                             vmem_limit_bytes=VMEM_LIMIT),
        name="attn",
    )(*args)


O_TM = 512


def _out_kernel(x_ref, prew_ref, hm_ref, att_ref, wg_ref, wpm_ref, wpa_ref, wout_ref, postw_ref,
                y_ref):
    x = x_ref[...]
    ms = jnp.mean(x * x, axis=-1, keepdims=True)
    h = (x * lax.rsqrt(ms + NORM_EPS) * prew_ref[...]).astype(BF16)
    gates = _dot_nt(h, wg_ref[...])
    za = _silu(gates[:, :A_OUT_WIDTH])
    sgm = _sigmoid(gates[:, A_OUT_WIDTH:A_OUT_WIDTH + D_MODEL])
    sga = _sigmoid(gates[:, A_OUT_WIDTH + D_MODEL:])
    a = (att_ref[...].astype(F32) * za).astype(BF16)
    bm = jnp.dot(hm_ref[...], wpm_ref[...], preferred_element_type=F32)
    ba = jnp.dot(a, wpa_ref[...], preferred_element_type=F32)
    merged = (sgm * bm + sga * ba).astype(BF16)
    y = jnp.dot(merged, wout_ref[...], preferred_element_type=F32)
    ms2 = jnp.mean(y * y, axis=-1, keepdims=True)
    y_ref[...] = x + y * lax.rsqrt(ms2 + NORM_EPS) * postw_ref[...]


def _out(x2d, pre_w, hm, att, wg4, wpm, wpa, wout, post_w):
    n = x2d.shape[0]
    const = lambda i: (0, 0)
    return pl.pallas_call(
        _out_kernel,
        grid=(n // O_TM,),
        in_specs=[
            pl.BlockSpec((O_TM, D_MODEL), lambda i: (i, 0)),
            pl.BlockSpec((1, D_MODEL), const),
            pl.BlockSpec((O_TM, M_WIDTH), lambda i: (i, 0)),
            pl.BlockSpec((O_TM, A_OUT_WIDTH), lambda i: (i, 0)),
            pl.BlockSpec((A_OUT_WIDTH + 2 * D_MODEL, D_MODEL), const),
            pl.BlockSpec((M_WIDTH, D_MODEL), const),
            pl.BlockSpec((A_OUT_WIDTH, D_MODEL), const),
            pl.BlockSpec((D_MODEL, D_MODEL), const),
            pl.BlockSpec((1, D_MODEL), const),
        ],
        out_specs=pl.BlockSpec((O_TM, D_MODEL), lambda i: (i, 0)),
        out_shape=jax.ShapeDtypeStruct((n, D_MODEL), F32),
        compiler_params=pltpu.CompilerParams(dimension_semantics=("parallel",),
                                             vmem_limit_bytes=VMEM_LIMIT),
        name="outproj",
    )(x2d, pre_w.reshape(1, D_MODEL), hm, att, wg4, wpm, wpa, wout, post_w.reshape(1, D_MODEL))


def _rope_tables():
    pos = np.arange(SEQ, dtype=np.float64)
    inv_freq = ROPE_THETA ** (-np.arange(0, ROPE_DIM, 2, dtype=np.float64) / ROPE_DIM)
    ang = pos[:, None] * inv_freq[None, :]
    cos, sin = np.cos(ang), np.sin(ang)
    half = ROPE_DIM // 2
    pad = A_HEAD_DIM - ROPE_DIM
    cos_t = np.concatenate([cos, cos, np.ones((SEQ, pad))], axis=1)
    sin_up = np.concatenate([np.zeros((SEQ, half)), sin, np.zeros((SEQ, pad))], axis=1)
    sin_dn = np.concatenate([-sin, np.zeros((SEQ, half + pad))], axis=1)
    tab = np.stack([cos_t, sin_up, sin_dn]).astype(np.float32)
    out = []
    for _, dil in A_GROUPS:
        ls = SEQ // dil
        out.append(jnp.asarray(
            tab.reshape(3, ls, dil, A_HEAD_DIM).transpose(0, 2, 1, 3).reshape(3, SEQ, A_HEAD_DIM)))
    return out


def _layer(x, pre_w, w_in, b_if, conv_w, conv_b, m_norm_w, w_pm, w_pa, w_out, post_w):
    x2d = x.reshape(BATCH * SEQ, D_MODEL)
    wt = w_in.T
    wt_m = wt[:OFF_GATES].astype(BF16)
    wg = jnp.pad(wt[OFF_GATES:OFF_QA], ((0, LANES - 2 * M_HEADS), (0, 0))).astype(BF16)
    w_att = []
    for g in range(N_GROUPS):
        rows = [wt[off + g * A_OUT_WIDTH: off + (g + 1) * A_OUT_WIDTH] for off in (OFF_QA, OFF_KA, OFF_VA)]
        w_att.append(jnp.stack(rows).astype(BF16))
    wg4 = wt[OFF_ZA:].astype(BF16)
    tabs = _rope_tables()

    h, h4, h16 = _norm(x2d, pre_w)
    mall, kt, gt = _mproj(h, wt_m, wg, b_if.reshape(2 * M_HEADS, 1), conv_w, conv_b.reshape(1, 2 * M_WIDTH))
    h_by_group = (h, h4.reshape(BATCH * SEQ, D_MODEL), h16.reshape(BATCH * SEQ, D_MODEL))
    qkv = [_aproj(h_by_group[g], w_att[g], tabs[g], "aproj_d%d" % A_GROUPS[g][1]) for g in range(N_GROUPS)]
    hm = _mlstm(mall, kt, _gates(gt))
    att = _attn(qkv)
    w_pm_n = (m_norm_w[:, None] * w_pm).astype(BF16)
    y = _out(x2d, pre_w, hm, att, wg4, w_pm_n, w_pa.astype(BF16), w_out.astype(BF16), post_w)
    return y.reshape(BATCH, SEQ, D_MODEL)


@jax.jit
def kernel(x, pre_norm_w, w_in, b_if, conv_w, conv_b, m_norm_w, w_proj_m, w_proj_a, w_out, post_norm_w):
    for layer in range(pre_norm_w.shape[0]):
        x = _layer(x, pre_norm_w[layer], w_in[layer], b_if[layer], conv_w[layer], conv_b[layer],
                   m_norm_w[layer], w_proj_m[layer], w_proj_a[layer], w_out[layer], post_norm_w[layer])
    return x
```

```python
import jax
import jax.numpy as jnp
import numpy as np
from jax import lax
from jax.experimental import pallas as pl
from jax.experimental.pallas import tpu as pltpu

D_MODEL = 1024
BATCH = 8
SEQ = 2048
M_WIDTH = 1024
M_HEADS = 4
M_HEAD_DIM = 256
CONV_WIDTH = 4
A_GROUPS = ((128, 1), (512, 4), (2048, 16))
N_GROUPS = 3
A_HEADS = 4
A_HEAD_DIM = 128
A_OUT_WIDTH = 512
A_QKV_WIDTH = 1536
A_BLOCK = 128
ROPE_DIM = 32
ROPE_THETA = 500000.0
NORM_EPS = 1e-6
NEG_INF = -1e30
LOG2E = 1.4426950408889634

OFF_GATES = 5 * M_WIDTH
OFF_QA = OFF_GATES + 2 * M_HEADS
OFF_KA = OFF_QA + A_QKV_WIDTH
OFF_VA = OFF_KA + A_QKV_WIDTH
OFF_ZA = OFF_VA + A_QKV_WIDTH
OFF_GM = OFF_ZA + A_OUT_WIDTH
IN_WIDTH = OFF_GM + 2 * D_MODEL

LANES = 128
M_CHUNK = 256
N_CHUNKS = SEQ // M_CHUNK
VMEM_LIMIT = 48 * 1024 * 1024

F32 = jnp.float32
BF16 = jnp.bfloat16


def _dot_nt(a, bt):
    return lax.dot_general(a, bt, (((1,), (1,)), ((), ())), preferred_element_type=F32)


def _sigmoid(x):
    return 1.0 / (1.0 + jnp.exp(-x))


def _silu(x):
    return x * _sigmoid(x)


M_TN = 512


SUBLANES = 8


def _causal_conv(acc, cw, cb):
    def taps(x, shifted):
        y = x * cw[CONV_WIDTH - 1:CONV_WIDTH, :] + cb
        for s in range(1, CONV_WIDTH):
            y = y + shifted(x, s) * cw[CONV_WIDTH - 1 - s:CONV_WIDTH - s, :]
        return y

    body = taps(acc, lambda x, s: pltpu.roll(x, s, axis=0))
    top = acc[:SUBLANES, :]
    rows = lax.broadcasted_iota(jnp.int32, top.shape, 0)
    head = taps(top, lambda x, s: jnp.where(rows >= s, pltpu.roll(x, s, axis=0), 0.0))
    return jnp.concatenate([head, body[SUBLANES:, :]], axis=0)


M_TILES = 5 * M_WIDTH // M_TN


def _mproj_kernel(x_ref, prew_ref, w_ref, wg_ref, bif_ref, cw_ref, cb_ref,
                  mall_ref, kt_ref, gt_ref, h_ref):
    j = pl.program_id(1)

    def matmul():
        return _dot_nt(h_ref[...], w_ref[...])

    @pl.when(j == 0)
    def _():
        x = x_ref[...]
        ms = jnp.mean(x * x, axis=-1, keepdims=True)
        h_ref[...] = (x * lax.rsqrt(ms + NORM_EPS) * prew_ref[...]).astype(BF16)
        g = _dot_nt(h_ref[...], wg_ref[...])
        gt = g.T[:2 * M_HEADS, :] + bif_ref[...]
        for c in range(N_CHUNKS):
            gt_ref[0, :, c, :] = gt[:, c * M_CHUNK:(c + 1) * M_CHUNK]

    @pl.when(j < 2)
    def _():
        y = _causal_conv(matmul(), cw_ref[...], cb_ref[...])
        mall_ref[...] = _silu(y).astype(BF16)

    @pl.when(jnp.logical_and(j >= 2, j < 4))
    def _():
        y = _causal_conv(matmul(), cw_ref[...], cb_ref[...])
        y = _silu(y) * (M_HEAD_DIM ** -0.5)
        mall_ref[...] = y.astype(BF16)
        for hh in range(M_TN // M_HEAD_DIM):
            for c in range(N_CHUNKS):
                blk = y[c * M_CHUNK:(c + 1) * M_CHUNK, hh * M_HEAD_DIM:(hh + 1) * M_HEAD_DIM]
                kt_ref[0, hh, c] = blk.T.astype(BF16)

    @pl.when(jnp.logical_and(j >= 4, j < 6))
    def _():
        mall_ref[...] = matmul().astype(BF16)

    @pl.when(jnp.logical_and(j >= 6, j < 8))
    def _():
        mall_ref[...] = _sigmoid(matmul()).astype(BF16)

    @pl.when(j >= 8)
    def _():
        mall_ref[...] = _silu(matmul()).astype(BF16)


def _mproj(x2d, pre_w, wt_m, wg, bif, conv_w, conv_b):
    return pl.pallas_call(
        _mproj_kernel,
        grid=(BATCH, M_TILES),
        in_specs=[
            pl.BlockSpec((SEQ, D_MODEL), lambda i, j: (i, 0)),
            pl.BlockSpec((1, D_MODEL), lambda i, j: (0, 0)),
            pl.BlockSpec((M_TN, D_MODEL), lambda i, j: (j, 0)),
            pl.BlockSpec((LANES, D_MODEL), lambda i, j: (0, 0)),
            pl.BlockSpec((2 * M_HEADS, 1), lambda i, j: (0, 0)),
            pl.BlockSpec((CONV_WIDTH, M_TN), lambda i, j: (0, jnp.minimum(j, 3))),
            pl.BlockSpec((1, M_TN), lambda i, j: (0, jnp.minimum(j, 3))),
        ],
        out_specs=[
            pl.BlockSpec((SEQ, M_TN), lambda i, j: (i, j)),
            pl.BlockSpec((1, M_TN // M_HEAD_DIM, N_CHUNKS, M_HEAD_DIM, M_CHUNK),
                         lambda i, j: (i, jnp.clip(j - 2, 0, 1), 0, 0, 0)),
            pl.BlockSpec((1, 2 * M_HEADS, N_CHUNKS, M_CHUNK), lambda i, j: (i, 0, 0, 0)),
            pl.BlockSpec((SEQ, D_MODEL), lambda i, j: (i, 0)),
        ],
        out_shape=[
            jax.ShapeDtypeStruct((BATCH * SEQ, 5 * M_WIDTH), BF16),
            jax.ShapeDtypeStruct((BATCH, M_HEADS, N_CHUNKS, M_HEAD_DIM, M_CHUNK), BF16),
            jax.ShapeDtypeStruct((BATCH, 2 * M_HEADS, N_CHUNKS, M_CHUNK), F32),
            jax.ShapeDtypeStruct((BATCH * SEQ, D_MODEL), BF16),
        ],
        compiler_params=pltpu.CompilerParams(dimension_semantics=("parallel", "arbitrary"),
                                             vmem_limit_bytes=VMEM_LIMIT),
        name="mproj",
    )(x2d, pre_w.reshape(1, D_MODEL), wt_m, wg, bif, conv_w, conv_b)


A_TILES = 3 * N_GROUPS


def _aproj_kernel(h_ref, w_ref, tab_ref, o_ref, acc0_ref, acc1_ref):
    s = pl.program_id(0)
    last = BATCH * A_TILES
    prev = (jnp.maximum(s, 1) - 1) % A_TILES
    pg, pt = prev // 3, prev % 3

    accs = (acc0_ref, acc1_ref)

    def matmul(slot):
        res = _dot_nt(h_ref[...], w_ref[0])
        for hd in range(A_HEADS):
            accs[slot][hd] = res[:, hd * A_HEAD_DIM:(hd + 1) * A_HEAD_DIM]

    def residue_rows(slot, hd, dil, r):
        ls = SEQ // dil
        if dil == 1:
            return accs[slot][hd]
        return accs[slot][hd, pl.ds(r, ls, stride=dil), :]

    def rope_epilogue(slot, dil):
        scale = jnp.where(pt == 0, A_HEAD_DIM ** -0.5 * LOG2E, 1.0).astype(F32)
        half = ROPE_DIM // 2
        ls = SEQ // dil
        for hd in range(A_HEADS):
            for r in range(dil):
                rows = slice(r * ls, (r + 1) * ls)
                xh = residue_rows(slot, hd, dil, r)
                y = (xh * tab_ref[0, 0, rows, :] + pltpu.roll(xh, half, axis=1) * tab_ref[0, 1, rows, :]
                     + pltpu.roll(xh, A_HEAD_DIM - half, axis=1) * tab_ref[0, 2, rows, :])
                o_ref[0, 0, 0, rows, hd * A_HEAD_DIM:(hd + 1) * A_HEAD_DIM] = (y * scale).astype(BF16)

    def cast_epilogue(slot, dil):
        ls = SEQ // dil
        for hd in range(A_HEADS):
            for r in range(dil):
                rows = slice(r * ls, (r + 1) * ls)
                o_ref[0, 0, 0, rows, hd * A_HEAD_DIM:(hd + 1) * A_HEAD_DIM] = \
                    residue_rows(slot, hd, dil, r).astype(BF16)

    mid = jnp.logical_and(s > 0, s < last)

    @pl.when(s == 0)
    def _():
        matmul(0)

    for slot in range(2):
        mine = jnp.logical_and(mid, s % 2 == slot)
        for g, (_, dil) in enumerate(A_GROUPS):
            in_group = jnp.logical_and(mine, pg == g)

            @pl.when(jnp.logical_and(in_group, pt < 2))
            def _(slot=slot, dil=dil):
                rope_epilogue(1 - slot, dil)
                matmul(slot)

            @pl.when(jnp.logical_and(in_group, pt == 2))
            def _(slot=slot, dil=dil):
                cast_epilogue(1 - slot, dil)
                matmul(slot)

    @pl.when(s == last)
    def _():
        cast_epilogue((last - 1) % 2, A_GROUPS[-1][1])


def _aproj(h, w_att, tabs):
    last = BATCH * A_TILES

    def cur(s):
        return jnp.minimum(s, last - 1)

    def prev(s):
        return jnp.maximum(s, 1) - 1

    return pl.pallas_call(
        _aproj_kernel,
        grid=(last + 1,),
        in_specs=[
            pl.BlockSpec((SEQ, D_MODEL), lambda s: (cur(s) // A_TILES, 0)),
            pl.BlockSpec((1, A_OUT_WIDTH, D_MODEL), lambda s: (cur(s) % A_TILES, 0, 0)),
            pl.BlockSpec((1, 3, SEQ, LANES), lambda s: (prev(s) % A_TILES // 3, 0, 0, 0)),
        ],
        out_specs=pl.BlockSpec((1, 1, 1, SEQ, A_OUT_WIDTH),
                               lambda s: (prev(s) // A_TILES, prev(s) % A_TILES // 3, prev(s) % 3, 0, 0)),
        out_shape=jax.ShapeDtypeStruct((BATCH, N_GROUPS, 3, SEQ, A_OUT_WIDTH), BF16),
        scratch_shapes=[pltpu.VMEM((A_HEADS, SEQ, A_HEAD_DIM), F32), pltpu.VMEM((A_HEADS, SEQ, A_HEAD_DIM), F32)],
        compiler_params=pltpu.CompilerParams(dimension_semantics=("arbitrary",),
                                             vmem_limit_bytes=VMEM_LIMIT),
        name="aproj",
    )(h, w_att, tabs)


def _lane_scan(x, op, fill):
    lane = lax.broadcasted_iota(jnp.int32, x.shape, 1)
    sh = 1
    while sh < x.shape[1]:
        shifted = jnp.where(lane >= sh, pltpu.roll(x, sh, axis=1), fill)
        x = op(x, shifted)
        sh *= 2
    return x


def _gates_kernel(ig_ref, fg_ref, u_ref, g_ref, mt_ref, mp_ref, w_ref, dec_ref):
    L = M_CHUNK
    ig = ig_ref[...]
    fg = fg_ref[...]
    logf = -(jnp.maximum(-fg, 0.0) + jnp.log1p(jnp.exp(-jnp.abs(fg))))
    b = _lane_scan(logf, jnp.add, 0.0)
    u = ig - b
    cm = _lane_scan(u, jnp.maximum, NEG_INF)
    lane = lax.broadcasted_iota(jnp.int32, ig.shape, 1)
    btot = jnp.broadcast_to(jnp.sum(jnp.where(lane == L - 1, b, 0.0), axis=1, keepdims=True), ig.shape)
    umax = jnp.broadcast_to(jnp.max(cm, axis=1, keepdims=True), ig.shape)
    chunk = lax.broadcasted_iota(jnp.int32, ig.shape, 0) % N_CHUNKS
    m = jnp.zeros(ig.shape, F32)
    for c in range(N_CHUNKS - 1):
        nxt = pltpu.roll(btot + jnp.maximum(m, umax), 1, axis=0)
        m = jnp.where(chunk == c + 1, nxt, m)
    g = jnp.maximum(m, cm)
    glast = jnp.max(g, axis=1, keepdims=True)
    u_ref[...] = u * LOG2E
    g_ref[...] = g * LOG2E
    mt_ref[...] = (b + g) * LOG2E
    mp_ref[...] = m * LOG2E
    w_ref[...] = jnp.exp(u - glast)
    dec_ref[...] = jnp.broadcast_to(jnp.exp(m - glast), ig.shape)


def _gates(gt):
    rows = BATCH * M_HEADS * N_CHUNKS
    ig = gt[:, :M_HEADS].reshape(rows, M_CHUNK)
    fg = gt[:, M_HEADS:].reshape(rows, M_CHUNK)
    spec = pl.BlockSpec((rows, M_CHUNK), lambda: (0, 0))
    return pl.pallas_call(
        _gates_kernel,
        in_specs=[spec, spec],
        out_specs=[spec] * 6,
        out_shape=[jax.ShapeDtypeStruct((rows, M_CHUNK), F32)] * 6,
        name="gates",
    )(ig, fg)


def _mlstm_kernel(q_ref, kt_ref, v_ref, o_ref, z_ref, u_ref, g_ref, mt_ref, mp_ref, w_ref, dec_ref,
                  out_ref):
    L = M_CHUNK
    ti = lax.broadcasted_iota(jnp.int32, (L, L), 0)
    si = lax.broadcasted_iota(jnp.int32, (L, L), 1)
    causal = ti >= si
    ones = jnp.ones((L, LANES), BF16)

    state = jnp.zeros((M_HEAD_DIM, M_HEAD_DIM + LANES), F32)
    for c in range(N_CHUNKS):
        rs = slice(c * L, (c + 1) * L)
        cs = slice(c, c + 1)
        qc = q_ref[rs, :]
        v1 = jnp.concatenate([v_ref[rs, :], ones], axis=1)
        ktc = kt_ref[0, 0, c]
        u_r = u_ref[cs, :]
        gcol = jnp.broadcast_to(g_ref[cs, :], (LANES, L)).T
        mcol = jnp.broadcast_to(mt_ref[cs, :], (LANES, L)).T
        dmat = jnp.where(causal, jnp.exp2(u_r - jnp.concatenate([gcol, gcol], axis=1)), 0.0)

        s = jnp.dot(qc, ktc, preferred_element_type=F32) * dmat
        tot = jnp.dot(s.astype(BF16), v1, preferred_element_type=F32)
        if c > 0:
            inter = jnp.exp2(mp_ref[cs, :LANES] - gcol)
            tot = tot + jnp.concatenate([inter, inter, inter], axis=1) \
                * jnp.dot(qc, state.astype(BF16), preferred_element_type=F32)
        den = tot[:, M_HEAD_DIM:]
        rinv = 1.0 / jnp.maximum(jnp.abs(den), jnp.exp2(-mcol))
        hval = tot[:, :M_HEAD_DIM] * jnp.concatenate([rinv, rinv], axis=1)

        hg = hval * o_ref[rs, :].astype(F32)
        ms = jnp.mean(hg * hg, axis=1, keepdims=True)
        hn = hg * lax.rsqrt(ms + NORM_EPS)
        out_ref[rs, :] = (hn * z_ref[rs, :].astype(F32)).astype(BF16)

        if c + 1 < N_CHUNKS:
            ktw = (ktc.astype(F32) * w_ref[cs, :]).astype(BF16)
            dec = dec_ref[cs, :]
            dec = jnp.concatenate([dec, dec[:, :LANES]], axis=1)
            state = dec * state + jnp.dot(ktw, v1, preferred_element_type=F32)


def _mlstm(mall, kt, gate_stats):
    blk = (SEQ, M_HEAD_DIM)

    def seg(k):
        return pl.BlockSpec(blk, lambda b, h, k=k: (b, k * M_HEADS + h))

    stat = pl.BlockSpec((N_CHUNKS, M_CHUNK), lambda b, h: (b * M_HEADS + h, 0))
    return pl.pallas_call(
        _mlstm_kernel,
        grid=(BATCH, M_HEADS),
        in_specs=[
            seg(0),
            pl.BlockSpec((1, 1, N_CHUNKS, M_HEAD_DIM, M_CHUNK), lambda b, h: (b, h, 0, 0, 0)),
            seg(2), seg(3), seg(4),
        ] + [stat] * 6,
        out_specs=pl.BlockSpec(blk, lambda b, h: (b, h)),
        out_shape=jax.ShapeDtypeStruct((BATCH * SEQ, M_WIDTH), BF16),
        compiler_params=pltpu.CompilerParams(dimension_semantics=("parallel", "parallel"),
                                             vmem_limit_bytes=VMEM_LIMIT),
        name="mlstm",
    )(mall, kt, mall, mall, mall, *gate_stats)


GATHERED = (2,)
GATHER_PITCH = A_BLOCK + 8
ATTN_WAVE = 12


def _attn_block_order():
    order = []
    for g in reversed(range(1, N_GROUPS)):
        nblk = SEQ // A_GROUPS[g][1] // A_BLOCK
        order += [(r * nblk + n, g) for n in range(nblk) for r in range(A_GROUPS[g][1])]
    order += [(i, 0) for i in range(SEQ // A_BLOCK)]
    return order


def _attn_kernel(q0, k0, v0, q1, k1, v1, q2, k2, v2, out_ref,
                 a0, a1, a2, m0, m1, m2, l0, l1, l2):
    BLK = A_BLOCK
    ti = lax.broadcasted_iota(jnp.int32, (BLK, BLK), 0)
    ji = lax.broadcasted_iota(jnp.int32, (BLK, BLK), 1)
    cur_ok = ji <= ti
    ti2 = lax.broadcasted_iota(jnp.int32, (BLK, 2 * BLK), 0)
    krel = lax.broadcasted_iota(jnp.int32, (BLK, 2 * BLK), 1) - BLK
    band_ok = jnp.logical_and(krel <= ti2, krel >= ti2 - BLK)

    qs = (q0, q1, q2)
    ks = (k0, k1, k2)
    vs = (v0, v1, v2)
    accs = (a0, a1, a2)
    ms = (m0, m1, m2)
    ls = (l0, l1, l2)

    RB = 256

    def merge(rb):
        sl = slice(rb * RB, (rb + 1) * RB)

        def rows(refs, g):
            if g in GATHERED:
                dil = A_GROUPS[g][1]
                j0 = rb * RB // dil
                return jnp.concatenate([refs[g][pl.ds(j0 + jj, dil, stride=GATHER_PITCH), :]
                                        for jj in range(RB // dil)], axis=0)
            return refs[g][sl, :]

        mg = [rows(ms, g) for g in range(N_GROUPS)]
        mmax = jnp.maximum(jnp.maximum(mg[0], mg[1]), mg[2])
        num = None
        den = None
        for g in range(N_GROUPS):
            e = jnp.exp2(mg[g] - mmax)
            num = e * rows(accs, g) if num is None else num + e * rows(accs, g)
            den = e * rows(ls, g) if den is None else den + e * rows(ls, g)
        out_ref[sl, :] = (num * (1.0 / den)).astype(BF16)

    blocks = _attn_block_order()
    done_rows = [set() for _ in range(N_GROUPS)]
    merged = 0
    for w0 in range(0, len(blocks), ATTN_WAVE):
        wave = []
        for i, g in blocks[w0:w0 + ATTN_WAVE]:
            dil = A_GROUPS[g][1]
            r, n = divmod(i, SEQ // dil // BLK)
            rows = slice(i * BLK, (i + 1) * BLK)
            keys = slice((i - 1) * BLK, (i + 1) * BLK) if n > 0 else rows
            valid = band_ok if n > 0 else cur_ok
            s = lax.dot_general(qs[g][0, 0, 0, rows, :], ks[g][0, 0, 0, keys, :], (((1,), (1,)), ((), ())),
                                preferred_element_type=F32)
            s = jnp.where(valid, s, NEG_INF)
            m = jnp.max(s, axis=1, keepdims=True)
            tok0 = n * BLK * dil + r
            if g in GATHERED:
                idx = pl.ds(r * GATHER_PITCH, BLK)
            else:
                idx = pl.ds(tok0, BLK, stride=dil) if dil > 1 else pl.ds(tok0, BLK)
            wave.append((g, keys, valid, idx, s, m))
        for g, keys, valid, idx, s, m in wave:
            p = jnp.where(valid, jnp.exp2(s - m), 0.0)
            nkeys = keys.stop - keys.start
            v1 = jnp.concatenate([vs[g][0, 0, 0, keys, :], jnp.ones((nkeys, LANES), BF16)], axis=1)
            acc = jnp.dot(p.astype(BF16), v1, preferred_element_type=F32)
            accs[g][idx, :] = acc[:, :LANES]
            ms[g][idx, :] = jnp.broadcast_to(m, (BLK, LANES))
            ls[g][idx, :] = acc[:, LANES:]
        for i, g in blocks[w0:w0 + ATTN_WAVE]:
            dil = A_GROUPS[g][1]
            r, n = divmod(i, SEQ // dil // BLK)
            span = BLK * dil
            if r == dil - 1:
                done_rows[g].update(range(n * span // RB, (n + 1) * span // RB))
        while merged < SEQ // RB and all(merged in d for d in done_rows):
            merge(merged)
            merged += 1


def _attn(qkv):
    in_specs = []
    args = []
    for g in range(N_GROUPS):
        for t in range(3):
            in_specs.append(pl.BlockSpec((1, 1, 1, SEQ, A_HEAD_DIM), lambda b, h, g=g, t=t: (b, g, t, 0, h)))
            args.append(qkv)
    return pl.pallas_call(
        _attn_kernel,
        grid=(BATCH, A_HEADS),
        in_specs=in_specs,
        out_specs=pl.BlockSpec((SEQ, A_HEAD_DIM), lambda b, h: (b, h)),
        out_shape=jax.ShapeDtypeStruct((BATCH * SEQ, A_OUT_WIDTH), BF16),
        scratch_shapes=[pltpu.VMEM((A_GROUPS[g][1] * GATHER_PITCH if g in GATHERED else SEQ, LANES), F32)
                        for _ in range(3) for g in range(N_GROUPS)],
        compiler_params=pltpu.CompilerParams(dimension_semantics=("parallel", "parallel"),
                                             vmem_limit_bytes=VMEM_LIMIT),
        name="attn",
    )(*args)


O_TM = 512


def _out_kernel(x_ref, prew_ref, hm_ref, att_ref, wg_ref, wpm_ref, wpa_ref, wout_ref, postw_ref,
                y_ref):
    x = x_ref[...]
    ms = jnp.mean(x * x, axis=-1, keepdims=True)
    h = (x * lax.rsqrt(ms + NORM_EPS) * prew_ref[...]).astype(BF16)
    gates = _dot_nt(h, wg_ref[...])
    za = _silu(gates[:, :A_OUT_WIDTH])
    sgm = _sigmoid(gates[:, A_OUT_WIDTH:A_OUT_WIDTH + D_MODEL])
    sga = _sigmoid(gates[:, A_OUT_WIDTH + D_MODEL:])
    a = (att_ref[...].astype(F32) * za).astype(BF16)
    bm = jnp.dot(hm_ref[...], wpm_ref[...], preferred_element_type=F32)
    ba = jnp.dot(a, wpa_ref[...], preferred_element_type=F32)
    merged = (sgm * bm + sga * ba).astype(BF16)
    y = jnp.dot(merged, wout_ref[...], preferred_element_type=F32)
    ms2 = jnp.mean(y * y, axis=-1, keepdims=True)
    y_ref[...] = x + y * lax.rsqrt(ms2 + NORM_EPS) * postw_ref[...]


def _out(x2d, pre_w, hm, att, wg4, wpm, wpa, wout, post_w):
    n = x2d.shape[0]
    const = lambda i: (0, 0)
    return pl.pallas_call(
        _out_kernel,
        grid=(n // O_TM,),
        in_specs=[
            pl.BlockSpec((O_TM, D_MODEL), lambda i: (i, 0)),
            pl.BlockSpec((1, D_MODEL), const),
            pl.BlockSpec((O_TM, M_WIDTH), lambda i: (i, 0)),
            pl.BlockSpec((O_TM, A_OUT_WIDTH), lambda i: (i, 0)),
            pl.BlockSpec((A_OUT_WIDTH + 2 * D_MODEL, D_MODEL), const),
            pl.BlockSpec((M_WIDTH, D_MODEL), const),
            pl.BlockSpec((A_OUT_WIDTH, D_MODEL), const),
            pl.BlockSpec((D_MODEL, D_MODEL), const),
            pl.BlockSpec((1, D_MODEL), const),
        ],
        out_specs=pl.BlockSpec((O_TM, D_MODEL), lambda i: (i, 0)),
        out_shape=jax.ShapeDtypeStruct((n, D_MODEL), F32),
        compiler_params=pltpu.CompilerParams(dimension_semantics=("parallel",),
                                             vmem_limit_bytes=VMEM_LIMIT),
        name="outproj",
    )(x2d, pre_w.reshape(1, D_MODEL), hm, att, wg4, wpm, wpa, wout, post_w.reshape(1, D_MODEL))


def _rope_tables():
    pos = np.arange(SEQ, dtype=np.float64)
    inv_freq = ROPE_THETA ** (-np.arange(0, ROPE_DIM, 2, dtype=np.float64) / ROPE_DIM)
    ang = pos[:, None] * inv_freq[None, :]
    cos, sin = np.cos(ang), np.sin(ang)
    half = ROPE_DIM // 2
    pad = A_HEAD_DIM - ROPE_DIM
    cos_t = np.concatenate([cos, cos, np.ones((SEQ, pad))], axis=1)
    sin_up = np.concatenate([np.zeros((SEQ, half)), sin, np.zeros((SEQ, pad))], axis=1)
    sin_dn = np.concatenate([-sin, np.zeros((SEQ, half + pad))], axis=1)
    tab = np.stack([cos_t, sin_up, sin_dn]).astype(np.float32)
    out = []
    for _, dil in A_GROUPS:
        ls = SEQ // dil
        out.append(tab.reshape(3, ls, dil, A_HEAD_DIM).transpose(0, 2, 1, 3).reshape(3, SEQ, A_HEAD_DIM))
    return jnp.asarray(np.stack(out))


def _layer(x, pre_w, w_in, b_if, conv_w, conv_b, m_norm_w, w_pm, w_pa, w_out, post_w):
    x2d = x.reshape(BATCH * SEQ, D_MODEL)
    wt = w_in.T
    wt_m = wt[:OFF_GATES].astype(BF16)
    wg = jnp.pad(wt[OFF_GATES:OFF_QA], ((0, LANES - 2 * M_HEADS), (0, 0))).astype(BF16)
    w_att = jnp.stack([wt[off + g * A_OUT_WIDTH: off + (g + 1) * A_OUT_WIDTH]
                       for g in range(N_GROUPS) for off in (OFF_QA, OFF_KA, OFF_VA)]).astype(BF16)
    wg4 = wt[OFF_ZA:].astype(BF16)
    tabs = _rope_tables()

    mall, kt, gt, h = _mproj(x2d, pre_w, wt_m, wg, b_if.reshape(2 * M_HEADS, 1), conv_w,
                             conv_b.reshape(1, 2 * M_WIDTH))
    qkv = _aproj(h, w_att, tabs)
    hm = _mlstm(mall, kt, _gates(gt))
    att = _attn(qkv)
    w_pm_n = (m_norm_w[:, None] * w_pm).astype(BF16)
    y = _out(x2d, pre_w, hm, att, wg4, w_pm_n, w_pa.astype(BF16), w_out.astype(BF16), post_w)
    return y.reshape(BATCH, SEQ, D_MODEL)


@jax.jit
def kernel(x, pre_norm_w, w_in, b_if, conv_w, conv_b, m_norm_w, w_proj_m, w_proj_a, w_out, post_norm_w):
    for layer in range(pre_norm_w.shape[0]):
        x = _layer(x, pre_norm_w[layer], w_in[layer], b_if[layer], conv_w[layer], conv_b[layer],
                   m_norm_w[layer], w_proj_m[layer], w_proj_a[layer], w_out[layer], post_norm_w[layer])
    return x
```

```python
import functools

import jax
import jax.numpy as jnp
import numpy as np
from jax import lax
from jax.experimental import pallas as pl
from jax.experimental.pallas import tpu as pltpu

D_MODEL = 1024
BATCH = 8
SEQ = 2048
M_WIDTH = 1024
M_HEADS = 4
M_HEAD_DIM = 256
CONV_WIDTH = 4
A_GROUPS = ((128, 1), (512, 4), (2048, 16))
N_GROUPS = 3
A_HEADS = 4
A_HEAD_DIM = 128
A_OUT_WIDTH = 512
A_QKV_WIDTH = 1536
A_BLOCK = 128
ROPE_DIM = 32
ROPE_THETA = 500000.0
NORM_EPS = 1e-6
NEG_INF = -1e30
LOG2E = 1.4426950408889634

OFF_GATES = 5 * M_WIDTH
OFF_QA = OFF_GATES + 2 * M_HEADS
OFF_KA = OFF_QA + A_QKV_WIDTH
OFF_VA = OFF_KA + A_QKV_WIDTH
OFF_ZA = OFF_VA + A_QKV_WIDTH
OFF_GM = OFF_ZA + A_OUT_WIDTH
IN_WIDTH = OFF_GM + 2 * D_MODEL

LANES = 128
M_CHUNK = 256
N_CHUNKS = SEQ // M_CHUNK
VMEM_LIMIT = 48 * 1024 * 1024

F32 = jnp.float32
BF16 = jnp.bfloat16


def _dot_nt(a, bt):
    return lax.dot_general(a, bt, (((1,), (1,)), ((), ())), preferred_element_type=F32)


def _sigmoid(x):
    return 1.0 / (1.0 + jnp.exp(-x))


def _silu(x):
    return x * _sigmoid(x)


M_TN = 512


SUBLANES = 8


def _causal_conv(acc, cw, cb):
    def taps(x, shifted):
        y = x * cw[CONV_WIDTH - 1:CONV_WIDTH, :] + cb
        for s in range(1, CONV_WIDTH):
            y = y + shifted(x, s) * cw[CONV_WIDTH - 1 - s:CONV_WIDTH - s, :]
        return y

    body = taps(acc, lambda x, s: pltpu.roll(x, s, axis=0))
    top = acc[:SUBLANES, :]
    rows = lax.broadcasted_iota(jnp.int32, top.shape, 0)
    head = taps(top, lambda x, s: jnp.where(rows >= s, pltpu.roll(x, s, axis=0), 0.0))
    return jnp.concatenate([head, body[SUBLANES:, :]], axis=0)


M_TILES = 5 * M_WIDTH // M_TN


def _mproj_kernel(x_ref, prew_ref, w_ref, wg_ref, bif_ref, cw_ref, cb_ref,
                  mall_ref, kt_ref, gt_ref, h_ref):
    j = pl.program_id(1)

    def matmul():
        return _dot_nt(h_ref[...], w_ref[...])

    @pl.when(j == 0)
    def _():
        x = x_ref[...]
        ms = jnp.mean(x * x, axis=-1, keepdims=True)
        h_ref[...] = (x * lax.rsqrt(ms + NORM_EPS) * prew_ref[...]).astype(BF16)
        g = _dot_nt(h_ref[...], wg_ref[...])
        gt = g.T[:2 * M_HEADS, :] + bif_ref[...]
        for c in range(N_CHUNKS):
            gt_ref[0, :, c, :] = gt[:, c * M_CHUNK:(c + 1) * M_CHUNK]

    @pl.when(j < 2)
    def _():
        y = _causal_conv(matmul(), cw_ref[...], cb_ref[...])
        mall_ref[...] = _silu(y).astype(BF16)

    @pl.when(jnp.logical_and(j >= 2, j < 4))
    def _():
        y = _causal_conv(matmul(), cw_ref[...], cb_ref[...])
        y = _silu(y) * (M_HEAD_DIM ** -0.5)
        mall_ref[...] = y.astype(BF16)
        for hh in range(M_TN // M_HEAD_DIM):
            for c in range(N_CHUNKS):
                blk = y[c * M_CHUNK:(c + 1) * M_CHUNK, hh * M_HEAD_DIM:(hh + 1) * M_HEAD_DIM]
                kt_ref[0, hh, c] = blk.T.astype(BF16)

    @pl.when(jnp.logical_and(j >= 4, j < 6))
    def _():
        mall_ref[...] = matmul().astype(BF16)

    @pl.when(jnp.logical_and(j >= 6, j < 8))
    def _():
        mall_ref[...] = _sigmoid(matmul()).astype(BF16)

    @pl.when(j >= 8)
    def _():
        mall_ref[...] = _silu(matmul()).astype(BF16)


def _mproj(x2d, pre_w, wt_m, wg, bif, conv_w, conv_b):
    return pl.pallas_call(
        _mproj_kernel,
        grid=(BATCH, M_TILES),
        in_specs=[
            pl.BlockSpec((SEQ, D_MODEL), lambda i, j: (i, 0)),
            pl.BlockSpec((1, D_MODEL), lambda i, j: (0, 0)),
            pl.BlockSpec((M_TN, D_MODEL), lambda i, j: (j, 0)),
            pl.BlockSpec((LANES, D_MODEL), lambda i, j: (0, 0)),
            pl.BlockSpec((2 * M_HEADS, 1), lambda i, j: (0, 0)),
            pl.BlockSpec((CONV_WIDTH, M_TN), lambda i, j: (0, jnp.minimum(j, 3))),
            pl.BlockSpec((1, M_TN), lambda i, j: (0, jnp.minimum(j, 3))),
        ],
        out_specs=[
            pl.BlockSpec((SEQ, M_TN), lambda i, j: (i, j)),
            pl.BlockSpec((1, M_TN // M_HEAD_DIM, N_CHUNKS, M_HEAD_DIM, M_CHUNK),
                         lambda i, j: (i, jnp.clip(j - 2, 0, 1), 0, 0, 0)),
            pl.BlockSpec((1, 2 * M_HEADS, N_CHUNKS, M_CHUNK), lambda i, j: (i, 0, 0, 0)),
            pl.BlockSpec((SEQ, D_MODEL), lambda i, j: (i, 0)),
        ],
        out_shape=[
            jax.ShapeDtypeStruct((BATCH * SEQ, 5 * M_WIDTH), BF16),
            jax.ShapeDtypeStruct((BATCH, M_HEADS, N_CHUNKS, M_HEAD_DIM, M_CHUNK), BF16),
            jax.ShapeDtypeStruct((BATCH, 2 * M_HEADS, N_CHUNKS, M_CHUNK), F32),
            jax.ShapeDtypeStruct((BATCH * SEQ, D_MODEL), BF16),
        ],
        compiler_params=pltpu.CompilerParams(dimension_semantics=("parallel", "arbitrary"),
                                             vmem_limit_bytes=VMEM_LIMIT),
        name="mproj",
    )(x2d, pre_w.reshape(1, D_MODEL), wt_m, wg, bif, conv_w, conv_b)


A_TILES = 3


def _aproj_kernel(dil, h_ref, w_ref, tab_ref, o_ref, acc0_ref, acc1_ref):
    s = pl.program_id(0)
    last = BATCH * A_TILES
    pt = (jnp.maximum(s, 1) - 1) % A_TILES

    accs = (acc0_ref, acc1_ref)

    def matmul(slot):
        res = _dot_nt(h_ref[...], w_ref[0])
        for hd in range(A_HEADS):
            accs[slot][hd] = res[:, hd * A_HEAD_DIM:(hd + 1) * A_HEAD_DIM]

    ls = SEQ // dil

    def residue_rows(slot, hd, r):
        if dil == 1:
            return accs[slot][hd]
        return accs[slot][hd, pl.ds(r, ls, stride=dil), :]

    def rope_epilogue(slot):
        scale = jnp.where(pt == 0, A_HEAD_DIM ** -0.5 * LOG2E, 1.0).astype(F32)
        half = ROPE_DIM // 2
        for hd in range(A_HEADS):
            for r in range(dil):
                rows = slice(r * ls, (r + 1) * ls)
                xh = residue_rows(slot, hd, r)
                y = (xh * tab_ref[0, rows, :] + pltpu.roll(xh, half, axis=1) * tab_ref[1, rows, :]
                     + pltpu.roll(xh, A_HEAD_DIM - half, axis=1) * tab_ref[2, rows, :])
                o_ref[0, 0, rows, hd * A_HEAD_DIM:(hd + 1) * A_HEAD_DIM] = (y * scale).astype(BF16)

    def cast_epilogue(slot):
        for hd in range(A_HEADS):
            for r in range(dil):
                rows = slice(r * ls, (r + 1) * ls)
                o_ref[0, 0, rows, hd * A_HEAD_DIM:(hd + 1) * A_HEAD_DIM] = \
                    residue_rows(slot, hd, r).astype(BF16)

    mid = jnp.logical_and(s > 0, s < last)

    @pl.when(s == 0)
    def _():
        matmul(0)

    for slot in range(2):
        mine = jnp.logical_and(mid, s % 2 == slot)

        @pl.when(jnp.logical_and(mine, pt < 2))
        def _(slot=slot):
            rope_epilogue(1 - slot)
            matmul(slot)

        @pl.when(jnp.logical_and(mine, pt == 2))
        def _(slot=slot):
            cast_epilogue(1 - slot)
            matmul(slot)

    @pl.when(s == last)
    def _():
        cast_epilogue((last - 1) % 2)


def _aproj(h, w_g, tabs_g, dil):
    last = BATCH * A_TILES

    def cur(s):
        return jnp.minimum(s, last - 1)

    def prev(s):
        return jnp.maximum(s, 1) - 1

    return pl.pallas_call(
        functools.partial(_aproj_kernel, dil),
        grid=(last + 1,),
        in_specs=[
            pl.BlockSpec((SEQ, D_MODEL), lambda s: (cur(s) // A_TILES, 0)),
            pl.BlockSpec((1, A_OUT_WIDTH, D_MODEL), lambda s: (cur(s) % A_TILES, 0, 0)),
            pl.BlockSpec((3, SEQ, LANES), lambda s: (0, 0, 0)),
        ],
        out_specs=pl.BlockSpec((1, 1, SEQ, A_OUT_WIDTH),
                               lambda s: (prev(s) // A_TILES, prev(s) % A_TILES, 0, 0)),
        out_shape=jax.ShapeDtypeStruct((BATCH, A_TILES, SEQ, A_OUT_WIDTH), BF16),
        scratch_shapes=[pltpu.VMEM((A_HEADS, SEQ, A_HEAD_DIM), F32), pltpu.VMEM((A_HEADS, SEQ, A_HEAD_DIM), F32)],
        compiler_params=pltpu.CompilerParams(dimension_semantics=("arbitrary",),
                                             vmem_limit_bytes=VMEM_LIMIT),
        name="aproj_d%d" % dil,
    )(h, w_g, tabs_g)


def _lane_scan(x, op, fill):
    lane = lax.broadcasted_iota(jnp.int32, x.shape, 1)
    sh = 1
    while sh < x.shape[1]:
        shifted = jnp.where(lane >= sh, pltpu.roll(x, sh, axis=1), fill)
        x = op(x, shifted)
        sh *= 2
    return x


def _gates_kernel(ig_ref, fg_ref, u_ref, g_ref, mt_ref, mp_ref, w_ref, dec_ref):
    L = M_CHUNK
    ig = ig_ref[...]
    fg = fg_ref[...]
    logf = -(jnp.maximum(-fg, 0.0) + jnp.log1p(jnp.exp(-jnp.abs(fg))))
    b = _lane_scan(logf, jnp.add, 0.0)
    u = ig - b
    cm = _lane_scan(u, jnp.maximum, NEG_INF)
    lane = lax.broadcasted_iota(jnp.int32, ig.shape, 1)
    btot = jnp.broadcast_to(jnp.sum(jnp.where(lane == L - 1, b, 0.0), axis=1, keepdims=True), ig.shape)
    umax = jnp.broadcast_to(jnp.max(cm, axis=1, keepdims=True), ig.shape)
    chunk = lax.broadcasted_iota(jnp.int32, ig.shape, 0) % N_CHUNKS
    m = jnp.zeros(ig.shape, F32)
    for c in range(N_CHUNKS - 1):
        nxt = pltpu.roll(btot + jnp.maximum(m, umax), 1, axis=0)
        m = jnp.where(chunk == c + 1, nxt, m)
    g = jnp.maximum(m, cm)
    glast = jnp.max(g, axis=1, keepdims=True)
    u_ref[...] = u * LOG2E
    g_ref[...] = g * LOG2E
    mt_ref[...] = (b + g) * LOG2E
    mp_ref[...] = m * LOG2E
    w_ref[...] = jnp.exp(u - glast)
    dec_ref[...] = jnp.broadcast_to(jnp.exp(m - glast), ig.shape)


def _gates(gt):
    rows = BATCH * M_HEADS * N_CHUNKS
    ig = gt[:, :M_HEADS].reshape(rows, M_CHUNK)
    fg = gt[:, M_HEADS:].reshape(rows, M_CHUNK)
    spec = pl.BlockSpec((rows, M_CHUNK), lambda: (0, 0))
    return pl.pallas_call(
        _gates_kernel,
        in_specs=[spec, spec],
        out_specs=[spec] * 6,
        out_shape=[jax.ShapeDtypeStruct((rows, M_CHUNK), F32)] * 6,
        name="gates",
    )(ig, fg)


def _mlstm_kernel(q_ref, kt_ref, v_ref, o_ref, z_ref, u_ref, g_ref, mt_ref, mp_ref, w_ref, dec_ref,
                  out_ref):
    L = M_CHUNK
    ti = lax.broadcasted_iota(jnp.int32, (L, L), 0)
    si = lax.broadcasted_iota(jnp.int32, (L, L), 1)
    causal = ti >= si
    ones = jnp.ones((L, LANES), BF16)

    state = jnp.zeros((M_HEAD_DIM, M_HEAD_DIM + LANES), F32)
    for c in range(N_CHUNKS):
        rs = slice(c * L, (c + 1) * L)
        cs = slice(c, c + 1)
        qc = q_ref[rs, :]
        v1 = jnp.concatenate([v_ref[rs, :], ones], axis=1)
        ktc = kt_ref[0, 0, c]
        u_r = u_ref[cs, :]
        gcol = jnp.broadcast_to(g_ref[cs, :], (LANES, L)).T
        mcol = jnp.broadcast_to(mt_ref[cs, :], (LANES, L)).T
        dmat = jnp.where(causal, jnp.exp2(u_r - jnp.concatenate([gcol, gcol], axis=1)), 0.0)

        s = jnp.dot(qc, ktc, preferred_element_type=F32) * dmat
        tot = jnp.dot(s.astype(BF16), v1, preferred_element_type=F32)
        if c > 0:
            inter = jnp.exp2(mp_ref[cs, :LANES] - gcol)
            tot = tot + jnp.concatenate([inter, inter, inter], axis=1) \
                * jnp.dot(qc, state.astype(BF16), preferred_element_type=F32)
        den = tot[:, M_HEAD_DIM:]
        rinv = 1.0 / jnp.maximum(jnp.abs(den), jnp.exp2(-mcol))
        hval = tot[:, :M_HEAD_DIM] * jnp.concatenate([rinv, rinv], axis=1)

        hg = hval * o_ref[rs, :].astype(F32)
        ms = jnp.mean(hg * hg, axis=1, keepdims=True)
        hn = hg * lax.rsqrt(ms + NORM_EPS)
        out_ref[rs, :] = (hn * z_ref[rs, :].astype(F32)).astype(BF16)

        if c + 1 < N_CHUNKS:
            ktw = (ktc.astype(F32) * w_ref[cs, :]).astype(BF16)
            dec = dec_ref[cs, :]
            dec = jnp.concatenate([dec, dec[:, :LANES]], axis=1)
            state = dec * state + jnp.dot(ktw, v1, preferred_element_type=F32)


def _mlstm(mall, kt, gate_stats):
    blk = (SEQ, M_HEAD_DIM)

    def seg(k):
        return pl.BlockSpec(blk, lambda b, h, k=k: (b, k * M_HEADS + h))

    stat = pl.BlockSpec((N_CHUNKS, M_CHUNK), lambda b, h: (b * M_HEADS + h, 0))
    return pl.pallas_call(
        _mlstm_kernel,
        grid=(BATCH, M_HEADS),
        in_specs=[
            seg(0),
            pl.BlockSpec((1, 1, N_CHUNKS, M_HEAD_DIM, M_CHUNK), lambda b, h: (b, h, 0, 0, 0)),
            seg(2), seg(3), seg(4),
        ] + [stat] * 6,
        out_specs=pl.BlockSpec(blk, lambda b, h: (b, h)),
        out_shape=jax.ShapeDtypeStruct((BATCH * SEQ, M_WIDTH), BF16),
        compiler_params=pltpu.CompilerParams(dimension_semantics=("parallel", "parallel"),
                                             vmem_limit_bytes=VMEM_LIMIT),
        name="mlstm",
    )(mall, kt, mall, mall, mall, *gate_stats)


GATHERED = (2,)
GATHER_PITCH = A_BLOCK + 8
ATTN_WAVE = 12


def _attn_block_order():
    order = []
    for g in reversed(range(1, N_GROUPS)):
        nblk = SEQ // A_GROUPS[g][1] // A_BLOCK
        order += [(r * nblk + n, g) for n in range(nblk) for r in range(A_GROUPS[g][1])]
    order += [(i, 0) for i in range(SEQ // A_BLOCK)]
    return order


def _attn_kernel(q0, k0, v0, q1, k1, v1, q2, k2, v2, out_ref,
                 a0, a1, a2, m0, m1, m2, l0, l1, l2):
    BLK = A_BLOCK
    ti = lax.broadcasted_iota(jnp.int32, (BLK, BLK), 0)
    ji = lax.broadcasted_iota(jnp.int32, (BLK, BLK), 1)
    cur_ok = ji <= ti
    ti2 = lax.broadcasted_iota(jnp.int32, (BLK, 2 * BLK), 0)
    krel = lax.broadcasted_iota(jnp.int32, (BLK, 2 * BLK), 1) - BLK
    band_ok = jnp.logical_and(krel <= ti2, krel >= ti2 - BLK)

    qs = (q0, q1, q2)
    ks = (k0, k1, k2)
    vs = (v0, v1, v2)
    accs = (a0, a1, a2)
    ms = (m0, m1, m2)
    ls = (l0, l1, l2)

    RB = 256

    def merge(rb):
        sl = slice(rb * RB, (rb + 1) * RB)

        def rows(refs, g):
            if g in GATHERED:
                dil = A_GROUPS[g][1]
                j0 = rb * RB // dil
                return jnp.concatenate([refs[g][pl.ds(j0 + jj, dil, stride=GATHER_PITCH), :]
                                        for jj in range(RB // dil)], axis=0)
            return refs[g][sl, :]

        mg = [rows(ms, g) for g in range(N_GROUPS)]
        mmax = jnp.maximum(jnp.maximum(mg[0], mg[1]), mg[2])
        num = None
        den = None
        for g in range(N_GROUPS):
            e = jnp.exp2(mg[g] - mmax)
            num = e * rows(accs, g) if num is None else num + e * rows(accs, g)
            den = e * rows(ls, g) if den is None else den + e * rows(ls, g)
        out_ref[sl, :] = (num * (1.0 / den)).astype(BF16)

    blocks = _attn_block_order()
    done_rows = [set() for _ in range(N_GROUPS)]
    merged = 0
    for w0 in range(0, len(blocks), ATTN_WAVE):
        wave = []
        for i, g in blocks[w0:w0 + ATTN_WAVE]:
            dil = A_GROUPS[g][1]
            r, n = divmod(i, SEQ // dil // BLK)
            rows = slice(i * BLK, (i + 1) * BLK)
            keys = slice((i - 1) * BLK, (i + 1) * BLK) if n > 0 else rows
            valid = band_ok if n > 0 else cur_ok
            s = lax.dot_general(qs[g][0, 0, rows, :], ks[g][0, 0, keys, :], (((1,), (1,)), ((), ())),
                                preferred_element_type=F32)
            s = jnp.where(valid, s, NEG_INF)
            m = jnp.max(s, axis=1, keepdims=True)
            tok0 = n * BLK * dil + r
            if g in GATHERED:
                idx = pl.ds(r * GATHER_PITCH, BLK)
            else:
                idx = pl.ds(tok0, BLK, stride=dil) if dil > 1 else pl.ds(tok0, BLK)
            wave.append((g, keys, valid, idx, s, m))
        for g, keys, valid, idx, s, m in wave:
            p = jnp.where(valid, jnp.exp2(s - m), 0.0)
            nkeys = keys.stop - keys.start
            v1 = jnp.concatenate([vs[g][0, 0, keys, :], jnp.ones((nkeys, LANES), BF16)], axis=1)
            acc = jnp.dot(p.astype(BF16), v1, preferred_element_type=F32)
            accs[g][idx, :] = acc[:, :LANES]
            ms[g][idx, :] = jnp.broadcast_to(m, (BLK, LANES))
            ls[g][idx, :] = acc[:, LANES:]
        for i, g in blocks[w0:w0 + ATTN_WAVE]:
            dil = A_GROUPS[g][1]
            r, n = divmod(i, SEQ // dil // BLK)
            span = BLK * dil
            if r == dil - 1:
                done_rows[g].update(range(n * span // RB, (n + 1) * span // RB))
        while merged < SEQ // RB and all(merged in d for d in done_rows):
            merge(merged)
            merged += 1


def _attn(qkv):
    in_specs = []
    args = []
    for g in range(N_GROUPS):
        for t in range(3):
            in_specs.append(pl.BlockSpec((1, 1, SEQ, A_HEAD_DIM), lambda b, h, t=t: (b, t, 0, h)))
            args.append(qkv[g])
    return pl.pallas_call(
        _attn_kernel,
        grid=(BATCH, A_HEADS),
        in_specs=in_specs,
        out_specs=pl.BlockSpec((SEQ, A_HEAD_DIM), lambda b, h: (b, h)),
        out_shape=jax.ShapeDtypeStruct((BATCH * SEQ, A_OUT_WIDTH), BF16),
        scratch_shapes=[pltpu.VMEM((A_GROUPS[g][1] * GATHER_PITCH if g in GATHERED else SEQ, LANES), F32)
                        for _ in range(3) for g in range(N_GROUPS)],
        compiler_params=pltpu.CompilerParams(dimension_semantics=("parallel", "parallel"),
                                             vmem_limit_bytes=VMEM_LIMIT),
        name="attn",
    )(*args)


O_TM = 512


def _out_kernel(x_ref, prew_ref, hm_ref, att_ref, wg_ref, wpm_ref, wpa_ref, wout_ref, postw_ref,
                y_ref):
    x = x_ref[...]
    ms = jnp.mean(x * x, axis=-1, keepdims=True)
    h = (x * lax.rsqrt(ms + NORM_EPS) * prew_ref[...]).astype(BF16)
    gates = _dot_nt(h, wg_ref[...])
    za = _silu(gates[:, :A_OUT_WIDTH])
    sgm = _sigmoid(gates[:, A_OUT_WIDTH:A_OUT_WIDTH + D_MODEL])
    sga = _sigmoid(gates[:, A_OUT_WIDTH + D_MODEL:])
    a = (att_ref[...].astype(F32) * za).astype(BF16)
    bm = jnp.dot(hm_ref[...], wpm_ref[...], preferred_element_type=F32)
    ba = jnp.dot(a, wpa_ref[...], preferred_element_type=F32)
    merged = (sgm * bm + sga * ba).astype(BF16)
    y = jnp.dot(merged, wout_ref[...], preferred_element_type=F32)
    ms2 = jnp.mean(y * y, axis=-1, keepdims=True)
    y_ref[...] = x + y * lax.rsqrt(ms2 + NORM_EPS) * postw_ref[...]


def _out(x2d, pre_w, hm, att, wg4, wpm, wpa, wout, post_w):
    n = x2d.shape[0]
    const = lambda i: (0, 0)
    return pl.pallas_call(
        _out_kernel,
        grid=(n // O_TM,),
        in_specs=[
            pl.BlockSpec((O_TM, D_MODEL), lambda i: (i, 0)),
            pl.BlockSpec((1, D_MODEL), const),
            pl.BlockSpec((O_TM, M_WIDTH), lambda i: (i, 0)),
            pl.BlockSpec((O_TM, A_OUT_WIDTH), lambda i: (i, 0)),
            pl.BlockSpec((A_OUT_WIDTH + 2 * D_MODEL, D_MODEL), const),
            pl.BlockSpec((M_WIDTH, D_MODEL), const),
            pl.BlockSpec((A_OUT_WIDTH, D_MODEL), const),
            pl.BlockSpec((D_MODEL, D_MODEL), const),
            pl.BlockSpec((1, D_MODEL), const),
        ],
        out_specs=pl.BlockSpec((O_TM, D_MODEL), lambda i: (i, 0)),
        out_shape=jax.ShapeDtypeStruct((n, D_MODEL), F32),
        compiler_params=pltpu.CompilerParams(dimension_semantics=("parallel",),
                                             vmem_limit_bytes=VMEM_LIMIT),
        name="outproj",
    )(x2d, pre_w.reshape(1, D_MODEL), hm, att, wg4, wpm, wpa, wout, post_w.reshape(1, D_MODEL))


def _rope_tables():
    pos = np.arange(SEQ, dtype=np.float64)
    inv_freq = ROPE_THETA ** (-np.arange(0, ROPE_DIM, 2, dtype=np.float64) / ROPE_DIM)
    ang = pos[:, None] * inv_freq[None, :]
    cos, sin = np.cos(ang), np.sin(ang)
    half = ROPE_DIM // 2
    pad = A_HEAD_DIM - ROPE_DIM
    cos_t = np.concatenate([cos, cos, np.ones((SEQ, pad))], axis=1)
    sin_up = np.concatenate([np.zeros((SEQ, half)), sin, np.zeros((SEQ, pad))], axis=1)
    sin_dn = np.concatenate([-sin, np.zeros((SEQ, half + pad))], axis=1)
    tab = np.stack([cos_t, sin_up, sin_dn]).astype(np.float32)
    out = []
    for _, dil in A_GROUPS:
        ls = SEQ // dil
        out.append(tab.reshape(3, ls, dil, A_HEAD_DIM).transpose(0, 2, 1, 3).reshape(3, SEQ, A_HEAD_DIM))
    return jnp.asarray(np.stack(out))


def _layer(x, pre_w, w_in, b_if, conv_w, conv_b, m_norm_w, w_pm, w_pa, w_out, post_w):
    x2d = x.reshape(BATCH * SEQ, D_MODEL)
    wt = w_in.T
    wt_m = wt[:OFF_GATES].astype(BF16)
    wg = jnp.pad(wt[OFF_GATES:OFF_QA], ((0, LANES - 2 * M_HEADS), (0, 0))).astype(BF16)
    w_att = [jnp.stack([wt[off + g * A_OUT_WIDTH: off + (g + 1) * A_OUT_WIDTH]
                        for off in (OFF_QA, OFF_KA, OFF_VA)]).astype(BF16) for g in range(N_GROUPS)]
    wg4 = wt[OFF_ZA:].astype(BF16)
    tabs = _rope_tables()

    mall, kt, gt, h = _mproj(x2d, pre_w, wt_m, wg, b_if.reshape(2 * M_HEADS, 1), conv_w,
                             conv_b.reshape(1, 2 * M_WIDTH))
    qkv = [_aproj(h, w_att[g], tabs[g], A_GROUPS[g][1]) for g in range(N_GROUPS)]
    hm = _mlstm(mall, kt, _gates(gt))
    att = _attn(qkv)
    w_pm_n = (m_norm_w[:, None] * w_pm).astype(BF16)
    y = _out(x2d, pre_w, hm, att, wg4, w_pm_n, w_pa.astype(BF16), w_out.astype(BF16), post_w)
    return y.reshape(BATCH, SEQ, D_MODEL)


@jax.jit
def kernel(x, pre_norm_w, w_in, b_if, conv_w, conv_b, m_norm_w, w_proj_m, w_proj_a, w_out, post_norm_w):
    for layer in range(pre_norm_w.shape[0]):
        x = _layer(x, pre_norm_w[layer], w_in[layer], b_if[layer], conv_w[layer], conv_b[layer],
                   m_norm_w[layer], w_proj_m[layer], w_proj_a[layer], w_out[layer], post_norm_w[layer])
    return x
```

```python
import functools

import jax
import jax.numpy as jnp
import numpy as np
from jax import lax
from jax.experimental import pallas as pl
from jax.experimental.pallas import tpu as pltpu

D_MODEL = 1024
BATCH = 8
SEQ = 2048
M_WIDTH = 1024
M_HEADS = 4
M_HEAD_DIM = 256
CONV_WIDTH = 4
A_GROUPS = ((128, 1), (512, 4), (2048, 16))
N_GROUPS = 3
A_HEADS = 4
A_HEAD_DIM = 128
A_OUT_WIDTH = 512
A_QKV_WIDTH = 1536
A_BLOCK = 128
ROPE_DIM = 32
ROPE_THETA = 500000.0
NORM_EPS = 1e-6
NEG_INF = -1e30
LOG2E = 1.4426950408889634

OFF_GATES = 5 * M_WIDTH
OFF_QA = OFF_GATES + 2 * M_HEADS
OFF_KA = OFF_QA + A_QKV_WIDTH
OFF_VA = OFF_KA + A_QKV_WIDTH
OFF_ZA = OFF_VA + A_QKV_WIDTH
OFF_GM = OFF_ZA + A_OUT_WIDTH
IN_WIDTH = OFF_GM + 2 * D_MODEL

LANES = 128
M_CHUNK = 256
N_CHUNKS = SEQ // M_CHUNK
VMEM_LIMIT = 48 * 1024 * 1024

F32 = jnp.float32
BF16 = jnp.bfloat16


def _dot_nt(a, bt):
    return lax.dot_general(a, bt, (((1,), (1,)), ((), ())), preferred_element_type=F32)


def _sigmoid(x):
    return 1.0 / (1.0 + jnp.exp(-x))


def _silu(x):
    return x * _sigmoid(x)


M_TN = 512


SUBLANES = 8


def _shift_rows(x3, s, sub):
    rot = pltpu.roll(x3, s, axis=1)
    prev = jnp.concatenate([jnp.zeros_like(rot[:1]), rot[:-1]], axis=0)
    return jnp.where(sub < s, prev, rot)


def _causal_conv(acc, cw, cb):
    rows, cols = acc.shape
    x3 = acc.reshape(rows // SUBLANES, SUBLANES, cols)
    sub = lax.broadcasted_iota(jnp.int32, (1, SUBLANES, cols), 1)
    d1 = _shift_rows(x3, 1, sub)
    near = x3 * cw[3] + d1 * cw[2] + cb[0]
    far = x3 * cw[1] + d1 * cw[0]
    return (near + _shift_rows(far, 2, sub)).reshape(rows, cols)


M_TILES = 5 * M_WIDTH // M_TN


def _mproj_kernel(x_ref, prew_ref, w_ref, wg_ref, bif_ref, cw_ref, cb_ref,
                  mall_ref, kt_ref, gt_ref, h_ref):
    j = pl.program_id(1)

    def matmul():
        return _dot_nt(h_ref[...], w_ref[...])

    @pl.when(j == 0)
    def _():
        x = x_ref[...]
        ms = jnp.mean(x * x, axis=-1, keepdims=True)
        h_ref[...] = (x * lax.rsqrt(ms + NORM_EPS) * prew_ref[...]).astype(BF16)
        g = _dot_nt(h_ref[...], wg_ref[...])
        gt = g.T[:2 * M_HEADS, :] + bif_ref[...]
        for c in range(N_CHUNKS):
            gt_ref[0, :, c, :] = gt[:, c * M_CHUNK:(c + 1) * M_CHUNK]

    @pl.when(j < 2)
    def _():
        y = _causal_conv(matmul(), cw_ref[...], cb_ref[...])
        mall_ref[...] = _silu(y).astype(BF16)

    @pl.when(jnp.logical_and(j >= 2, j < 4))
    def _():
        y = _causal_conv(matmul(), cw_ref[...], cb_ref[...])
        y = _silu(y) * (M_HEAD_DIM ** -0.5)
        mall_ref[...] = y.astype(BF16)
        for hh in range(M_TN // M_HEAD_DIM):
            for c in range(N_CHUNKS):
                blk = y[c * M_CHUNK:(c + 1) * M_CHUNK, hh * M_HEAD_DIM:(hh + 1) * M_HEAD_DIM]
                kt_ref[0, hh, c] = blk.T.astype(BF16)

    @pl.when(jnp.logical_and(j >= 4, j < 6))
    def _():
        mall_ref[...] = matmul().astype(BF16)

    @pl.when(jnp.logical_and(j >= 6, j < 8))
    def _():
        mall_ref[...] = _sigmoid(matmul()).astype(BF16)

    @pl.when(j >= 8)
    def _():
        mall_ref[...] = _silu(matmul()).astype(BF16)


def _mproj(x2d, pre_w, wt_m, wg, bif, conv_w, conv_b):
    return pl.pallas_call(
        _mproj_kernel,
        grid=(BATCH, M_TILES),
        in_specs=[
            pl.BlockSpec((SEQ, D_MODEL), lambda i, j: (i, 0)),
            pl.BlockSpec((1, D_MODEL), lambda i, j: (0, 0)),
            pl.BlockSpec((M_TN, D_MODEL), lambda i, j: (j, 0)),
            pl.BlockSpec((LANES, D_MODEL), lambda i, j: (0, 0)),
            pl.BlockSpec((2 * M_HEADS, 1), lambda i, j: (0, 0)),
            pl.BlockSpec((CONV_WIDTH, M_TN), lambda i, j: (0, jnp.minimum(j, 3))),
            pl.BlockSpec((1, M_TN), lambda i, j: (0, jnp.minimum(j, 3))),
        ],
        out_specs=[
            pl.BlockSpec((SEQ, M_TN), lambda i, j: (i, j)),
            pl.BlockSpec((1, M_TN // M_HEAD_DIM, N_CHUNKS, M_HEAD_DIM, M_CHUNK),
                         lambda i, j: (i, jnp.clip(j - 2, 0, 1), 0, 0, 0)),
            pl.BlockSpec((1, 2 * M_HEADS, N_CHUNKS, M_CHUNK), lambda i, j: (i, 0, 0, 0)),
            pl.BlockSpec((SEQ, D_MODEL), lambda i, j: (i, 0)),
        ],
        out_shape=[
            jax.ShapeDtypeStruct((BATCH * SEQ, 5 * M_WIDTH), BF16),
            jax.ShapeDtypeStruct((BATCH, M_HEADS, N_CHUNKS, M_HEAD_DIM, M_CHUNK), BF16),
            jax.ShapeDtypeStruct((BATCH, 2 * M_HEADS, N_CHUNKS, M_CHUNK), F32),
            jax.ShapeDtypeStruct((BATCH * SEQ, D_MODEL), BF16),
        ],
        compiler_params=pltpu.CompilerParams(dimension_semantics=("parallel", "arbitrary"),
                                             vmem_limit_bytes=VMEM_LIMIT),
        name="mproj",
    )(x2d, pre_w.reshape(1, D_MODEL), wt_m, wg, bif, conv_w, conv_b)


A_TILES = 3
EASY_STRIDE = 4


def _aproj_kernel(dil, h_ref, w_ref, tab_ref, o_ref, acc0_ref, acc1_ref, tmp_ref):
    s = pl.program_id(0)
    last = BATCH * A_TILES
    pt = (jnp.maximum(s, 1) - 1) % A_TILES

    accs = (acc0_ref, acc1_ref)

    def matmul(slot):
        res = _dot_nt(h_ref[...], w_ref[0])
        for hd in range(A_HEADS):
            accs[slot][hd] = res[:, hd * A_HEAD_DIM:(hd + 1) * A_HEAD_DIM]

    ls = SEQ // dil

    two_pass = dil == EASY_STRIDE * EASY_STRIDE

    def stage(slot, hd):
        if two_pass:
            n4 = SEQ // EASY_STRIDE
            for r4 in range(EASY_STRIDE):
                tmp_ref[hd, r4 * n4:(r4 + 1) * n4, :] = accs[slot][hd, pl.ds(r4, n4, stride=EASY_STRIDE), :]

    def residue_rows(slot, hd, r):
        if dil == 1:
            return accs[slot][hd]
        if two_pass:
            q, r4 = divmod(r, EASY_STRIDE)
            return tmp_ref[hd, pl.ds(r4 * (SEQ // EASY_STRIDE) + q, ls, stride=EASY_STRIDE), :]
        return accs[slot][hd, pl.ds(r, ls, stride=dil), :]

    def rope_epilogue(slot):
        scale = jnp.where(pt == 0, A_HEAD_DIM ** -0.5 * LOG2E, 1.0).astype(F32)
        half = ROPE_DIM // 2
        for hd in range(A_HEADS):
            stage(slot, hd)
            for r in range(dil):
                rows = slice(r * ls, (r + 1) * ls)
                xh = residue_rows(slot, hd, r)
                y = (xh * tab_ref[0, rows, :] + pltpu.roll(xh, half, axis=1) * tab_ref[1, rows, :]
                     + pltpu.roll(xh, A_HEAD_DIM - half, axis=1) * tab_ref[2, rows, :])
                o_ref[0, 0, rows, hd * A_HEAD_DIM:(hd + 1) * A_HEAD_DIM] = (y * scale).astype(BF16)

    def cast_epilogue(slot):
        for hd in range(A_HEADS):
            stage(slot, hd)
            for r in range(dil):
                rows = slice(r * ls, (r + 1) * ls)
                o_ref[0, 0, rows, hd * A_HEAD_DIM:(hd + 1) * A_HEAD_DIM] = \
                    residue_rows(slot, hd, r).astype(BF16)

    mid = jnp.logical_and(s > 0, s < last)

    @pl.when(s == 0)
    def _():
        matmul(0)

    for slot in range(2):
        mine = jnp.logical_and(mid, s % 2 == slot)

        @pl.when(jnp.logical_and(mine, pt < 2))
        def _(slot=slot):
            rope_epilogue(1 - slot)
            matmul(slot)

        @pl.when(jnp.logical_and(mine, pt == 2))
        def _(slot=slot):
            cast_epilogue(1 - slot)
            matmul(slot)

    @pl.when(s == last)
    def _():
        cast_epilogue((last - 1) % 2)


def _aproj(h, w_g, tabs_g, dil):
    last = BATCH * A_TILES

    def cur(s):
        return jnp.minimum(s, last - 1)

    def prev(s):
        return jnp.maximum(s, 1) - 1

    return pl.pallas_call(
        functools.partial(_aproj_kernel, dil),
        grid=(last + 1,),
        in_specs=[
            pl.BlockSpec((SEQ, D_MODEL), lambda s: (cur(s) // A_TILES, 0)),
            pl.BlockSpec((1, A_OUT_WIDTH, D_MODEL), lambda s: (cur(s) % A_TILES, 0, 0)),
            pl.BlockSpec((3, SEQ, LANES), lambda s: (0, 0, 0)),
        ],
        out_specs=pl.BlockSpec((1, 1, SEQ, A_OUT_WIDTH),
                               lambda s: (prev(s) // A_TILES, prev(s) % A_TILES, 0, 0)),
        out_shape=jax.ShapeDtypeStruct((BATCH, A_TILES, SEQ, A_OUT_WIDTH), BF16),
        scratch_shapes=[pltpu.VMEM((A_HEADS, SEQ, A_HEAD_DIM), F32) for _ in range(3)],
        compiler_params=pltpu.CompilerParams(dimension_semantics=("arbitrary",),
                                             vmem_limit_bytes=VMEM_LIMIT),
        name="aproj_d%d" % dil,
    )(h, w_g, tabs_g)


def _lane_scan(x, op, fill):
    lane = lax.broadcasted_iota(jnp.int32, x.shape, 1)
    sh = 1
    while sh < x.shape[1]:
        shifted = jnp.where(lane >= sh, pltpu.roll(x, sh, axis=1), fill)
        x = op(x, shifted)
        sh *= 2
    return x


def _gates_kernel(ig_ref, fg_ref, u_ref, g_ref, mt_ref, mp_ref, w_ref, dec_ref):
    L = M_CHUNK
    ig = ig_ref[...]
    fg = fg_ref[...]
    logf = -(jnp.maximum(-fg, 0.0) + jnp.log1p(jnp.exp(-jnp.abs(fg))))
    b = _lane_scan(logf, jnp.add, 0.0)
    u = ig - b
    cm = _lane_scan(u, jnp.maximum, NEG_INF)
    lane = lax.broadcasted_iota(jnp.int32, ig.shape, 1)
    btot = jnp.broadcast_to(jnp.sum(jnp.where(lane == L - 1, b, 0.0), axis=1, keepdims=True), ig.shape)
    umax = jnp.broadcast_to(jnp.max(cm, axis=1, keepdims=True), ig.shape)
    chunk = lax.broadcasted_iota(jnp.int32, ig.shape, 0) % N_CHUNKS
    m = jnp.zeros(ig.shape, F32)
    for c in range(N_CHUNKS - 1):
        nxt = pltpu.roll(btot + jnp.maximum(m, umax), 1, axis=0)
        m = jnp.where(chunk == c + 1, nxt, m)
    g = jnp.maximum(m, cm)
    glast = jnp.max(g, axis=1, keepdims=True)
    u_ref[...] = u * LOG2E
    g_ref[...] = g * LOG2E
    mt_ref[...] = (b + g) * LOG2E
    mp_ref[...] = m * LOG2E
    w_ref[...] = jnp.exp(u - glast)
    dec_ref[...] = jnp.broadcast_to(jnp.exp(m - glast), ig.shape)


def _gates(gt):
    rows = BATCH * M_HEADS * N_CHUNKS
    ig = gt[:, :M_HEADS].reshape(rows, M_CHUNK)
    fg = gt[:, M_HEADS:].reshape(rows, M_CHUNK)
    spec = pl.BlockSpec((rows, M_CHUNK), lambda: (0, 0))
    return pl.pallas_call(
        _gates_kernel,
        in_specs=[spec, spec],
        out_specs=[spec] * 6,
        out_shape=[jax.ShapeDtypeStruct((rows, M_CHUNK), F32)] * 6,
        name="gates",
    )(ig, fg)


def _mlstm_kernel(q_ref, kt_ref, v_ref, o_ref, z_ref, u_ref, g_ref, mt_ref, mp_ref, w_ref, dec_ref,
                  out_ref):
    L = M_CHUNK
    ti = lax.broadcasted_iota(jnp.int32, (L, L), 0)
    si = lax.broadcasted_iota(jnp.int32, (L, L), 1)
    causal = ti >= si
    ones = jnp.ones((L, LANES), BF16)

    state = jnp.zeros((M_HEAD_DIM, M_HEAD_DIM + LANES), F32)
    for c in range(N_CHUNKS):
        rs = slice(c * L, (c + 1) * L)
        cs = slice(c, c + 1)
        qc = q_ref[rs, :]
        v1 = jnp.concatenate([v_ref[rs, :], ones], axis=1)
        ktc = kt_ref[0, 0, c]
        u_r = u_ref[cs, :]
        gcol = jnp.broadcast_to(g_ref[cs, :], (LANES, L)).T
        mcol = jnp.broadcast_to(mt_ref[cs, :], (LANES, L)).T
        dmat = jnp.where(causal, jnp.exp2(u_r - jnp.concatenate([gcol, gcol], axis=1)), 0.0)

        s = jnp.dot(qc, ktc, preferred_element_type=F32) * dmat
        tot = jnp.dot(s.astype(BF16), v1, preferred_element_type=F32)
        if c > 0:
            inter = jnp.exp2(mp_ref[cs, :LANES] - gcol)
            tot = tot + jnp.concatenate([inter, inter, inter], axis=1) \
                * jnp.dot(qc, state.astype(BF16), preferred_element_type=F32)
        den = tot[:, M_HEAD_DIM:]
        rinv = 1.0 / jnp.maximum(jnp.abs(den), jnp.exp2(-mcol))
        hval = tot[:, :M_HEAD_DIM] * jnp.concatenate([rinv, rinv], axis=1)

        hg = hval * o_ref[rs, :].astype(F32)
        ms = jnp.mean(hg * hg, axis=1, keepdims=True)
        hn = hg * lax.rsqrt(ms + NORM_EPS)
        out_ref[rs, :] = (hn * z_ref[rs, :].astype(F32)).astype(BF16)

        if c + 1 < N_CHUNKS:
            ktw = (ktc.astype(F32) * w_ref[cs, :]).astype(BF16)
            dec = dec_ref[cs, :]
            dec = jnp.concatenate([dec, dec[:, :LANES]], axis=1)
            state = dec * state + jnp.dot(ktw, v1, preferred_element_type=F32)


def _mlstm(mall, kt, gate_stats):
    blk = (SEQ, M_HEAD_DIM)

    def seg(k):
        return pl.BlockSpec(blk, lambda b, h, k=k: (b, k * M_HEADS + h))

    stat = pl.BlockSpec((N_CHUNKS, M_CHUNK), lambda b, h: (b * M_HEADS + h, 0))
    return pl.pallas_call(
        _mlstm_kernel,
        grid=(BATCH, M_HEADS),
        in_specs=[
            seg(0),
            pl.BlockSpec((1, 1, N_CHUNKS, M_HEAD_DIM, M_CHUNK), lambda b, h: (b, h, 0, 0, 0)),
            seg(2), seg(3), seg(4),
        ] + [stat] * 6,
        out_specs=pl.BlockSpec(blk, lambda b, h: (b, h)),
        out_shape=jax.ShapeDtypeStruct((BATCH * SEQ, M_WIDTH), BF16),
        compiler_params=pltpu.CompilerParams(dimension_semantics=("parallel", "parallel"),
                                             vmem_limit_bytes=VMEM_LIMIT),
        name="mlstm",
    )(mall, kt, mall, mall, mall, *gate_stats)


GATHERED = (2,)
GATHER_PITCH = A_BLOCK + 8
ATTN_WAVE = 12


def _attn_block_order():
    order = []
    for g in reversed(range(1, N_GROUPS)):
        nblk = SEQ // A_GROUPS[g][1] // A_BLOCK
        order += [(r * nblk + n, g) for n in range(nblk) for r in range(A_GROUPS[g][1])]
    order += [(i, 0) for i in range(SEQ // A_BLOCK)]
    return order


def _attn_kernel(q0, k0, v0, q1, k1, v1, q2, k2, v2, out_ref,
                 a0, a1, a2, m0, m1, m2, l0, l1, l2):
    BLK = A_BLOCK
    ti = lax.broadcasted_iota(jnp.int32, (BLK, BLK), 0)
    ji = lax.broadcasted_iota(jnp.int32, (BLK, BLK), 1)
    cur_ok = ji <= ti
    ti2 = lax.broadcasted_iota(jnp.int32, (BLK, 2 * BLK), 0)
    krel = lax.broadcasted_iota(jnp.int32, (BLK, 2 * BLK), 1) - BLK
    band_ok = jnp.logical_and(krel <= ti2, krel >= ti2 - BLK)

    qs = (q0, q1, q2)
    ks = (k0, k1, k2)
    vs = (v0, v1, v2)
    accs = (a0, a1, a2)
    ms = (m0, m1, m2)
    ls = (l0, l1, l2)

    RB = 256

    def merge(rb):
        sl = slice(rb * RB, (rb + 1) * RB)

        def rows(refs, g):
            if g in GATHERED:
                dil = A_GROUPS[g][1]
                j0 = rb * RB // dil
                return jnp.concatenate([refs[g][pl.ds(j0 + jj, dil, stride=GATHER_PITCH), :]
                                        for jj in range(RB // dil)], axis=0)
            return refs[g][sl, :]

        mg = [rows(ms, g) for g in range(N_GROUPS)]
        mmax = jnp.maximum(jnp.maximum(mg[0], mg[1]), mg[2])
        num = None
        den = None
        for g in range(N_GROUPS):
            e = jnp.exp2(mg[g] - mmax)
            num = e * rows(accs, g) if num is None else num + e * rows(accs, g)
            den = e * rows(ls, g) if den is None else den + e * rows(ls, g)
        out_ref[sl, :] = (num * (1.0 / den)).astype(BF16)

    blocks = _attn_block_order()
    done_rows = [set() for _ in range(N_GROUPS)]
    merged = 0
    for w0 in range(0, len(blocks), ATTN_WAVE):
        wave = []
        for i, g in blocks[w0:w0 + ATTN_WAVE]:
            dil = A_GROUPS[g][1]
            r, n = divmod(i, SEQ // dil // BLK)
            rows = slice(i * BLK, (i + 1) * BLK)
            keys = slice((i - 1) * BLK, (i + 1) * BLK) if n > 0 else rows
            valid = band_ok if n > 0 else cur_ok
            s = lax.dot_general(qs[g][0, 0, rows, :], ks[g][0, 0, keys, :], (((1,), (1,)), ((), ())),
                                preferred_element_type=F32)
            s = jnp.where(valid, s, NEG_INF)
            m = jnp.max(s, axis=1, keepdims=True)
            tok0 = n * BLK * dil + r
            if g in GATHERED:
                idx = pl.ds(r * GATHER_PITCH, BLK)
            else:
                idx = pl.ds(tok0, BLK, stride=dil) if dil > 1 else pl.ds(tok0, BLK)
            wave.append((g, keys, valid, idx, s, m))
        for g, keys, valid, idx, s, m in wave:
            p = jnp.where(valid, jnp.exp2(s - m), 0.0)
            nkeys = keys.stop - keys.start
            v1 = jnp.concatenate([vs[g][0, 0, keys, :], jnp.ones((nkeys, LANES), BF16)], axis=1)
            acc = jnp.dot(p.astype(BF16), v1, preferred_element_type=F32)
            accs[g][idx, :] = acc[:, :LANES]
            ms[g][idx, :] = jnp.broadcast_to(m, (BLK, LANES))
            ls[g][idx, :] = acc[:, LANES:]
        for i, g in blocks[w0:w0 + ATTN_WAVE]:
            dil = A_GROUPS[g][1]
            r, n = divmod(i, SEQ // dil // BLK)
            span = BLK * dil
            if r == dil - 1:
                done_rows[g].update(range(n * span // RB, (n + 1) * span // RB))
        while merged < SEQ // RB and all(merged in d for d in done_rows):
            merge(merged)
            merged += 1


def _attn(qkv):
    in_specs = []
    args = []
    for g in range(N_GROUPS):
        for t in range(3):
            in_specs.append(pl.BlockSpec((1, 1, SEQ, A_HEAD_DIM), lambda b, h, t=t: (b, t, 0, h)))
            args.append(qkv[g])
    return pl.pallas_call(
        _attn_kernel,
        grid=(BATCH, A_HEADS),
        in_specs=in_specs,
        out_specs=pl.BlockSpec((SEQ, A_HEAD_DIM), lambda b, h: (b, h)),
        out_shape=jax.ShapeDtypeStruct((BATCH * SEQ, A_OUT_WIDTH), BF16),
        scratch_shapes=[pltpu.VMEM((A_GROUPS[g][1] * GATHER_PITCH if g in GATHERED else SEQ, LANES), F32)
                        for _ in range(3) for g in range(N_GROUPS)],
        compiler_params=pltpu.CompilerParams(dimension_semantics=("parallel", "parallel"),
                                             vmem_limit_bytes=VMEM_LIMIT),
        name="attn",
    )(*args)


O_TM = 1024


def _out_kernel(x_ref, prew_ref, hm_ref, att_ref, wg_ref, wpm_ref, wpa_ref, wout_ref, postw_ref,
                y_ref):
    x = x_ref[...]
    ms = jnp.mean(x * x, axis=-1, keepdims=True)
    h = (x * lax.rsqrt(ms + NORM_EPS) * prew_ref[...]).astype(BF16)
    gates = _dot_nt(h, wg_ref[...])
    za = _silu(gates[:, :A_OUT_WIDTH])
    sgm = _sigmoid(gates[:, A_OUT_WIDTH:A_OUT_WIDTH + D_MODEL])
    sga = _sigmoid(gates[:, A_OUT_WIDTH + D_MODEL:])
    a = (att_ref[...].astype(F32) * za).astype(BF16)
    bm = jnp.dot(hm_ref[...], wpm_ref[...], preferred_element_type=F32)
    ba = jnp.dot(a, wpa_ref[...], preferred_element_type=F32)
    merged = (sgm * bm + sga * ba).astype(BF16)
    y = jnp.dot(merged, wout_ref[...], preferred_element_type=F32)
    ms2 = jnp.mean(y * y, axis=-1, keepdims=True)
    y_ref[...] = x + y * lax.rsqrt(ms2 + NORM_EPS) * postw_ref[...]


def _out(x2d, pre_w, hm, att, wg4, wpm, wpa, wout, post_w):
    n = x2d.shape[0]
    const = lambda i: (0, 0)
    return pl.pallas_call(
        _out_kernel,
        grid=(n // O_TM,),
        in_specs=[
            pl.BlockSpec((O_TM, D_MODEL), lambda i: (i, 0)),
            pl.BlockSpec((1, D_MODEL), const),
            pl.BlockSpec((O_TM, M_WIDTH), lambda i: (i, 0)),
            pl.BlockSpec((O_TM, A_OUT_WIDTH), lambda i: (i, 0)),
            pl.BlockSpec((A_OUT_WIDTH + 2 * D_MODEL, D_MODEL), const),
            pl.BlockSpec((M_WIDTH, D_MODEL), const),
            pl.BlockSpec((A_OUT_WIDTH, D_MODEL), const),
            pl.BlockSpec((D_MODEL, D_MODEL), const),
            pl.BlockSpec((1, D_MODEL), const),
        ],
        out_specs=pl.BlockSpec((O_TM, D_MODEL), lambda i: (i, 0)),
        out_shape=jax.ShapeDtypeStruct((n, D_MODEL), F32),
        compiler_params=pltpu.CompilerParams(dimension_semantics=("parallel",),
                                             vmem_limit_bytes=VMEM_LIMIT),
        name="outproj",
    )(x2d, pre_w.reshape(1, D_MODEL), hm, att, wg4, wpm, wpa, wout, post_w.reshape(1, D_MODEL))


def _rope_tables():
    pos = np.arange(SEQ, dtype=np.float64)
    inv_freq = ROPE_THETA ** (-np.arange(0, ROPE_DIM, 2, dtype=np.float64) / ROPE_DIM)
    ang = pos[:, None] * inv_freq[None, :]
    cos, sin = np.cos(ang), np.sin(ang)
    half = ROPE_DIM // 2
    pad = A_HEAD_DIM - ROPE_DIM
    cos_t = np.concatenate([cos, cos, np.ones((SEQ, pad))], axis=1)
    sin_up = np.concatenate([np.zeros((SEQ, half)), sin, np.zeros((SEQ, pad))], axis=1)
    sin_dn = np.concatenate([-sin, np.zeros((SEQ, half + pad))], axis=1)
    tab = np.stack([cos_t, sin_up, sin_dn]).astype(np.float32)
    out = []
    for _, dil in A_GROUPS:
        ls = SEQ // dil
        out.append(tab.reshape(3, ls, dil, A_HEAD_DIM).transpose(0, 2, 1, 3).reshape(3, SEQ, A_HEAD_DIM))
    return jnp.asarray(np.stack(out))


def _layer(x, pre_w, w_in, b_if, conv_w, conv_b, m_norm_w, w_pm, w_pa, w_out, post_w):
    x2d = x.reshape(BATCH * SEQ, D_MODEL)
    wt = w_in.T
    wt_m = wt[:OFF_GATES].astype(BF16)
    wg = jnp.pad(wt[OFF_GATES:OFF_QA], ((0, LANES - 2 * M_HEADS), (0, 0))).astype(BF16)
    w_att = [jnp.stack([wt[off + g * A_OUT_WIDTH: off + (g + 1) * A_OUT_WIDTH]
                        for off in (OFF_QA, OFF_KA, OFF_VA)]).astype(BF16) for g in range(N_GROUPS)]
    wg4 = wt[OFF_ZA:].astype(BF16)
    tabs = _rope_tables()

    mall, kt, gt, h = _mproj(x2d, pre_w, wt_m, wg, b_if.reshape(2 * M_HEADS, 1), conv_w,
                             conv_b.reshape(1, 2 * M_WIDTH))
    qkv = [_aproj(h, w_att[g], tabs[g], A_GROUPS[g][1]) for g in range(N_GROUPS)]
    hm = _mlstm(mall, kt, _gates(gt))
    att = _attn(qkv)
    w_pm_n = (m_norm_w[:, None] * w_pm).astype(BF16)
    y = _out(x2d, pre_w, hm, att, wg4, w_pm_n, w_pa.astype(BF16), w_out.astype(BF16), post_w)
    return y.reshape(BATCH, SEQ, D_MODEL)


@jax.jit
def kernel(x, pre_norm_w, w_in, b_if, conv_w, conv_b, m_norm_w, w_proj_m, w_proj_a, w_out, post_norm_w):
    for layer in range(pre_norm_w.shape[0]):
        x = _layer(x, pre_norm_w[layer], w_in[layer], b_if[layer], conv_w[layer], conv_b[layer],
                   m_norm_w[layer], w_proj_m[layer], w_proj_a[layer], w_out[layer], post_norm_w[layer])
    return x
```

```python
import functools

import jax
import jax.numpy as jnp
import numpy as np
from jax import lax
from jax.experimental import pallas as pl
from jax.experimental.pallas import tpu as pltpu

D_MODEL = 1024
BATCH = 8
SEQ = 2048
M_WIDTH = 1024
M_HEADS = 4
M_HEAD_DIM = 256
CONV_WIDTH = 4
A_GROUPS = ((128, 1), (512, 4), (2048, 16))
N_GROUPS = 3
A_HEADS = 4
A_HEAD_DIM = 128
A_OUT_WIDTH = 512
A_QKV_WIDTH = 1536
A_BLOCK = 128
ROPE_DIM = 32
ROPE_THETA = 500000.0
NORM_EPS = 1e-6
NEG_INF = -1e30
LOG2E = 1.4426950408889634

OFF_GATES = 5 * M_WIDTH
OFF_QA = OFF_GATES + 2 * M_HEADS
OFF_KA = OFF_QA + A_QKV_WIDTH
OFF_VA = OFF_KA + A_QKV_WIDTH
OFF_ZA = OFF_VA + A_QKV_WIDTH
OFF_GM = OFF_ZA + A_OUT_WIDTH
IN_WIDTH = OFF_GM + 2 * D_MODEL

LANES = 128
M_CHUNK = 256
N_CHUNKS = SEQ // M_CHUNK
VMEM_LIMIT = 48 * 1024 * 1024

F32 = jnp.float32
BF16 = jnp.bfloat16


def _dot_nt(a, bt):
    return lax.dot_general(a, bt, (((1,), (1,)), ((), ())), preferred_element_type=F32)


def _sigmoid(x):
    return 1.0 / (1.0 + jnp.exp(-x))


def _silu(x):
    return x * _sigmoid(x)


M_TN = 512


SUBLANES = 8


def _shift_rows(x3, s, sub):
    rot = pltpu.roll(x3, s, axis=1)
    prev = jnp.concatenate([jnp.zeros_like(rot[:1]), rot[:-1]], axis=0)
    return jnp.where(sub < s, prev, rot)


def _causal_conv(acc, cw, cb):
    rows, cols = acc.shape
    x3 = acc.reshape(rows // SUBLANES, SUBLANES, cols)
    sub = lax.broadcasted_iota(jnp.int32, (1, SUBLANES, cols), 1)
    d1 = _shift_rows(x3, 1, sub)
    near = x3 * cw[3] + d1 * cw[2] + cb[0]
    far = x3 * cw[1] + d1 * cw[0]
    return (near + _shift_rows(far, 2, sub)).reshape(rows, cols)


M_TILES = 5 * M_WIDTH // M_TN


def _mproj_kernel(x_ref, prew_ref, w_ref, wg_ref, bif_ref, cw_ref, cb_ref,
                  mall_ref, kt_ref, gt_ref, h_ref):
    j = pl.program_id(1)

    def matmul():
        return _dot_nt(h_ref[...], w_ref[...].astype(BF16))

    @pl.when(j == 0)
    def _():
        x = x_ref[...]
        ms = jnp.mean(x * x, axis=-1, keepdims=True)
        h_ref[...] = (x * lax.rsqrt(ms + NORM_EPS) * prew_ref[...]).astype(BF16)
        g = _dot_nt(h_ref[...], wg_ref[...])
        gt = g.T[:2 * M_HEADS, :] + bif_ref[...]
        for c in range(N_CHUNKS):
            gt_ref[0, :, c, :] = gt[:, c * M_CHUNK:(c + 1) * M_CHUNK]

    @pl.when(j < 2)
    def _():
        y = _causal_conv(matmul(), cw_ref[...], cb_ref[...])
        mall_ref[...] = _silu(y).astype(BF16)

    @pl.when(jnp.logical_and(j >= 2, j < 4))
    def _():
        y = _causal_conv(matmul(), cw_ref[...], cb_ref[...])
        y = _silu(y) * (M_HEAD_DIM ** -0.5)
        mall_ref[...] = y.astype(BF16)
        for hh in range(M_TN // M_HEAD_DIM):
            for c in range(N_CHUNKS):
                blk = y[c * M_CHUNK:(c + 1) * M_CHUNK, hh * M_HEAD_DIM:(hh + 1) * M_HEAD_DIM]
                kt_ref[0, hh, c] = blk.T.astype(BF16)

    @pl.when(jnp.logical_and(j >= 4, j < 6))
    def _():
        mall_ref[...] = matmul().astype(BF16)

    @pl.when(jnp.logical_and(j >= 6, j < 8))
    def _():
        mall_ref[...] = _sigmoid(matmul()).astype(BF16)

    @pl.when(j >= 8)
    def _():
        mall_ref[...] = _silu(matmul()).astype(BF16)


def _mproj(x2d, pre_w, wt, wg, bif, conv_w, conv_b):
    return pl.pallas_call(
        _mproj_kernel,
        grid=(BATCH, M_TILES),
        in_specs=[
            pl.BlockSpec((SEQ, D_MODEL), lambda i, j: (i, 0)),
            pl.BlockSpec((1, D_MODEL), lambda i, j: (0, 0)),
            pl.BlockSpec((M_TN, D_MODEL), lambda i, j: (j, 0)),
            pl.BlockSpec((LANES, D_MODEL), lambda i, j: (0, 0)),
            pl.BlockSpec((2 * M_HEADS, 1), lambda i, j: (0, 0)),
            pl.BlockSpec((CONV_WIDTH, M_TN), lambda i, j: (0, jnp.minimum(j, 3))),
            pl.BlockSpec((1, M_TN), lambda i, j: (0, jnp.minimum(j, 3))),
        ],
        out_specs=[
            pl.BlockSpec((SEQ, M_TN), lambda i, j: (i, j)),
            pl.BlockSpec((1, M_TN // M_HEAD_DIM, N_CHUNKS, M_HEAD_DIM, M_CHUNK),
                         lambda i, j: (i, jnp.clip(j - 2, 0, 1), 0, 0, 0)),
            pl.BlockSpec((1, 2 * M_HEADS, N_CHUNKS, M_CHUNK), lambda i, j: (i, 0, 0, 0)),
            pl.BlockSpec((SEQ, D_MODEL), lambda i, j: (i, 0)),
        ],
        out_shape=[
            jax.ShapeDtypeStruct((BATCH * SEQ, 5 * M_WIDTH), BF16),
            jax.ShapeDtypeStruct((BATCH, M_HEADS, N_CHUNKS, M_HEAD_DIM, M_CHUNK), BF16),
            jax.ShapeDtypeStruct((BATCH, 2 * M_HEADS, N_CHUNKS, M_CHUNK), F32),
            jax.ShapeDtypeStruct((BATCH * SEQ, D_MODEL), BF16),
        ],
        compiler_params=pltpu.CompilerParams(dimension_semantics=("parallel", "arbitrary"),
                                             vmem_limit_bytes=VMEM_LIMIT),
        name="mproj",
    )(x2d, pre_w.reshape(1, D_MODEL), wt, wg, bif, conv_w, conv_b)


A_TILES = 3
EASY_STRIDE = 4


def _aproj_kernel(dil, h_ref, w_ref, tab_ref, o_ref, acc0_ref, acc1_ref, tmp_ref):
    s = pl.program_id(0)
    last = BATCH * A_TILES
    pt = (jnp.maximum(s, 1) - 1) % A_TILES

    accs = (acc0_ref, acc1_ref)

    def matmul(slot):
        res = _dot_nt(h_ref[...], w_ref[...].astype(BF16))
        for hd in range(A_HEADS):
            accs[slot][hd] = res[:, hd * A_HEAD_DIM:(hd + 1) * A_HEAD_DIM]

    ls = SEQ // dil

    two_pass = dil == EASY_STRIDE * EASY_STRIDE

    def stage(slot, hd):
        if two_pass:
            n4 = SEQ // EASY_STRIDE
            for r4 in range(EASY_STRIDE):
                tmp_ref[hd, r4 * n4:(r4 + 1) * n4, :] = accs[slot][hd, pl.ds(r4, n4, stride=EASY_STRIDE), :]

    def residue_rows(slot, hd, r):
        if dil == 1:
            return accs[slot][hd]
        if two_pass:
            q, r4 = divmod(r, EASY_STRIDE)
            return tmp_ref[hd, pl.ds(r4 * (SEQ // EASY_STRIDE) + q, ls, stride=EASY_STRIDE), :]
        return accs[slot][hd, pl.ds(r, ls, stride=dil), :]

    def rope_epilogue(slot):
        scale = jnp.where(pt == 0, A_HEAD_DIM ** -0.5 * LOG2E, 1.0).astype(F32)
        half = ROPE_DIM // 2
        for hd in range(A_HEADS):
            stage(slot, hd)
            for r in range(dil):
                rows = slice(r * ls, (r + 1) * ls)
                xh = residue_rows(slot, hd, r)
                y = (xh * tab_ref[0, rows, :] + pltpu.roll(xh, half, axis=1) * tab_ref[1, rows, :]
                     + pltpu.roll(xh, A_HEAD_DIM - half, axis=1) * tab_ref[2, rows, :])
                o_ref[0, 0, rows, hd * A_HEAD_DIM:(hd + 1) * A_HEAD_DIM] = (y * scale).astype(BF16)

    def cast_epilogue(slot):
        for hd in range(A_HEADS):
            stage(slot, hd)
            for r in range(dil):
                rows = slice(r * ls, (r + 1) * ls)
                o_ref[0, 0, rows, hd * A_HEAD_DIM:(hd + 1) * A_HEAD_DIM] = \
                    residue_rows(slot, hd, r).astype(BF16)

    mid = jnp.logical_and(s > 0, s < last)

    @pl.when(s == 0)
    def _():
        matmul(0)

    for slot in range(2):
        mine = jnp.logical_and(mid, s % 2 == slot)

        @pl.when(jnp.logical_and(mine, pt < 2))
        def _(slot=slot):
            rope_epilogue(1 - slot)
            matmul(slot)

        @pl.when(jnp.logical_and(mine, pt == 2))
        def _(slot=slot):
            cast_epilogue(1 - slot)
            matmul(slot)

    @pl.when(s == last)
    def _():
        cast_epilogue((last - 1) % 2)


def _aproj(h, wt, tabs_g, g):
    dil = A_GROUPS[g][1]

    def w_row(s):
        return OFF_QA + (cur(s) % A_TILES) * A_QKV_WIDTH + g * A_OUT_WIDTH
    last = BATCH * A_TILES

    def cur(s):
        return jnp.minimum(s, last - 1)

    def prev(s):
        return jnp.maximum(s, 1) - 1

    return pl.pallas_call(
        functools.partial(_aproj_kernel, dil),
        grid=(last + 1,),
        in_specs=[
            pl.BlockSpec((SEQ, D_MODEL), lambda s: (cur(s) // A_TILES, 0)),
            pl.BlockSpec((pl.Element(A_OUT_WIDTH), pl.Element(D_MODEL)),
                         lambda s: (pl.multiple_of(w_row(s), SUBLANES), 0)),
            pl.BlockSpec((3, SEQ, LANES), lambda s: (0, 0, 0)),
        ],
        out_specs=pl.BlockSpec((1, 1, SEQ, A_OUT_WIDTH),
                               lambda s: (prev(s) // A_TILES, prev(s) % A_TILES, 0, 0)),
        out_shape=jax.ShapeDtypeStruct((BATCH, A_TILES, SEQ, A_OUT_WIDTH), BF16),
        scratch_shapes=[pltpu.VMEM((A_HEADS, SEQ, A_HEAD_DIM), F32) for _ in range(3)],
        compiler_params=pltpu.CompilerParams(dimension_semantics=("arbitrary",),
                                             vmem_limit_bytes=VMEM_LIMIT),
        name="aproj_d%d" % dil,
    )(h, wt, tabs_g)


def _lane_scan(x, op, fill):
    lane = lax.broadcasted_iota(jnp.int32, x.shape, 1)
    sh = 1
    while sh < x.shape[1]:
        shifted = jnp.where(lane >= sh, pltpu.roll(x, sh, axis=1), fill)
        x = op(x, shifted)
        sh *= 2
    return x


def _gates_kernel(ig_ref, fg_ref, u_ref, g_ref, mt_ref, mp_ref, w_ref, dec_ref):
    L = M_CHUNK
    ig = ig_ref[...]
    fg = fg_ref[...]
    logf = -(jnp.maximum(-fg, 0.0) + jnp.log1p(jnp.exp(-jnp.abs(fg))))
    b = _lane_scan(logf, jnp.add, 0.0)
    u = ig - b
    cm = _lane_scan(u, jnp.maximum, NEG_INF)
    lane = lax.broadcasted_iota(jnp.int32, ig.shape, 1)
    btot = jnp.broadcast_to(jnp.sum(jnp.where(lane == L - 1, b, 0.0), axis=1, keepdims=True), ig.shape)
    umax = jnp.broadcast_to(jnp.max(cm, axis=1, keepdims=True), ig.shape)
    chunk = lax.broadcasted_iota(jnp.int32, ig.shape, 0) % N_CHUNKS
    m = jnp.zeros(ig.shape, F32)
    for c in range(N_CHUNKS - 1):
        nxt = pltpu.roll(btot + jnp.maximum(m, umax), 1, axis=0)
        m = jnp.where(chunk == c + 1, nxt, m)
    g = jnp.maximum(m, cm)
    glast = jnp.max(g, axis=1, keepdims=True)
    u_ref[...] = u * LOG2E
    g_ref[...] = g * LOG2E
    mt_ref[...] = (b + g) * LOG2E
    mp_ref[...] = m * LOG2E
    w_ref[...] = jnp.exp(u - glast)
    dec_ref[...] = jnp.broadcast_to(jnp.exp(m - glast), ig.shape)


def _gates(gt):
    rows = BATCH * M_HEADS * N_CHUNKS
    ig = gt[:, :M_HEADS].reshape(rows, M_CHUNK)
    fg = gt[:, M_HEADS:].reshape(rows, M_CHUNK)
    spec = pl.BlockSpec((rows, M_CHUNK), lambda: (0, 0))
    return pl.pallas_call(
        _gates_kernel,
        in_specs=[spec, spec],
        out_specs=[spec] * 6,
        out_shape=[jax.ShapeDtypeStruct((rows, M_CHUNK), F32)] * 6,
        name="gates",
    )(ig, fg)


def _mlstm_kernel(q_ref, kt_ref, v_ref, o_ref, z_ref, u_ref, g_ref, mt_ref, mp_ref, w_ref, dec_ref,
                  out_ref):
    L = M_CHUNK
    ti = lax.broadcasted_iota(jnp.int32, (L, L), 0)
    si = lax.broadcasted_iota(jnp.int32, (L, L), 1)
    causal = ti >= si
    ones = jnp.ones((L, LANES), BF16)

    state = jnp.zeros((M_HEAD_DIM, M_HEAD_DIM + LANES), F32)
    for c in range(N_CHUNKS):
        rs = slice(c * L, (c + 1) * L)
        cs = slice(c, c + 1)
        qc = q_ref[rs, :]
        v1 = jnp.concatenate([v_ref[rs, :], ones], axis=1)
        ktc = kt_ref[0, 0, c]
        u_r = u_ref[cs, :]
        gcol = jnp.broadcast_to(g_ref[cs, :], (LANES, L)).T
        mcol = jnp.broadcast_to(mt_ref[cs, :], (LANES, L)).T
        dmat = jnp.where(causal, jnp.exp2(u_r - jnp.concatenate([gcol, gcol], axis=1)), 0.0)

        s = jnp.dot(qc, ktc, preferred_element_type=F32) * dmat
        tot = jnp.dot(s.astype(BF16), v1, preferred_element_type=F32)
        if c > 0:
            inter = jnp.exp2(mp_ref[cs, :LANES] - gcol)
            tot = tot + jnp.concatenate([inter, inter, inter], axis=1) \
                * jnp.dot(qc, state.astype(BF16), preferred_element_type=F32)
        den = tot[:, M_HEAD_DIM:]
        rinv = 1.0 / jnp.maximum(jnp.abs(den), jnp.exp2(-mcol))
        hval = tot[:, :M_HEAD_DIM] * jnp.concatenate([rinv, rinv], axis=1)

        hg = hval * o_ref[rs, :].astype(F32)
        ms = jnp.mean(hg * hg, axis=1, keepdims=True)
        hn = hg * lax.rsqrt(ms + NORM_EPS)
        out_ref[rs, :] = (hn * z_ref[rs, :].astype(F32)).astype(BF16)

        if c + 1 < N_CHUNKS:
            ktw = (ktc.astype(F32) * w_ref[cs, :]).astype(BF16)
            dec = dec_ref[cs, :]
            dec = jnp.concatenate([dec, dec[:, :LANES]], axis=1)
            state = dec * state + jnp.dot(ktw, v1, preferred_element_type=F32)


def _mlstm(mall, kt, gate_stats):
    blk = (SEQ, M_HEAD_DIM)

    def seg(k):
        return pl.BlockSpec(blk, lambda b, h, k=k: (b, k * M_HEADS + h))

    stat = pl.BlockSpec((N_CHUNKS, M_CHUNK), lambda b, h: (b * M_HEADS + h, 0))
    return pl.pallas_call(
        _mlstm_kernel,
        grid=(BATCH, M_HEADS),
        in_specs=[
            seg(0),
            pl.BlockSpec((1, 1, N_CHUNKS, M_HEAD_DIM, M_CHUNK), lambda b, h: (b, h, 0, 0, 0)),
            seg(2), seg(3), seg(4),
        ] + [stat] * 6,
        out_specs=pl.BlockSpec(blk, lambda b, h: (b, h)),
        out_shape=jax.ShapeDtypeStruct((BATCH * SEQ, M_WIDTH), BF16),
        compiler_params=pltpu.CompilerParams(dimension_semantics=("parallel", "parallel"),
                                             vmem_limit_bytes=VMEM_LIMIT),
        name="mlstm",
    )(mall, kt, mall, mall, mall, *gate_stats)


GATHERED = (2,)
GATHER_PITCH = A_BLOCK + 8
ATTN_WAVE = 12


def _attn_block_order():
    order = []
    for g in reversed(range(1, N_GROUPS)):
        nblk = SEQ // A_GROUPS[g][1] // A_BLOCK
        order += [(r * nblk + n, g) for n in range(nblk) for r in range(A_GROUPS[g][1])]
    order += [(i, 0) for i in range(SEQ // A_BLOCK)]
    return order


def _attn_kernel(q0, k0, v0, q1, k1, v1, q2, k2, v2, out_ref,
                 a0, a1, a2, m0, m1, m2, l0, l1, l2):
    BLK = A_BLOCK
    ti = lax.broadcasted_iota(jnp.int32, (BLK, BLK), 0)
    ji = lax.broadcasted_iota(jnp.int32, (BLK, BLK), 1)
    cur_ok = ji <= ti
    ti2 = lax.broadcasted_iota(jnp.int32, (BLK, 2 * BLK), 0)
    krel = lax.broadcasted_iota(jnp.int32, (BLK, 2 * BLK), 1) - BLK
    band_ok = jnp.logical_and(krel <= ti2, krel >= ti2 - BLK)

    qs = (q0, q1, q2)
    ks = (k0, k1, k2)
    vs = (v0, v1, v2)
    accs = (a0, a1, a2)
    ms = (m0, m1, m2)
    ls = (l0, l1, l2)

    RB = 256

    def merge(rb):
        sl = slice(rb * RB, (rb + 1) * RB)

        def rows(refs, g):
            if g in GATHERED:
                dil = A_GROUPS[g][1]
                j0 = rb * RB // dil
                return jnp.concatenate([refs[g][pl.ds(j0 + jj, dil, stride=GATHER_PITCH), :]
                                        for jj in range(RB // dil)], axis=0)
            return refs[g][sl, :]

        mg = [rows(ms, g) for g in range(N_GROUPS)]
        mmax = jnp.maximum(jnp.maximum(mg[0], mg[1]), mg[2])
        num = None
        den = None
        for g in range(N_GROUPS):
            e = jnp.exp2(mg[g] - mmax)
            num = e * rows(accs, g) if num is None else num + e * rows(accs, g)
            den = e * rows(ls, g) if den is None else den + e * rows(ls, g)
        out_ref[sl, :] = (num * (1.0 / den)).astype(BF16)

    blocks = _attn_block_order()
    done_rows = [set() for _ in range(N_GROUPS)]
    merged = 0
    for w0 in range(0, len(blocks), ATTN_WAVE):
        wave = []
        for i, g in blocks[w0:w0 + ATTN_WAVE]:
            dil = A_GROUPS[g][1]
            r, n = divmod(i, SEQ // dil // BLK)
            rows = slice(i * BLK, (i + 1) * BLK)
            keys = slice((i - 1) * BLK, (i + 1) * BLK) if n > 0 else rows
            valid = band_ok if n > 0 else cur_ok
            s = lax.dot_general(qs[g][0, 0, rows, :], ks[g][0, 0, keys, :], (((1,), (1,)), ((), ())),
                                preferred_element_type=F32)
            s = jnp.where(valid, s, NEG_INF)
            m = jnp.max(s, axis=1, keepdims=True)
            tok0 = n * BLK * dil + r
            if g in GATHERED:
                idx = pl.ds(r * GATHER_PITCH, BLK)
            else:
                idx = pl.ds(tok0, BLK, stride=dil) if dil > 1 else pl.ds(tok0, BLK)
            wave.append((g, keys, valid, idx, s, m))
        for g, keys, valid, idx, s, m in wave:
            p = jnp.where(valid, jnp.exp2(s - m), 0.0)
            nkeys = keys.stop - keys.start
            v1 = jnp.concatenate([vs[g][0, 0, keys, :], jnp.ones((nkeys, LANES), BF16)], axis=1)
            acc = jnp.dot(p.astype(BF16), v1, preferred_element_type=F32)
            accs[g][idx, :] = acc[:, :LANES]
            ms[g][idx, :] = jnp.broadcast_to(m, (BLK, LANES))
            ls[g][idx, :] = acc[:, LANES:]
        for i, g in blocks[w0:w0 + ATTN_WAVE]:
            dil = A_GROUPS[g][1]
            r, n = divmod(i, SEQ // dil // BLK)
            span = BLK * dil
            if r == dil - 1:
                done_rows[g].update(range(n * span // RB, (n + 1) * span // RB))
        while merged < SEQ // RB and all(merged in d for d in done_rows):
            merge(merged)
            merged += 1


def _attn(qkv):
    in_specs = []
    args = []
    for g in range(N_GROUPS):
        for t in range(3):
            in_specs.append(pl.BlockSpec((1, 1, SEQ, A_HEAD_DIM), lambda b, h, t=t: (b, t, 0, h)))
            args.append(qkv[g])
    return pl.pallas_call(
        _attn_kernel,
        grid=(BATCH, A_HEADS),
        in_specs=in_specs,
        out_specs=pl.BlockSpec((SEQ, A_HEAD_DIM), lambda b, h: (b, h)),
        out_shape=jax.ShapeDtypeStruct((BATCH * SEQ, A_OUT_WIDTH), BF16),
        scratch_shapes=[pltpu.VMEM((A_GROUPS[g][1] * GATHER_PITCH if g in GATHERED else SEQ, LANES), F32)
                        for _ in range(3) for g in range(N_GROUPS)],
        compiler_params=pltpu.CompilerParams(dimension_semantics=("parallel", "parallel"),
                                             vmem_limit_bytes=VMEM_LIMIT),
        name="attn",
    )(*args)


O_TM = 1024


def _out_kernel(x_ref, prew_ref, hm_ref, att_ref, wg_ref, wpm_ref, wpa_ref, wout_ref, postw_ref,
                y_ref):
    x = x_ref[...]
    ms = jnp.mean(x * x, axis=-1, keepdims=True)
    h = (x * lax.rsqrt(ms + NORM_EPS) * prew_ref[...]).astype(BF16)
    gates = _dot_nt(h, wg_ref[...])
    za = _silu(gates[:, :A_OUT_WIDTH])
    sgm = _sigmoid(gates[:, A_OUT_WIDTH:A_OUT_WIDTH + D_MODEL])
    sga = _sigmoid(gates[:, A_OUT_WIDTH + D_MODEL:])
    a = (att_ref[...].astype(F32) * za).astype(BF16)
    bm = jnp.dot(hm_ref[...], wpm_ref[...], preferred_element_type=F32)
    ba = jnp.dot(a, wpa_ref[...], preferred_element_type=F32)
    merged = (sgm * bm + sga * ba).astype(BF16)
    y = jnp.dot(merged, wout_ref[...], preferred_element_type=F32)
    ms2 = jnp.mean(y * y, axis=-1, keepdims=True)
    y_ref[...] = x + y * lax.rsqrt(ms2 + NORM_EPS) * postw_ref[...]


def _out(x2d, pre_w, hm, att, wg4, wpm, wpa, wout, post_w):
    n = x2d.shape[0]
    const = lambda i: (0, 0)
    return pl.pallas_call(
        _out_kernel,
        grid=(n // O_TM,),
        in_specs=[
            pl.BlockSpec((O_TM, D_MODEL), lambda i: (i, 0)),
            pl.BlockSpec((1, D_MODEL), const),
            pl.BlockSpec((O_TM, M_WIDTH), lambda i: (i, 0)),
            pl.BlockSpec((O_TM, A_OUT_WIDTH), lambda i: (i, 0)),
            pl.BlockSpec((A_OUT_WIDTH + 2 * D_MODEL, D_MODEL), const),
            pl.BlockSpec((M_WIDTH, D_MODEL), const),
            pl.BlockSpec((A_OUT_WIDTH, D_MODEL), const),
            pl.BlockSpec((D_MODEL, D_MODEL), const),
            pl.BlockSpec((1, D_MODEL), const),
        ],
        out_specs=pl.BlockSpec((O_TM, D_MODEL), lambda i: (i, 0)),
        out_shape=jax.ShapeDtypeStruct((n, D_MODEL), F32),
        compiler_params=pltpu.CompilerParams(dimension_semantics=("parallel",),
                                             vmem_limit_bytes=VMEM_LIMIT),
        name="outproj",
    )(x2d, pre_w.reshape(1, D_MODEL), hm, att, wg4, wpm, wpa, wout, post_w.reshape(1, D_MODEL))


def _rope_tables():
    pos = np.arange(SEQ, dtype=np.float64)
    inv_freq = ROPE_THETA ** (-np.arange(0, ROPE_DIM, 2, dtype=np.float64) / ROPE_DIM)
    ang = pos[:, None] * inv_freq[None, :]
    cos, sin = np.cos(ang), np.sin(ang)
    half = ROPE_DIM // 2
    pad = A_HEAD_DIM - ROPE_DIM
    cos_t = np.concatenate([cos, cos, np.ones((SEQ, pad))], axis=1)
    sin_up = np.concatenate([np.zeros((SEQ, half)), sin, np.zeros((SEQ, pad))], axis=1)
    sin_dn = np.concatenate([-sin, np.zeros((SEQ, half + pad))], axis=1)
    tab = np.stack([cos_t, sin_up, sin_dn]).astype(np.float32)
    out = []
    for _, dil in A_GROUPS:
        ls = SEQ // dil
        out.append(tab.reshape(3, ls, dil, A_HEAD_DIM).transpose(0, 2, 1, 3).reshape(3, SEQ, A_HEAD_DIM))
    return jnp.asarray(np.stack(out))


def _layer(x, pre_w, w_in, b_if, conv_w, conv_b, m_norm_w, w_pm, w_pa, w_out, post_w):
    x2d = x.reshape(BATCH * SEQ, D_MODEL)
    wt = w_in.T
    wg = jnp.pad(wt[OFF_GATES:OFF_QA], ((0, LANES - 2 * M_HEADS), (0, 0))).astype(BF16)
    wg4 = wt[OFF_ZA:].astype(BF16)
    tabs = _rope_tables()

    mall, kt, gt, h = _mproj(x2d, pre_w, wt, wg, b_if.reshape(2 * M_HEADS, 1), conv_w,
                             conv_b.reshape(1, 2 * M_WIDTH))
    qkv = [_aproj(h, wt, tabs[g], g) for g in range(N_GROUPS)]
    hm = _mlstm(mall, kt, _gates(gt))
    att = _attn(qkv)
    w_pm_n = (m_norm_w[:, None] * w_pm).astype(BF16)
    y = _out(x2d, pre_w, hm, att, wg4, w_pm_n, w_pa.astype(BF16), w_out.astype(BF16), post_w)
    return y.reshape(BATCH, SEQ, D_MODEL)


@jax.jit
def kernel(x, pre_norm_w, w_in, b_if, conv_w, conv_b, m_norm_w, w_proj_m, w_proj_a, w_out, post_norm_w):
    for layer in range(pre_norm_w.shape[0]):
        x = _layer(x, pre_norm_w[layer], w_in[layer], b_if[layer], conv_w[layer], conv_b[layer],
                   m_norm_w[layer], w_proj_m[layer], w_proj_a[layer], w_out[layer], post_norm_w[layer])
    return x
```

```python
import functools

import jax
import jax.numpy as jnp
import numpy as np
from jax import lax
from jax.experimental import pallas as pl
from jax.experimental.pallas import tpu as pltpu

D_MODEL = 1024
BATCH = 8
SEQ = 2048
M_WIDTH = 1024
M_HEADS = 4
M_HEAD_DIM = 256
CONV_WIDTH = 4
A_GROUPS = ((128, 1), (512, 4), (2048, 16))
N_GROUPS = 3
A_HEADS = 4
A_HEAD_DIM = 128
A_OUT_WIDTH = 512
A_QKV_WIDTH = 1536
A_BLOCK = 128
ROPE_DIM = 32
ROPE_THETA = 500000.0
NORM_EPS = 1e-6
NEG_INF = -1e30
LOG2E = 1.4426950408889634

OFF_GATES = 5 * M_WIDTH
OFF_QA = OFF_GATES + 2 * M_HEADS
OFF_KA = OFF_QA + A_QKV_WIDTH
OFF_VA = OFF_KA + A_QKV_WIDTH
OFF_ZA = OFF_VA + A_QKV_WIDTH
OFF_GM = OFF_ZA + A_OUT_WIDTH
IN_WIDTH = OFF_GM + 2 * D_MODEL

LANES = 128
M_CHUNK = 256
N_CHUNKS = SEQ // M_CHUNK
VMEM_LIMIT = 48 * 1024 * 1024

F32 = jnp.float32
BF16 = jnp.bfloat16


def _dot_nt(a, bt):
    return lax.dot_general(a, bt, (((1,), (1,)), ((), ())), preferred_element_type=F32)


def _sigmoid(x):
    return 1.0 / (1.0 + jnp.exp(-x))


def _silu(x):
    return x * _sigmoid(x)


M_TN = 512


SUBLANES = 8


def _shift_rows(x3, s, sub):
    rot = pltpu.roll(x3, s, axis=1)
    prev = jnp.concatenate([jnp.zeros_like(rot[:1]), rot[:-1]], axis=0)
    return jnp.where(sub < s, prev, rot)


def _causal_conv(acc, cw, cb):
    rows, cols = acc.shape
    x3 = acc.reshape(rows // SUBLANES, SUBLANES, cols)
    sub = lax.broadcasted_iota(jnp.int32, (1, SUBLANES, cols), 1)
    d1 = _shift_rows(x3, 1, sub)
    near = x3 * cw[3] + d1 * cw[2] + cb[0]
    far = x3 * cw[1] + d1 * cw[0]
    return (near + _shift_rows(far, 2, sub)).reshape(rows, cols)


M_TILES = 5 * M_WIDTH // M_TN


def _mproj_kernel(x_ref, prew_ref, w_ref, wg_ref, bif_ref, cw_ref, cb_ref,
                  mall_ref, kt_ref, gt_ref, h_ref):
    j = pl.program_id(1)

    def matmul():
        return _dot_nt(h_ref[...], w_ref[...].astype(BF16))

    @pl.when(j == 0)
    def _():
        x = x_ref[...]
        ms = jnp.mean(x * x, axis=-1, keepdims=True)
        h_ref[...] = (x * lax.rsqrt(ms + NORM_EPS) * prew_ref[...]).astype(BF16)
        g = _dot_nt(h_ref[...], wg_ref[...])
        gt = g.T[:2 * M_HEADS, :] + bif_ref[...]
        for c in range(N_CHUNKS):
            gt_ref[0, :, c, :] = gt[:, c * M_CHUNK:(c + 1) * M_CHUNK]

    @pl.when(j < 2)
    def _():
        y = _causal_conv(matmul(), cw_ref[...], cb_ref[...])
        mall_ref[...] = _silu(y).astype(BF16)

    @pl.when(jnp.logical_and(j >= 2, j < 4))
    def _():
        y = _causal_conv(matmul(), cw_ref[...], cb_ref[...])
        y = _silu(y) * (M_HEAD_DIM ** -0.5)
        mall_ref[...] = y.astype(BF16)
        for hh in range(M_TN // M_HEAD_DIM):
            for c in range(N_CHUNKS):
                blk = y[c * M_CHUNK:(c + 1) * M_CHUNK, hh * M_HEAD_DIM:(hh + 1) * M_HEAD_DIM]
                kt_ref[0, hh, c] = blk.T.astype(BF16)

    @pl.when(jnp.logical_and(j >= 4, j < 6))
    def _():
        mall_ref[...] = matmul().astype(BF16)

    @pl.when(jnp.logical_and(j >= 6, j < 8))
    def _():
        mall_ref[...] = _sigmoid(matmul()).astype(BF16)

    @pl.when(j >= 8)
    def _():
        mall_ref[...] = _silu(matmul()).astype(BF16)


def _mproj(x2d, pre_w, wt, wg, bif, conv_w, conv_b):
    return pl.pallas_call(
        _mproj_kernel,
        grid=(BATCH, M_TILES),
        in_specs=[
            pl.BlockSpec((SEQ, D_MODEL), lambda i, j: (i, 0)),
            pl.BlockSpec((1, D_MODEL), lambda i, j: (0, 0)),
            pl.BlockSpec((M_TN, D_MODEL), lambda i, j: (j, 0)),
            pl.BlockSpec((LANES, D_MODEL), lambda i, j: (0, 0)),
            pl.BlockSpec((2 * M_HEADS, 1), lambda i, j: (0, 0)),
            pl.BlockSpec((CONV_WIDTH, M_TN), lambda i, j: (0, jnp.minimum(j, 3))),
            pl.BlockSpec((1, M_TN), lambda i, j: (0, jnp.minimum(j, 3))),
        ],
        out_specs=[
            pl.BlockSpec((SEQ, M_TN), lambda i, j: (i, j)),
            pl.BlockSpec((1, M_TN // M_HEAD_DIM, N_CHUNKS, M_HEAD_DIM, M_CHUNK),
                         lambda i, j: (i, jnp.clip(j - 2, 0, 1), 0, 0, 0)),
            pl.BlockSpec((1, 2 * M_HEADS, N_CHUNKS, M_CHUNK), lambda i, j: (i, 0, 0, 0)),
            pl.BlockSpec((SEQ, D_MODEL), lambda i, j: (i, 0)),
        ],
        out_shape=[
            jax.ShapeDtypeStruct((BATCH * SEQ, 5 * M_WIDTH), BF16),
            jax.ShapeDtypeStruct((BATCH, M_HEADS, N_CHUNKS, M_HEAD_DIM, M_CHUNK), BF16),
            jax.ShapeDtypeStruct((BATCH, 2 * M_HEADS, N_CHUNKS, M_CHUNK), F32),
            jax.ShapeDtypeStruct((BATCH * SEQ, D_MODEL), BF16),
        ],
        compiler_params=pltpu.CompilerParams(dimension_semantics=("parallel", "arbitrary"),
                                             vmem_limit_bytes=VMEM_LIMIT),
        name="mproj",
    )(x2d, pre_w.reshape(1, D_MODEL), wt, wg, bif, conv_w, conv_b)


A_TILES = 3
EASY_STRIDE = 4


def _aproj_kernel(dil, h_ref, w_ref, tab_ref, o_ref, acc0_ref, acc1_ref, tmp_ref):
    s = pl.program_id(0)
    last = BATCH * A_TILES
    pt = (jnp.maximum(s, 1) - 1) % A_TILES

    accs = (acc0_ref, acc1_ref)

    def matmul(slot):
        res = _dot_nt(h_ref[...], w_ref[...].astype(BF16))
        for hd in range(A_HEADS):
            accs[slot][hd] = res[:, hd * A_HEAD_DIM:(hd + 1) * A_HEAD_DIM]

    ls = SEQ // dil

    two_pass = dil == EASY_STRIDE * EASY_STRIDE

    def stage(slot, hd):
        if two_pass:
            n4 = SEQ // EASY_STRIDE
            for r4 in range(EASY_STRIDE):
                tmp_ref[hd, r4 * n4:(r4 + 1) * n4, :] = accs[slot][hd, pl.ds(r4, n4, stride=EASY_STRIDE), :]

    def residue_rows(slot, hd, r):
        if dil == 1:
            return accs[slot][hd]
        if two_pass:
            q, r4 = divmod(r, EASY_STRIDE)
            return tmp_ref[hd, pl.ds(r4 * (SEQ // EASY_STRIDE) + q, ls, stride=EASY_STRIDE), :]
        return accs[slot][hd, pl.ds(r, ls, stride=dil), :]

    def rope_epilogue(slot):
        scale = jnp.where(pt == 0, A_HEAD_DIM ** -0.5 * LOG2E, 1.0).astype(F32)
        half = ROPE_DIM // 2
        for hd in range(A_HEADS):
            stage(slot, hd)
            for r in range(dil):
                rows = slice(r * ls, (r + 1) * ls)
                xh = residue_rows(slot, hd, r)
                y = (xh * tab_ref[0, rows, :] + pltpu.roll(xh, half, axis=1) * tab_ref[1, rows, :]
                     + pltpu.roll(xh, A_HEAD_DIM - half, axis=1) * tab_ref[2, rows, :])
                o_ref[0, 0, rows, hd * A_HEAD_DIM:(hd + 1) * A_HEAD_DIM] = (y * scale).astype(BF16)

    def cast_epilogue(slot):
        for hd in range(A_HEADS):
            stage(slot, hd)
            for r in range(dil):
                rows = slice(r * ls, (r + 1) * ls)
                o_ref[0, 0, rows, hd * A_HEAD_DIM:(hd + 1) * A_HEAD_DIM] = \
                    residue_rows(slot, hd, r).astype(BF16)

    mid = jnp.logical_and(s > 0, s < last)

    @pl.when(s == 0)
    def _():
        matmul(0)

    for slot in range(2):
        mine = jnp.logical_and(mid, s % 2 == slot)

        @pl.when(jnp.logical_and(mine, pt < 2))
        def _(slot=slot):
            rope_epilogue(1 - slot)
            matmul(slot)

        @pl.when(jnp.logical_and(mine, pt == 2))
        def _(slot=slot):
            cast_epilogue(1 - slot)
            matmul(slot)

    @pl.when(s == last)
    def _():
        cast_epilogue((last - 1) % 2)


def _aproj(h, wt, tabs_g, g):
    dil = A_GROUPS[g][1]

    def w_row(s):
        return OFF_QA + (cur(s) % A_TILES) * A_QKV_WIDTH + g * A_OUT_WIDTH
    last = BATCH * A_TILES

    def cur(s):
        return jnp.minimum(s, last - 1)

    def prev(s):
        return jnp.maximum(s, 1) - 1

    return pl.pallas_call(
        functools.partial(_aproj_kernel, dil),
        grid=(last + 1,),
        in_specs=[
            pl.BlockSpec((SEQ, D_MODEL), lambda s: (cur(s) // A_TILES, 0)),
            pl.BlockSpec((pl.Element(A_OUT_WIDTH), pl.Element(D_MODEL)),
                         lambda s: (pl.multiple_of(w_row(s), SUBLANES), 0)),
            pl.BlockSpec((3, SEQ, LANES), lambda s: (0, 0, 0)),
        ],
        out_specs=pl.BlockSpec((1, 1, SEQ, A_OUT_WIDTH),
                               lambda s: (prev(s) // A_TILES, prev(s) % A_TILES, 0, 0)),
        out_shape=jax.ShapeDtypeStruct((BATCH, A_TILES, SEQ, A_OUT_WIDTH), BF16),
        scratch_shapes=[pltpu.VMEM((A_HEADS, SEQ, A_HEAD_DIM), F32) for _ in range(3)],
        compiler_params=pltpu.CompilerParams(dimension_semantics=("arbitrary",),
                                             vmem_limit_bytes=VMEM_LIMIT),
        name="aproj_d%d" % dil,
    )(h, wt, tabs_g)


def _lane_scan(x, op, fill):
    lane = lax.broadcasted_iota(jnp.int32, x.shape, 1)
    sh = 1
    while sh < x.shape[1]:
        shifted = jnp.where(lane >= sh, pltpu.roll(x, sh, axis=1), fill)
        x = op(x, shifted)
        sh *= 2
    return x


def _gates_kernel(ig_ref, fg_ref, u_ref, g_ref, mt_ref, mp_ref, w_ref, dec_ref):
    L = M_CHUNK
    ig = ig_ref[...]
    fg = fg_ref[...]
    logf = -(jnp.maximum(-fg, 0.0) + jnp.log1p(jnp.exp(-jnp.abs(fg))))
    b = _lane_scan(logf, jnp.add, 0.0)
    u = ig - b
    cm = _lane_scan(u, jnp.maximum, NEG_INF)
    lane = lax.broadcasted_iota(jnp.int32, ig.shape, 1)
    btot = jnp.broadcast_to(jnp.sum(jnp.where(lane == L - 1, b, 0.0), axis=1, keepdims=True), ig.shape)
    umax = jnp.broadcast_to(jnp.max(cm, axis=1, keepdims=True), ig.shape)
    chunk = lax.broadcasted_iota(jnp.int32, ig.shape, 0) % N_CHUNKS
    m = jnp.zeros(ig.shape, F32)
    for c in range(N_CHUNKS - 1):
        nxt = pltpu.roll(btot + jnp.maximum(m, umax), 1, axis=0)
        m = jnp.where(chunk == c + 1, nxt, m)
    g = jnp.maximum(m, cm)
    glast = jnp.max(g, axis=1, keepdims=True)
    u_ref[...] = u * LOG2E
    g_ref[...] = g * LOG2E
    mt_ref[...] = (b + g) * LOG2E
    mp_ref[...] = m * LOG2E
    w_ref[...] = jnp.exp(u - glast)
    dec_ref[...] = jnp.broadcast_to(jnp.exp(m - glast), ig.shape)


def _gates(gt):
    rows = BATCH * M_HEADS * N_CHUNKS
    ig = gt[:, :M_HEADS].reshape(rows, M_CHUNK)
    fg = gt[:, M_HEADS:].reshape(rows, M_CHUNK)
    spec = pl.BlockSpec((rows, M_CHUNK), lambda: (0, 0))
    return pl.pallas_call(
        _gates_kernel,
        in_specs=[spec, spec],
        out_specs=[spec] * 6,
        out_shape=[jax.ShapeDtypeStruct((rows, M_CHUNK), F32)] * 6,
        name="gates",
    )(ig, fg)


def _mlstm_kernel(q_ref, kt_ref, v_ref, o_ref, z_ref, u_ref, g_ref, mt_ref, mp_ref, w_ref, dec_ref,
                  out_ref):
    L = M_CHUNK
    ti = lax.broadcasted_iota(jnp.int32, (L, L), 0)
    si = lax.broadcasted_iota(jnp.int32, (L, L), 1)
    causal = ti >= si
    ones = jnp.ones((L, LANES), BF16)

    state = jnp.zeros((M_HEAD_DIM, M_HEAD_DIM + LANES), F32)
    for c in range(N_CHUNKS):
        rs = slice(c * L, (c + 1) * L)
        cs = slice(c, c + 1)
        qc = q_ref[rs, :]
        v1 = jnp.concatenate([v_ref[rs, :], ones], axis=1)
        ktc = kt_ref[0, 0, c]
        u_r = u_ref[cs, :]
        gcol = jnp.broadcast_to(g_ref[cs, :], (LANES, L)).T
        mcol = jnp.broadcast_to(mt_ref[cs, :], (LANES, L)).T
        dmat = jnp.where(causal, jnp.exp2(u_r - jnp.concatenate([gcol, gcol], axis=1)), 0.0)

        s = jnp.dot(qc, ktc, preferred_element_type=F32) * dmat
        tot = jnp.dot(s.astype(BF16), v1, preferred_element_type=F32)
        if c > 0:
            inter = jnp.exp2(mp_ref[cs, :LANES] - gcol)
            tot = tot + jnp.concatenate([inter, inter, inter], axis=1) \
                * jnp.dot(qc, state.astype(BF16), preferred_element_type=F32)
        den = tot[:, M_HEAD_DIM:]
        rinv = 1.0 / jnp.maximum(jnp.abs(den), jnp.exp2(-mcol))
        hval = tot[:, :M_HEAD_DIM] * jnp.concatenate([rinv, rinv], axis=1)

        hg = hval * o_ref[rs, :].astype(F32)
        ms = jnp.mean(hg * hg, axis=1, keepdims=True)
        hn = hg * lax.rsqrt(ms + NORM_EPS)
        out_ref[rs, :] = (hn * z_ref[rs, :].astype(F32)).astype(BF16)

        if c + 1 < N_CHUNKS:
            ktw = (ktc.astype(F32) * w_ref[cs, :]).astype(BF16)
            dec = dec_ref[cs, :]
            dec = jnp.concatenate([dec, dec[:, :LANES]], axis=1)
            state = dec * state + jnp.dot(ktw, v1, preferred_element_type=F32)


def _mlstm(mall, kt, gate_stats):
    blk = (SEQ, M_HEAD_DIM)

    def seg(k):
        return pl.BlockSpec(blk, lambda b, h, k=k: (b, k * M_HEADS + h))

    stat = pl.BlockSpec((N_CHUNKS, M_CHUNK), lambda b, h: (b * M_HEADS + h, 0))
    return pl.pallas_call(
        _mlstm_kernel,
        grid=(BATCH, M_HEADS),
        in_specs=[
            seg(0),
            pl.BlockSpec((1, 1, N_CHUNKS, M_HEAD_DIM, M_CHUNK), lambda b, h: (b, h, 0, 0, 0)),
            seg(2), seg(3), seg(4),
        ] + [stat] * 6,
        out_specs=pl.BlockSpec(blk, lambda b, h: (b, h)),
        out_shape=jax.ShapeDtypeStruct((BATCH * SEQ, M_WIDTH), BF16),
        compiler_params=pltpu.CompilerParams(dimension_semantics=("parallel", "parallel"),
                                             vmem_limit_bytes=VMEM_LIMIT),
        name="mlstm",
    )(mall, kt, mall, mall, mall, *gate_stats)


GATHERED = (2,)
GATHER_PITCH = A_BLOCK + 8
ATTN_WAVE = 12


def _attn_block_order():
    order = []
    for g in reversed(range(1, N_GROUPS)):
        nblk = SEQ // A_GROUPS[g][1] // A_BLOCK
        order += [(r * nblk + n, g) for n in range(nblk) for r in range(A_GROUPS[g][1])]
    order += [(i, 0) for i in range(SEQ // A_BLOCK)]
    return order


def _attn_kernel(q0, k0, v0, q1, k1, v1, q2, k2, v2, out_ref,
                 a0, a1, a2, m0, m1, m2, l0, l1, l2):
    BLK = A_BLOCK
    ti = lax.broadcasted_iota(jnp.int32, (BLK, BLK), 0)
    ji = lax.broadcasted_iota(jnp.int32, (BLK, BLK), 1)
    cur_ok = ji <= ti
    ti2 = lax.broadcasted_iota(jnp.int32, (BLK, 2 * BLK), 0)
    krel = lax.broadcasted_iota(jnp.int32, (BLK, 2 * BLK), 1) - BLK
    band_ok = jnp.logical_and(krel <= ti2, krel >= ti2 - BLK)

    qs = (q0, q1, q2)
    ks = (k0, k1, k2)
    vs = (v0, v1, v2)
    accs = (a0, a1, a2)
    ms = (m0, m1, m2)
    ls = (l0, l1, l2)

    RB = 256

    def merge(rb):
        sl = slice(rb * RB, (rb + 1) * RB)

        def rows(refs, g):
            if g in GATHERED:
                dil = A_GROUPS[g][1]
                j0 = rb * RB // dil
                return jnp.concatenate([refs[g][pl.ds(j0 + jj, dil, stride=GATHER_PITCH), :]
                                        for jj in range(RB // dil)], axis=0)
            return refs[g][sl, :]

        mg = [rows(ms, g) for g in range(N_GROUPS)]
        mmax = jnp.maximum(jnp.maximum(mg[0], mg[1]), mg[2])
        num = None
        den = None
        for g in range(N_GROUPS):
            e = jnp.exp2(mg[g] - mmax)
            num = e * rows(accs, g) if num is None else num + e * rows(accs, g)
            den = e * rows(ls, g) if den is None else den + e * rows(ls, g)
        out_ref[sl, :] = (num * (1.0 / den)).astype(BF16)

    blocks = _attn_block_order()
    done_rows = [set() for _ in range(N_GROUPS)]
    merged = 0
    for w0 in range(0, len(blocks), ATTN_WAVE):
        wave = []
        for i, g in blocks[w0:w0 + ATTN_WAVE]:
            dil = A_GROUPS[g][1]
            r, n = divmod(i, SEQ // dil // BLK)
            rows = slice(i * BLK, (i + 1) * BLK)
            keys = slice((i - 1) * BLK, (i + 1) * BLK) if n > 0 else rows
            valid = band_ok if n > 0 else cur_ok
            s = lax.dot_general(qs[g][0, 0, rows, :], ks[g][0, 0, keys, :], (((1,), (1,)), ((), ())),
                                preferred_element_type=F32)
            s = jnp.where(valid, s, NEG_INF)
            m = jnp.max(s, axis=1, keepdims=True)
            tok0 = n * BLK * dil + r
            if g in GATHERED:
                idx = pl.ds(r * GATHER_PITCH, BLK)
            else:
                idx = pl.ds(tok0, BLK, stride=dil) if dil > 1 else pl.ds(tok0, BLK)
            wave.append((g, keys, valid, idx, s, m))
        for g, keys, valid, idx, s, m in wave:
            p = jnp.where(valid, jnp.exp2(s - m), 0.0)
            nkeys = keys.stop - keys.start
            v1 = jnp.concatenate([vs[g][0, 0, keys, :], jnp.ones((nkeys, LANES), BF16)], axis=1)
            acc = jnp.dot(p.astype(BF16), v1, preferred_element_type=F32)
            accs[g][idx, :] = acc[:, :LANES]
            ms[g][idx, :] = jnp.broadcast_to(m, (BLK, LANES))
            ls[g][idx, :] = acc[:, LANES:]
        for i, g in blocks[w0:w0 + ATTN_WAVE]:
            dil = A_GROUPS[g][1]
            r, n = divmod(i, SEQ // dil // BLK)
            span = BLK * dil
            if r == dil - 1:
                done_rows[g].update(range(n * span // RB, (n + 1) * span // RB))
        while merged < SEQ // RB and all(merged in d for d in done_rows):
            merge(merged)
            merged += 1


def _attn(qkv):
    in_specs = []
    args = []
    for g in range(N_GROUPS):
        for t in range(3):
            in_specs.append(pl.BlockSpec((1, 1, SEQ, A_HEAD_DIM), lambda b, h, t=t: (b, t, 0, h)))
            args.append(qkv[g])
    return pl.pallas_call(
        _attn_kernel,
        grid=(BATCH, A_HEADS),
        in_specs=in_specs,
        out_specs=pl.BlockSpec((SEQ, A_HEAD_DIM), lambda b, h: (b, h)),
        out_shape=jax.ShapeDtypeStruct((BATCH * SEQ, A_OUT_WIDTH), BF16),
        scratch_shapes=[pltpu.VMEM((A_GROUPS[g][1] * GATHER_PITCH if g in GATHERED else SEQ, LANES), F32)
                        for _ in range(3) for g in range(N_GROUPS)],
        compiler_params=pltpu.CompilerParams(dimension_semantics=("parallel", "parallel"),
                                             vmem_limit_bytes=VMEM_LIMIT),
        name="attn",
    )(*args)


O_TM = 1024


def _out_kernel(x_ref, prew_ref, hm_ref, att_ref, wg_ref, wpm_ref, wpa_ref, wout_ref, postw_ref,
                y_ref, wgb_ref):
    @pl.when(pl.program_id(0) == 0)
    def _():
        wgb_ref[...] = wg_ref[...].astype(BF16)

    x = x_ref[...]
    ms = jnp.mean(x * x, axis=-1, keepdims=True)
    h = (x * lax.rsqrt(ms + NORM_EPS) * prew_ref[...]).astype(BF16)
    gates = _dot_nt(h, wgb_ref[...])
    za = _silu(gates[:, :A_OUT_WIDTH])
    sgm = _sigmoid(gates[:, A_OUT_WIDTH:A_OUT_WIDTH + D_MODEL])
    sga = _sigmoid(gates[:, A_OUT_WIDTH + D_MODEL:])
    a = (att_ref[...].astype(F32) * za).astype(BF16)
    bm = jnp.dot(hm_ref[...], wpm_ref[...], preferred_element_type=F32)
    ba = jnp.dot(a, wpa_ref[...], preferred_element_type=F32)
    merged = (sgm * bm + sga * ba).astype(BF16)
    y = jnp.dot(merged, wout_ref[...], preferred_element_type=F32)
    ms2 = jnp.mean(y * y, axis=-1, keepdims=True)
    y_ref[...] = x + y * lax.rsqrt(ms2 + NORM_EPS) * postw_ref[...]


def _out(x2d, pre_w, hm, att, wt, wpm, wpa, wout, post_w):
    n = x2d.shape[0]
    const = lambda i: (0, 0)
    return pl.pallas_call(
        _out_kernel,
        grid=(n // O_TM,),
        in_specs=[
            pl.BlockSpec((O_TM, D_MODEL), lambda i: (i, 0)),
            pl.BlockSpec((1, D_MODEL), const),
            pl.BlockSpec((O_TM, M_WIDTH), lambda i: (i, 0)),
            pl.BlockSpec((O_TM, A_OUT_WIDTH), lambda i: (i, 0)),
            pl.BlockSpec((pl.Element(IN_WIDTH - OFF_ZA), pl.Element(D_MODEL)), lambda i: (OFF_ZA, 0),
                         pipeline_mode=pl.Buffered(1)),
            pl.BlockSpec((M_WIDTH, D_MODEL), const),
            pl.BlockSpec((A_OUT_WIDTH, D_MODEL), const),
            pl.BlockSpec((D_MODEL, D_MODEL), const),
            pl.BlockSpec((1, D_MODEL), const),
        ],
        out_specs=pl.BlockSpec((O_TM, D_MODEL), lambda i: (i, 0)),
        out_shape=jax.ShapeDtypeStruct((n, D_MODEL), F32),
        scratch_shapes=[pltpu.VMEM((IN_WIDTH - OFF_ZA, D_MODEL), BF16)],
        compiler_params=pltpu.CompilerParams(dimension_semantics=("arbitrary",),
                                             vmem_limit_bytes=VMEM_LIMIT),
        name="outproj",
    )(x2d, pre_w.reshape(1, D_MODEL), hm, att, wt, wpm, wpa, wout, post_w.reshape(1, D_MODEL))


def _rope_tables():
    pos = np.arange(SEQ, dtype=np.float64)
    inv_freq = ROPE_THETA ** (-np.arange(0, ROPE_DIM, 2, dtype=np.float64) / ROPE_DIM)
    ang = pos[:, None] * inv_freq[None, :]
    cos, sin = np.cos(ang), np.sin(ang)
    half = ROPE_DIM // 2
    pad = A_HEAD_DIM - ROPE_DIM
    cos_t = np.concatenate([cos, cos, np.ones((SEQ, pad))], axis=1)
    sin_up = np.concatenate([np.zeros((SEQ, half)), sin, np.zeros((SEQ, pad))], axis=1)
    sin_dn = np.concatenate([-sin, np.zeros((SEQ, half + pad))], axis=1)
    tab = np.stack([cos_t, sin_up, sin_dn]).astype(np.float32)
    out = []
    for _, dil in A_GROUPS:
        ls = SEQ // dil
        out.append(tab.reshape(3, ls, dil, A_HEAD_DIM).transpose(0, 2, 1, 3).reshape(3, SEQ, A_HEAD_DIM))
    return jnp.asarray(np.stack(out))


def _layer(x, pre_w, w_in, b_if, conv_w, conv_b, m_norm_w, w_pm, w_pa, w_out, post_w):
    x2d = x.reshape(BATCH * SEQ, D_MODEL)
    wt = w_in.T
    wg = jnp.pad(wt[OFF_GATES:OFF_QA], ((0, LANES - 2 * M_HEADS), (0, 0))).astype(BF16)
    tabs = _rope_tables()

    mall, kt, gt, h = _mproj(x2d, pre_w, wt, wg, b_if.reshape(2 * M_HEADS, 1), conv_w,
                             conv_b.reshape(1, 2 * M_WIDTH))
    qkv = [_aproj(h, wt, tabs[g], g) for g in range(N_GROUPS)]
    hm = _mlstm(mall, kt, _gates(gt))
    att = _attn(qkv)
    w_pm_n = (m_norm_w[:, None] * w_pm).astype(BF16)
    y = _out(x2d, pre_w, hm, att, wt, w_pm_n, w_pa.astype(BF16), w_out.astype(BF16), post_w)
    return y.reshape(BATCH, SEQ, D_MODEL)


@jax.jit
def kernel(x, pre_norm_w, w_in, b_if, conv_w, conv_b, m_norm_w, w_proj_m, w_proj_a, w_out, post_norm_w):
    for layer in range(pre_norm_w.shape[0]):
        x = _layer(x, pre_norm_w[layer], w_in[layer], b_if[layer], conv_w[layer], conv_b[layer],
                   m_norm_w[layer], w_proj_m[layer], w_proj_a[layer], w_out[layer], post_norm_w[layer])
    return x
```

```python
import functools

import jax
import jax.numpy as jnp
import numpy as np
from jax import lax
from jax.experimental import pallas as pl
from jax.experimental.pallas import tpu as pltpu

D_MODEL = 1024
BATCH = 8
SEQ = 2048
M_WIDTH = 1024
M_HEADS = 4
M_HEAD_DIM = 256
CONV_WIDTH = 4
A_GROUPS = ((128, 1), (512, 4), (2048, 16))
N_GROUPS = 3
A_HEADS = 4
A_HEAD_DIM = 128
A_OUT_WIDTH = 512
A_QKV_WIDTH = 1536
A_BLOCK = 128
ROPE_DIM = 32
ROPE_THETA = 500000.0
NORM_EPS = 1e-6
NEG_INF = -1e30
LOG2E = 1.4426950408889634

OFF_GATES = 5 * M_WIDTH
OFF_QA = OFF_GATES + 2 * M_HEADS
OFF_KA = OFF_QA + A_QKV_WIDTH
OFF_VA = OFF_KA + A_QKV_WIDTH
OFF_ZA = OFF_VA + A_QKV_WIDTH
OFF_GM = OFF_ZA + A_OUT_WIDTH
IN_WIDTH = OFF_GM + 2 * D_MODEL

LANES = 128
M_CHUNK = 256
N_CHUNKS = SEQ // M_CHUNK
VMEM_LIMIT = 48 * 1024 * 1024

F32 = jnp.float32
BF16 = jnp.bfloat16


def _dot_nt(a, bt):
    return lax.dot_general(a, bt, (((1,), (1,)), ((), ())), preferred_element_type=F32)


def _sigmoid(x):
    return 1.0 / (1.0 + jnp.exp(-x))


def _silu(x):
    return x * _sigmoid(x)


M_TN = 512


SUBLANES = 8


def _shift_rows(x3, s, sub):
    rot = pltpu.roll(x3, s, axis=1)
    prev = jnp.concatenate([jnp.zeros_like(rot[:1]), rot[:-1]], axis=0)
    return jnp.where(sub < s, prev, rot)


def _causal_conv(acc, cw, cb):
    rows, cols = acc.shape
    x3 = acc.reshape(rows // SUBLANES, SUBLANES, cols)
    sub = lax.broadcasted_iota(jnp.int32, (1, SUBLANES, cols), 1)
    d1 = _shift_rows(x3, 1, sub)
    near = x3 * cw[3] + d1 * cw[2] + cb[0]
    far = x3 * cw[1] + d1 * cw[0]
    return (near + _shift_rows(far, 2, sub)).reshape(rows, cols)


M_TILES = 5 * M_WIDTH // M_TN


def _mproj_kernel(x_ref, prew_ref, w_ref, wg_ref, bif_ref, cw_ref, cb_ref,
                  mall_ref, kt_ref, gt_ref, h_ref):
    j = pl.program_id(1)

    def matmul():
        return _dot_nt(h_ref[...], w_ref[...].astype(BF16))

    @pl.when(j == 0)
    def _():
        x = x_ref[...]
        ms = jnp.mean(x * x, axis=-1, keepdims=True)
        h_ref[...] = (x * lax.rsqrt(ms + NORM_EPS) * prew_ref[...]).astype(BF16)
        g = _dot_nt(h_ref[...], wg_ref[...])
        gt = g.T[:2 * M_HEADS, :] + bif_ref[...]
        for c in range(N_CHUNKS):
            gt_ref[0, :, c, :] = gt[:, c * M_CHUNK:(c + 1) * M_CHUNK]

    @pl.when(j < 2)
    def _():
        y = _causal_conv(matmul(), cw_ref[...], cb_ref[...])
        mall_ref[...] = _silu(y).astype(BF16)

    @pl.when(jnp.logical_and(j >= 2, j < 4))
    def _():
        y = _causal_conv(matmul(), cw_ref[...], cb_ref[...])
        y = _silu(y) * (M_HEAD_DIM ** -0.5)
        mall_ref[...] = y.astype(BF16)
        for hh in range(M_TN // M_HEAD_DIM):
            for c in range(N_CHUNKS):
                blk = y[c * M_CHUNK:(c + 1) * M_CHUNK, hh * M_HEAD_DIM:(hh + 1) * M_HEAD_DIM]
                kt_ref[0, hh, c] = blk.T.astype(BF16)

    @pl.when(jnp.logical_and(j >= 4, j < 6))
    def _():
        mall_ref[...] = matmul().astype(BF16)

    @pl.when(jnp.logical_and(j >= 6, j < 8))
    def _():
        mall_ref[...] = _sigmoid(matmul()).astype(BF16)

    @pl.when(j >= 8)
    def _():
        mall_ref[...] = _silu(matmul()).astype(BF16)


def _mproj(x2d, pre_w, wt, wg, bif, conv_w, conv_b):
    return pl.pallas_call(
        _mproj_kernel,
        grid=(BATCH, M_TILES),
        in_specs=[
            pl.BlockSpec((SEQ, D_MODEL), lambda i, j: (i, 0)),
            pl.BlockSpec((1, D_MODEL), lambda i, j: (0, 0)),
            pl.BlockSpec((M_TN, D_MODEL), lambda i, j: (j, 0)),
            pl.BlockSpec((LANES, D_MODEL), lambda i, j: (0, 0)),
            pl.BlockSpec((2 * M_HEADS, 1), lambda i, j: (0, 0)),
            pl.BlockSpec((CONV_WIDTH, M_TN), lambda i, j: (0, jnp.minimum(j, 3))),
            pl.BlockSpec((1, M_TN), lambda i, j: (0, jnp.minimum(j, 3))),
        ],
        out_specs=[
            pl.BlockSpec((SEQ, M_TN), lambda i, j: (i, j)),
            pl.BlockSpec((1, M_TN // M_HEAD_DIM, N_CHUNKS, M_HEAD_DIM, M_CHUNK),
                         lambda i, j: (i, jnp.clip(j - 2, 0, 1), 0, 0, 0)),
            pl.BlockSpec((1, 2 * M_HEADS, N_CHUNKS, M_CHUNK), lambda i, j: (i, 0, 0, 0)),
            pl.BlockSpec((SEQ, D_MODEL), lambda i, j: (i, 0)),
        ],
        out_shape=[
            jax.ShapeDtypeStruct((BATCH * SEQ, 5 * M_WIDTH), BF16),
            jax.ShapeDtypeStruct((BATCH, M_HEADS, N_CHUNKS, M_HEAD_DIM, M_CHUNK), BF16),
            jax.ShapeDtypeStruct((BATCH, 2 * M_HEADS, N_CHUNKS, M_CHUNK), F32),
            jax.ShapeDtypeStruct((BATCH * SEQ, D_MODEL), BF16),
        ],
        compiler_params=pltpu.CompilerParams(dimension_semantics=("parallel", "arbitrary"),
                                             vmem_limit_bytes=VMEM_LIMIT),
        name="mproj",
    )(x2d, pre_w.reshape(1, D_MODEL), wt, wg, bif, conv_w, conv_b)


A_TILES = 3
EASY_STRIDE = 4
ROT_LANE = LANES // 2


def _aproj_kernel(dil, h_ref, w_ref, tab_ref, o_ref, acc0_ref, acc1_ref, tmp_ref):
    s = pl.program_id(0)
    last = BATCH * A_TILES
    pt = (jnp.maximum(s, 1) - 1) % A_TILES

    accs = (acc0_ref, acc1_ref)

    def matmul(slot, rotary):
        w = w_ref[...].astype(BF16)
        if rotary:
            half = ROPE_DIM // 2
            order = ((0, half), (ROPE_DIM, ROT_LANE + half), (half, ROPE_DIM), (ROT_LANE + half, A_HEAD_DIM))
            w = jnp.concatenate([w[hd * A_HEAD_DIM + lo:hd * A_HEAD_DIM + hi]
                                 for hd in range(A_HEADS) for lo, hi in order], axis=0)
        res = _dot_nt(h_ref[...], w)
        for hd in range(A_HEADS):
            accs[slot][hd] = res[:, hd * A_HEAD_DIM:(hd + 1) * A_HEAD_DIM]

    ls = SEQ // dil

    two_pass = dil == EASY_STRIDE * EASY_STRIDE

    def stage(slot, hd):
        if two_pass:
            n4 = SEQ // EASY_STRIDE
            for r4 in range(EASY_STRIDE):
                tmp_ref[hd, r4 * n4:(r4 + 1) * n4, :] = accs[slot][hd, pl.ds(r4, n4, stride=EASY_STRIDE), :]

    def residue_rows(slot, hd, r):
        if dil == 1:
            return accs[slot][hd]
        if two_pass:
            q, r4 = divmod(r, EASY_STRIDE)
            return tmp_ref[hd, pl.ds(r4 * (SEQ // EASY_STRIDE) + q, ls, stride=EASY_STRIDE), :]
        return accs[slot][hd, pl.ds(r, ls, stride=dil), :]

    def rope_epilogue(slot, scale):
        for hd in range(A_HEADS):
            stage(slot, hd)
            for r in range(dil):
                rows = slice(r * ls, (r + 1) * ls)
                xh = residue_rows(slot, hd, r)
                y = xh * tab_ref[0, rows, :] + pltpu.roll(xh, ROT_LANE, axis=1) * tab_ref[1, rows, :]
                if scale is not None:
                    y = y * scale
                o_ref[0, 0, rows, hd * A_HEAD_DIM:(hd + 1) * A_HEAD_DIM] = y.astype(BF16)

    def cast_epilogue(slot):
        for hd in range(A_HEADS):
            stage(slot, hd)
            for r in range(dil):
                rows = slice(r * ls, (r + 1) * ls)
                o_ref[0, 0, rows, hd * A_HEAD_DIM:(hd + 1) * A_HEAD_DIM] = \
                    residue_rows(slot, hd, r).astype(BF16)

    mid = jnp.logical_and(s > 0, s < last)

    @pl.when(s == 0)
    def _():
        matmul(0, True)

    for slot in range(2):
        mine = jnp.logical_and(mid, s % 2 == slot)

        @pl.when(jnp.logical_and(mine, pt == 0))
        def _(slot=slot):
            rope_epilogue(1 - slot, A_HEAD_DIM ** -0.5 * LOG2E)
            matmul(slot, True)

        @pl.when(jnp.logical_and(mine, pt == 1))
        def _(slot=slot):
            rope_epilogue(1 - slot, None)
            matmul(slot, False)

        @pl.when(jnp.logical_and(mine, pt == 2))
        def _(slot=slot):
            cast_epilogue(1 - slot)
            matmul(slot, True)

    @pl.when(s == last)
    def _():
        cast_epilogue((last - 1) % 2)


def _aproj(h, wt, tabs_g, g):
    dil = A_GROUPS[g][1]

    def w_row(s):
        return OFF_QA + (cur(s) % A_TILES) * A_QKV_WIDTH + g * A_OUT_WIDTH
    last = BATCH * A_TILES

    def cur(s):
        return jnp.minimum(s, last - 1)

    def prev(s):
        return jnp.maximum(s, 1) - 1

    return pl.pallas_call(
        functools.partial(_aproj_kernel, dil),
        grid=(last + 1,),
        in_specs=[
            pl.BlockSpec((SEQ, D_MODEL), lambda s: (cur(s) // A_TILES, 0)),
            pl.BlockSpec((pl.Element(A_OUT_WIDTH), pl.Element(D_MODEL)),
                         lambda s: (pl.multiple_of(w_row(s), SUBLANES), 0)),
            pl.BlockSpec((2, SEQ, LANES), lambda s: (0, 0, 0)),
        ],
        out_specs=pl.BlockSpec((1, 1, SEQ, A_OUT_WIDTH),
                               lambda s: (prev(s) // A_TILES, prev(s) % A_TILES, 0, 0)),
        out_shape=jax.ShapeDtypeStruct((BATCH, A_TILES, SEQ, A_OUT_WIDTH), BF16),
        scratch_shapes=[pltpu.VMEM((A_HEADS, SEQ, A_HEAD_DIM), F32) for _ in range(3)],
        compiler_params=pltpu.CompilerParams(dimension_semantics=("arbitrary",),
                                             vmem_limit_bytes=VMEM_LIMIT),
        name="aproj_d%d" % dil,
    )(h, wt, tabs_g)


def _lane_scan(x, op, fill):
    lane = lax.broadcasted_iota(jnp.int32, x.shape, 1)
    sh = 1
    while sh < x.shape[1]:
        shifted = jnp.where(lane >= sh, pltpu.roll(x, sh, axis=1), fill)
        x = op(x, shifted)
        sh *= 2
    return x


def _gates_kernel(ig_ref, fg_ref, u_ref, g_ref, mt_ref, mp_ref, w_ref, dec_ref):
    L = M_CHUNK
    ig = ig_ref[...]
    fg = fg_ref[...]
    logf = -(jnp.maximum(-fg, 0.0) + jnp.log1p(jnp.exp(-jnp.abs(fg))))
    b = _lane_scan(logf, jnp.add, 0.0)
    u = ig - b
    cm = _lane_scan(u, jnp.maximum, NEG_INF)
    lane = lax.broadcasted_iota(jnp.int32, ig.shape, 1)
    btot = jnp.broadcast_to(jnp.sum(jnp.where(lane == L - 1, b, 0.0), axis=1, keepdims=True), ig.shape)
    umax = jnp.broadcast_to(jnp.max(cm, axis=1, keepdims=True), ig.shape)
    chunk = lax.broadcasted_iota(jnp.int32, ig.shape, 0) % N_CHUNKS
    m = jnp.zeros(ig.shape, F32)
    for c in range(N_CHUNKS - 1):
        nxt = pltpu.roll(btot + jnp.maximum(m, umax), 1, axis=0)
        m = jnp.where(chunk == c + 1, nxt, m)
    g = jnp.maximum(m, cm)
    glast = jnp.max(g, axis=1, keepdims=True)
    u_ref[...] = u * LOG2E
    g_ref[...] = g * LOG2E
    mt_ref[...] = (b + g) * LOG2E
    mp_ref[...] = m * LOG2E
    w_ref[...] = jnp.exp(u - glast)
    dec_ref[...] = jnp.broadcast_to(jnp.exp(m - glast), ig.shape)


def _gates(gt):
    rows = BATCH * M_HEADS * N_CHUNKS
    ig = gt[:, :M_HEADS].reshape(rows, M_CHUNK)
    fg = gt[:, M_HEADS:].reshape(rows, M_CHUNK)
    spec = pl.BlockSpec((rows, M_CHUNK), lambda: (0, 0))
    return pl.pallas_call(
        _gates_kernel,
        in_specs=[spec, spec],
        out_specs=[spec] * 6,
        out_shape=[jax.ShapeDtypeStruct((rows, M_CHUNK), F32)] * 6,
        name="gates",
    )(ig, fg)


def _mlstm_kernel(q_ref, kt_ref, v_ref, o_ref, z_ref, u_ref, g_ref, mt_ref, mp_ref, w_ref, dec_ref,
                  out_ref):
    L = M_CHUNK
    ti = lax.broadcasted_iota(jnp.int32, (L, L), 0)
    si = lax.broadcasted_iota(jnp.int32, (L, L), 1)
    causal = ti >= si
    ones = jnp.ones((L, LANES), BF16)

    state = jnp.zeros((M_HEAD_DIM, M_HEAD_DIM + LANES), F32)
    for c in range(N_CHUNKS):
        rs = slice(c * L, (c + 1) * L)
        cs = slice(c, c + 1)
        qc = q_ref[rs, :]
        v1 = jnp.concatenate([v_ref[rs, :], ones], axis=1)
        ktc = kt_ref[0, 0, c]
        u_r = u_ref[cs, :]
        gcol = jnp.broadcast_to(g_ref[cs, :], (LANES, L)).T
        mcol = jnp.broadcast_to(mt_ref[cs, :], (LANES, L)).T
        dmat = jnp.where(causal, jnp.exp2(u_r - jnp.concatenate([gcol, gcol], axis=1)), 0.0)

        s = jnp.dot(qc, ktc, preferred_element_type=F32) * dmat
        tot = jnp.dot(s.astype(BF16), v1, preferred_element_type=F32)
        if c > 0:
            inter = jnp.exp2(mp_ref[cs, :LANES] - gcol)
            tot = tot + jnp.concatenate([inter, inter, inter], axis=1) \
                * jnp.dot(qc, state.astype(BF16), preferred_element_type=F32)
        den = tot[:, M_HEAD_DIM:]
        rinv = 1.0 / jnp.maximum(jnp.abs(den), jnp.exp2(-mcol))
        hval = tot[:, :M_HEAD_DIM] * jnp.concatenate([rinv, rinv], axis=1)

        hg = hval * o_ref[rs, :].astype(F32)
        ms = jnp.mean(hg * hg, axis=1, keepdims=True)
        hn = hg * lax.rsqrt(ms + NORM_EPS)
        out_ref[rs, :] = (hn * z_ref[rs, :].astype(F32)).astype(BF16)

        if c + 1 < N_CHUNKS:
            ktw = (ktc.astype(F32) * w_ref[cs, :]).astype(BF16)
            dec = dec_ref[cs, :]
            dec = jnp.concatenate([dec, dec[:, :LANES]], axis=1)
            state = dec * state + jnp.dot(ktw, v1, preferred_element_type=F32)


def _mlstm(mall, kt, gate_stats):
    blk = (SEQ, M_HEAD_DIM)

    def seg(k):
        return pl.BlockSpec(blk, lambda b, h, k=k: (b, k * M_HEADS + h))

    stat = pl.BlockSpec((N_CHUNKS, M_CHUNK), lambda b, h: (b * M_HEADS + h, 0))
    return pl.pallas_call(
        _mlstm_kernel,
        grid=(BATCH, M_HEADS),
        in_specs=[
            seg(0),
            pl.BlockSpec((1, 1, N_CHUNKS, M_HEAD_DIM, M_CHUNK), lambda b, h: (b, h, 0, 0, 0)),
            seg(2), seg(3), seg(4),
        ] + [stat] * 6,
        out_specs=pl.BlockSpec(blk, lambda b, h: (b, h)),
        out_shape=jax.ShapeDtypeStruct((BATCH * SEQ, M_WIDTH), BF16),
        compiler_params=pltpu.CompilerParams(dimension_semantics=("parallel", "parallel"),
                                             vmem_limit_bytes=VMEM_LIMIT),
        name="mlstm",
    )(mall, kt, mall, mall, mall, *gate_stats)


GATHERED = (2,)
GATHER_PITCH = A_BLOCK + 8
ATTN_WAVE = 12


def _attn_block_order():
    order = []
    for g in reversed(range(1, N_GROUPS)):
        nblk = SEQ // A_GROUPS[g][1] // A_BLOCK
        order += [(r * nblk + n, g) for n in range(nblk) for r in range(A_GROUPS[g][1])]
    order += [(i, 0) for i in range(SEQ // A_BLOCK)]
    return order


def _attn_kernel(q0, k0, v0, q1, k1, v1, q2, k2, v2, out_ref,
                 a0, a1, a2, m0, m1, m2, l0, l1, l2):
    BLK = A_BLOCK
    ti = lax.broadcasted_iota(jnp.int32, (BLK, BLK), 0)
    ji = lax.broadcasted_iota(jnp.int32, (BLK, BLK), 1)
    cur_ok = ji <= ti
    ti2 = lax.broadcasted_iota(jnp.int32, (BLK, 2 * BLK), 0)
    krel = lax.broadcasted_iota(jnp.int32, (BLK, 2 * BLK), 1) - BLK
    band_ok = jnp.logical_and(krel <= ti2, krel >= ti2 - BLK)

    qs = (q0, q1, q2)
    ks = (k0, k1, k2)
    vs = (v0, v1, v2)
    accs = (a0, a1, a2)
    ms = (m0, m1, m2)
    ls = (l0, l1, l2)

    RB = 256

    def merge(rb):
        sl = slice(rb * RB, (rb + 1) * RB)

        def rows(refs, g):
            if g in GATHERED:
                dil = A_GROUPS[g][1]
                j0 = rb * RB // dil
                return jnp.concatenate([refs[g][pl.ds(j0 + jj, dil, stride=GATHER_PITCH), :]
                                        for jj in range(RB // dil)], axis=0)
            return refs[g][sl, :]

        mg = [rows(ms, g) for g in range(N_GROUPS)]
        mmax = jnp.maximum(jnp.maximum(mg[0], mg[1]), mg[2])
        num = None
        den = None
        for g in range(N_GROUPS):
            e = jnp.exp2(mg[g] - mmax)
            num = e * rows(accs, g) if num is None else num + e * rows(accs, g)
            den = e * rows(ls, g) if den is None else den + e * rows(ls, g)
        out_ref[sl, :] = (num * (1.0 / den)).astype(BF16)

    blocks = _attn_block_order()
    done_rows = [set() for _ in range(N_GROUPS)]
    merged = 0
    for w0 in range(0, len(blocks), ATTN_WAVE):
        wave = []
        for i, g in blocks[w0:w0 + ATTN_WAVE]:
            dil = A_GROUPS[g][1]
            r, n = divmod(i, SEQ // dil // BLK)
            rows = slice(i * BLK, (i + 1) * BLK)
            keys = slice((i - 1) * BLK, (i + 1) * BLK) if n > 0 else rows
            valid = band_ok if n > 0 else cur_ok
            s = lax.dot_general(qs[g][0, 0, rows, :], ks[g][0, 0, keys, :], (((1,), (1,)), ((), ())),
                                preferred_element_type=F32)
            s = jnp.where(valid, s, NEG_INF)
            m = jnp.max(s, axis=1, keepdims=True)
            tok0 = n * BLK * dil + r
            if g in GATHERED:
                idx = pl.ds(r * GATHER_PITCH, BLK)
            else:
                idx = pl.ds(tok0, BLK, stride=dil) if dil > 1 else pl.ds(tok0, BLK)
            wave.append((g, keys, valid, idx, s, m))
        for g, keys, valid, idx, s, m in wave:
            p = jnp.where(valid, jnp.exp2(s - m), 0.0)
            nkeys = keys.stop - keys.start
            v1 = jnp.concatenate([vs[g][0, 0, keys, :], jnp.ones((nkeys, LANES), BF16)], axis=1)
            acc = jnp.dot(p.astype(BF16), v1, preferred_element_type=F32)
            accs[g][idx, :] = acc[:, :LANES]
            ms[g][idx, :] = jnp.broadcast_to(m, (BLK, LANES))
            ls[g][idx, :] = acc[:, LANES:]
        for i, g in blocks[w0:w0 + ATTN_WAVE]:
            dil = A_GROUPS[g][1]
            r, n = divmod(i, SEQ // dil // BLK)
            span = BLK * dil
            if r == dil - 1:
                done_rows[g].update(range(n * span // RB, (n + 1) * span // RB))
        while merged < SEQ // RB and all(merged in d for d in done_rows):
            merge(merged)
            merged += 1


def _attn(qkv):
    in_specs = []
    args = []
    for g in range(N_GROUPS):
        for t in range(3):
            in_specs.append(pl.BlockSpec((1, 1, SEQ, A_HEAD_DIM), lambda b, h, t=t: (b, t, 0, h)))
            args.append(qkv[g])
    return pl.pallas_call(
        _attn_kernel,
        grid=(BATCH, A_HEADS),
        in_specs=in_specs,
        out_specs=pl.BlockSpec((SEQ, A_HEAD_DIM), lambda b, h: (b, h)),
        out_shape=jax.ShapeDtypeStruct((BATCH * SEQ, A_OUT_WIDTH), BF16),
        scratch_shapes=[pltpu.VMEM((A_GROUPS[g][1] * GATHER_PITCH if g in GATHERED else SEQ, LANES), F32)
                        for _ in range(3) for g in range(N_GROUPS)],
        compiler_params=pltpu.CompilerParams(dimension_semantics=("parallel", "parallel"),
                                             vmem_limit_bytes=VMEM_LIMIT),
        name="attn",
    )(*args)


O_TM = 1024


def _out_kernel(x_ref, prew_ref, hm_ref, att_ref, wg_ref, wpm_ref, wpa_ref, wout_ref, postw_ref,
                y_ref, wgb_ref):
    @pl.when(pl.program_id(0) == 0)
    def _():
        wgb_ref[...] = wg_ref[...].astype(BF16)

    x = x_ref[...]
    ms = jnp.mean(x * x, axis=-1, keepdims=True)
    h = (x * lax.rsqrt(ms + NORM_EPS) * prew_ref[...]).astype(BF16)
    gates = _dot_nt(h, wgb_ref[...])
    za = _silu(gates[:, :A_OUT_WIDTH])
    sgm = _sigmoid(gates[:, A_OUT_WIDTH:A_OUT_WIDTH + D_MODEL])
    sga = _sigmoid(gates[:, A_OUT_WIDTH + D_MODEL:])
    a = (att_ref[...].astype(F32) * za).astype(BF16)
    bm = jnp.dot(hm_ref[...], wpm_ref[...], preferred_element_type=F32)
    ba = jnp.dot(a, wpa_ref[...], preferred_element_type=F32)
    merged = (sgm * bm + sga * ba).astype(BF16)
    y = jnp.dot(merged, wout_ref[...], preferred_element_type=F32)
    ms2 = jnp.mean(y * y, axis=-1, keepdims=True)
    y_ref[...] = x + y * lax.rsqrt(ms2 + NORM_EPS) * postw_ref[...]


def _out(x2d, pre_w, hm, att, wt, wpm, wpa, wout, post_w):
    n = x2d.shape[0]
    const = lambda i: (0, 0)
    return pl.pallas_call(
        _out_kernel,
        grid=(n // O_TM,),
        in_specs=[
            pl.BlockSpec((O_TM, D_MODEL), lambda i: (i, 0)),
            pl.BlockSpec((1, D_MODEL), const),
            pl.BlockSpec((O_TM, M_WIDTH), lambda i: (i, 0)),
            pl.BlockSpec((O_TM, A_OUT_WIDTH), lambda i: (i, 0)),
            pl.BlockSpec((pl.Element(IN_WIDTH - OFF_ZA), pl.Element(D_MODEL)), lambda i: (OFF_ZA, 0),
                         pipeline_mode=pl.Buffered(1)),
            pl.BlockSpec((M_WIDTH, D_MODEL), const),
            pl.BlockSpec((A_OUT_WIDTH, D_MODEL), const),
            pl.BlockSpec((D_MODEL, D_MODEL), const),
            pl.BlockSpec((1, D_MODEL), const),
        ],
        out_specs=pl.BlockSpec((O_TM, D_MODEL), lambda i: (i, 0)),
        out_shape=jax.ShapeDtypeStruct((n, D_MODEL), F32),
        scratch_shapes=[pltpu.VMEM((IN_WIDTH - OFF_ZA, D_MODEL), BF16)],
        compiler_params=pltpu.CompilerParams(dimension_semantics=("arbitrary",),
                                             vmem_limit_bytes=VMEM_LIMIT),
        name="outproj",
    )(x2d, pre_w.reshape(1, D_MODEL), hm, att, wt, wpm, wpa, wout, post_w.reshape(1, D_MODEL))


def _rope_tables():
    pos = np.arange(SEQ, dtype=np.float64)
    inv_freq = ROPE_THETA ** (-np.arange(0, ROPE_DIM, 2, dtype=np.float64) / ROPE_DIM)
    ang = pos[:, None] * inv_freq[None, :]
    cos, sin = np.cos(ang), np.sin(ang)
    half = ROPE_DIM // 2
    cos_t = np.ones((SEQ, A_HEAD_DIM))
    sin_t = np.zeros((SEQ, A_HEAD_DIM))
    cos_t[:, :half] = cos
    cos_t[:, ROT_LANE:ROT_LANE + half] = cos
    sin_t[:, :half] = -sin
    sin_t[:, ROT_LANE:ROT_LANE + half] = sin
    tab = np.stack([cos_t, sin_t]).astype(np.float32)
    out = []
    for _, dil in A_GROUPS:
        ls = SEQ // dil
        out.append(tab.reshape(2, ls, dil, A_HEAD_DIM).transpose(0, 2, 1, 3).reshape(2, SEQ, A_HEAD_DIM))
    return jnp.asarray(np.stack(out))


def _layer(x, pre_w, w_in, b_if, conv_w, conv_b, m_norm_w, w_pm, w_pa, w_out, post_w):
    x2d = x.reshape(BATCH * SEQ, D_MODEL)
    wt = w_in.T
    wg = jnp.pad(wt[OFF_GATES:OFF_QA], ((0, LANES - 2 * M_HEADS), (0, 0))).astype(BF16)
    tabs = _rope_tables()

    mall, kt, gt, h = _mproj(x2d, pre_w, wt, wg, b_if.reshape(2 * M_HEADS, 1), conv_w,
                             conv_b.reshape(1, 2 * M_WIDTH))
    qkv = [_aproj(h, wt, tabs[g], g) for g in range(N_GROUPS)]
    hm = _mlstm(mall, kt, _gates(gt))
    att = _attn(qkv)
    w_pm_n = (m_norm_w[:, None] * w_pm).astype(BF16)
    y = _out(x2d, pre_w, hm, att, wt, w_pm_n, w_pa.astype(BF16), w_out.astype(BF16), post_w)
    return y.reshape(BATCH, SEQ, D_MODEL)


@jax.jit
def kernel(x, pre_norm_w, w_in, b_if, conv_w, conv_b, m_norm_w, w_proj_m, w_proj_a, w_out, post_norm_w):
    for layer in range(pre_norm_w.shape[0]):
        x = _layer(x, pre_norm_w[layer], w_in[layer], b_if[layer], conv_w[layer], conv_b[layer],
                   m_norm_w[layer], w_proj_m[layer], w_proj_a[layer], w_out[layer], post_norm_w[layer])
    return x
```

```python
import functools

import jax
import jax.numpy as jnp
import numpy as np
from jax import lax
from jax.experimental import pallas as pl
from jax.experimental.pallas import tpu as pltpu

D_MODEL = 1024
BATCH = 8
SEQ = 2048
M_WIDTH = 1024
M_HEADS = 4
M_HEAD_DIM = 256
CONV_WIDTH = 4
A_GROUPS = ((128, 1), (512, 4), (2048, 16))
N_GROUPS = 3
A_HEADS = 4
A_HEAD_DIM = 128
A_OUT_WIDTH = 512
A_QKV_WIDTH = 1536
A_BLOCK = 128
ROPE_DIM = 32
ROPE_THETA = 500000.0
NORM_EPS = 1e-6
NEG_INF = -1e30
LOG2E = 1.4426950408889634

OFF_GATES = 5 * M_WIDTH
OFF_QA = OFF_GATES + 2 * M_HEADS
OFF_KA = OFF_QA + A_QKV_WIDTH
OFF_VA = OFF_KA + A_QKV_WIDTH
OFF_ZA = OFF_VA + A_QKV_WIDTH
OFF_GM = OFF_ZA + A_OUT_WIDTH
IN_WIDTH = OFF_GM + 2 * D_MODEL

LANES = 128
M_CHUNK = 256
N_CHUNKS = SEQ // M_CHUNK
V7X_VMEM_BYTES = 64 * 1024 * 1024
VMEM_LIMIT = V7X_VMEM_BYTES * 3 // 4

F32 = jnp.float32
BF16 = jnp.bfloat16


def _dot_nt(a, bt):
    return lax.dot_general(a, bt, (((1,), (1,)), ((), ())), preferred_element_type=F32)


def _sigmoid(x):
    return 1.0 / (1.0 + jnp.exp(-x))


def _silu(x):
    return x * _sigmoid(x)


M_TN = 512


SUBLANES = 8


def _shift_rows(x3, s, sub):
    rot = pltpu.roll(x3, s, axis=1)
    prev = jnp.concatenate([jnp.zeros_like(rot[:1]), rot[:-1]], axis=0)
    return jnp.where(sub < s, prev, rot)


def _causal_conv(acc, cw, cb):
    rows, cols = acc.shape
    x3 = acc.reshape(rows // SUBLANES, SUBLANES, cols)
    sub = lax.broadcasted_iota(jnp.int32, (1, SUBLANES, cols), 1)
    d1 = _shift_rows(x3, 1, sub)
    near = x3 * cw[3] + d1 * cw[2] + cb[0]
    far = x3 * cw[1] + d1 * cw[0]
    return (near + _shift_rows(far, 2, sub)).reshape(rows, cols)


M_TILES = 5 * M_WIDTH // M_TN


def _mproj_kernel(x_ref, prew_ref, w_ref, wg_ref, bif_ref, cw_ref, cb_ref,
                  mall_ref, kt_ref, gt_ref, h_ref):
    j = pl.program_id(1)

    def matmul():
        return _dot_nt(h_ref[...], w_ref[...].astype(BF16))

    @pl.when(j == 0)
    def _():
        x = x_ref[...]
        ms = jnp.mean(x * x, axis=-1, keepdims=True)
        h_ref[...] = (x * lax.rsqrt(ms + NORM_EPS) * prew_ref[...]).astype(BF16)
        g = _dot_nt(h_ref[...], wg_ref[...])
        gt = g.T[:2 * M_HEADS, :] + bif_ref[...]
        for c in range(N_CHUNKS):
            gt_ref[0, :, c, :] = gt[:, c * M_CHUNK:(c + 1) * M_CHUNK]

    @pl.when(j < 2)
    def _():
        y = _causal_conv(matmul(), cw_ref[...], cb_ref[...])
        mall_ref[...] = _silu(y).astype(BF16)

    @pl.when(jnp.logical_and(j >= 2, j < 4))
    def _():
        y = _causal_conv(matmul(), cw_ref[...], cb_ref[...])
        y = _silu(y) * (M_HEAD_DIM ** -0.5)
        mall_ref[...] = y.astype(BF16)
        for hh in range(M_TN // M_HEAD_DIM):
            for c in range(N_CHUNKS):
                blk = y[c * M_CHUNK:(c + 1) * M_CHUNK, hh * M_HEAD_DIM:(hh + 1) * M_HEAD_DIM]
                kt_ref[0, hh, c] = blk.T.astype(BF16)

    @pl.when(jnp.logical_and(j >= 4, j < 6))
    def _():
        mall_ref[...] = matmul().astype(BF16)

    @pl.when(jnp.logical_and(j >= 6, j < 8))
    def _():
        mall_ref[...] = _sigmoid(matmul()).astype(BF16)

    @pl.when(j >= 8)
    def _():
        mall_ref[...] = _silu(matmul()).astype(BF16)


def _mproj(x2d, pre_w, wt, wg, bif, conv_w, conv_b):
    return pl.pallas_call(
        _mproj_kernel,
        grid=(BATCH, M_TILES),
        in_specs=[
            pl.BlockSpec((SEQ, D_MODEL), lambda i, j: (i, 0)),
            pl.BlockSpec((1, D_MODEL), lambda i, j: (0, 0)),
            pl.BlockSpec((M_TN, D_MODEL), lambda i, j: (j, 0)),
            pl.BlockSpec((LANES, D_MODEL), lambda i, j: (0, 0)),
            pl.BlockSpec((2 * M_HEADS, 1), lambda i, j: (0, 0)),
            pl.BlockSpec((CONV_WIDTH, M_TN), lambda i, j: (0, jnp.minimum(j, 3))),
            pl.BlockSpec((1, M_TN), lambda i, j: (0, jnp.minimum(j, 3))),
        ],
        out_specs=[
            pl.BlockSpec((SEQ, M_TN), lambda i, j: (i, j)),
            pl.BlockSpec((1, M_TN // M_HEAD_DIM, N_CHUNKS, M_HEAD_DIM, M_CHUNK),
                         lambda i, j: (i, jnp.clip(j - 2, 0, 1), 0, 0, 0)),
            pl.BlockSpec((1, 2 * M_HEADS, N_CHUNKS, M_CHUNK), lambda i, j: (i, 0, 0, 0)),
            pl.BlockSpec((SEQ, D_MODEL), lambda i, j: (i, 0)),
        ],
        out_shape=[
            jax.ShapeDtypeStruct((BATCH * SEQ, 5 * M_WIDTH), BF16),
            jax.ShapeDtypeStruct((BATCH, M_HEADS, N_CHUNKS, M_HEAD_DIM, M_CHUNK), BF16),
            jax.ShapeDtypeStruct((BATCH, 2 * M_HEADS, N_CHUNKS, M_CHUNK), F32),
            jax.ShapeDtypeStruct((BATCH * SEQ, D_MODEL), BF16),
        ],
        compiler_params=pltpu.CompilerParams(dimension_semantics=("parallel", "arbitrary"),
                                             vmem_limit_bytes=VMEM_LIMIT),
        name="mproj",
    )(x2d, pre_w.reshape(1, D_MODEL), wt, wg, bif, conv_w, conv_b)


A_TILES = 3
EASY_STRIDE = 4


def _aproj_kernel(dil, h_ref, w_ref, tab_ref, o_ref, acc0_ref, acc1_ref, tmp_ref):
    s = pl.program_id(0)
    last = BATCH * A_TILES
    pt = (jnp.maximum(s, 1) - 1) % A_TILES

    accs = (acc0_ref, acc1_ref)

    def matmul(slot):
        res = _dot_nt(h_ref[...], w_ref[...].astype(BF16))
        for hd in range(A_HEADS):
            accs[slot][hd] = res[:, hd * A_HEAD_DIM:(hd + 1) * A_HEAD_DIM]

    ls = SEQ // dil

    two_pass = dil == EASY_STRIDE * EASY_STRIDE

    def stage(slot, hd):
        if two_pass:
            n4 = SEQ // EASY_STRIDE
            for r4 in range(EASY_STRIDE):
                tmp_ref[hd, r4 * n4:(r4 + 1) * n4, :] = accs[slot][hd, pl.ds(r4, n4, stride=EASY_STRIDE), :]

    def residue_rows(slot, hd, r):
        if dil == 1:
            return accs[slot][hd]
        if two_pass:
            q, r4 = divmod(r, EASY_STRIDE)
            return tmp_ref[hd, pl.ds(r4 * (SEQ // EASY_STRIDE) + q, ls, stride=EASY_STRIDE), :]
        return accs[slot][hd, pl.ds(r, ls, stride=dil), :]

    def rope_epilogue(slot):
        scale = jnp.where(pt == 0, A_HEAD_DIM ** -0.5 * LOG2E, 1.0).astype(F32)
        half = ROPE_DIM // 2
        for hd in range(A_HEADS):
            stage(slot, hd)
            for r in range(dil):
                rows = slice(r * ls, (r + 1) * ls)
                xh = residue_rows(slot, hd, r)
                y = (xh * tab_ref[0, rows, :] + pltpu.roll(xh, half, axis=1) * tab_ref[1, rows, :]
                     + pltpu.roll(xh, A_HEAD_DIM - half, axis=1) * tab_ref[2, rows, :])
                o_ref[0, 0, rows, hd * A_HEAD_DIM:(hd + 1) * A_HEAD_DIM] = (y * scale).astype(BF16)

    def cast_epilogue(slot):
        for hd in range(A_HEADS):
            stage(slot, hd)
            for r in range(dil):
                rows = slice(r * ls, (r + 1) * ls)
                o_ref[0, 0, rows, hd * A_HEAD_DIM:(hd + 1) * A_HEAD_DIM] = \
                    residue_rows(slot, hd, r).astype(BF16)

    mid = jnp.logical_and(s > 0, s < last)

    @pl.when(s == 0)
    def _():
        matmul(0)

    for slot in range(2):
        mine = jnp.logical_and(mid, s % 2 == slot)

        @pl.when(jnp.logical_and(mine, pt < 2))
        def _(slot=slot):
            rope_epilogue(1 - slot)
            matmul(slot)

        @pl.when(jnp.logical_and(mine, pt == 2))
        def _(slot=slot):
            cast_epilogue(1 - slot)
            matmul(slot)

    @pl.when(s == last)
    def _():
        cast_epilogue((last - 1) % 2)


def _aproj(h, wt, tabs_g, g):
    dil = A_GROUPS[g][1]

    def w_row(s):
        return OFF_QA + (cur(s) % A_TILES) * A_QKV_WIDTH + g * A_OUT_WIDTH
    last = BATCH * A_TILES

    def cur(s):
        return jnp.minimum(s, last - 1)

    def prev(s):
        return jnp.maximum(s, 1) - 1

    return pl.pallas_call(
        functools.partial(_aproj_kernel, dil),
        grid=(last + 1,),
        in_specs=[
            pl.BlockSpec((SEQ, D_MODEL), lambda s: (cur(s) // A_TILES, 0)),
            pl.BlockSpec((pl.Element(A_OUT_WIDTH), pl.Element(D_MODEL)),
                         lambda s: (pl.multiple_of(w_row(s), SUBLANES), 0)),
            pl.BlockSpec((3, SEQ, LANES), lambda s: (0, 0, 0)),
        ],
        out_specs=pl.BlockSpec((1, 1, SEQ, A_OUT_WIDTH),
                               lambda s: (prev(s) // A_TILES, prev(s) % A_TILES, 0, 0)),
        out_shape=jax.ShapeDtypeStruct((BATCH, A_TILES, SEQ, A_OUT_WIDTH), BF16),
        scratch_shapes=[pltpu.VMEM((A_HEADS, SEQ, A_HEAD_DIM), F32) for _ in range(3)],
        compiler_params=pltpu.CompilerParams(dimension_semantics=("arbitrary",),
                                             vmem_limit_bytes=VMEM_LIMIT),
        name="aproj_d%d" % dil,
    )(h, wt, tabs_g)


def _lane_scan(x, op, fill):
    lane = lax.broadcasted_iota(jnp.int32, x.shape, 1)
    sh = 1
    while sh < x.shape[1]:
        shifted = jnp.where(lane >= sh, pltpu.roll(x, sh, axis=1), fill)
        x = op(x, shifted)
        sh *= 2
    return x


def _gates_kernel(ig_ref, fg_ref, u_ref, g_ref, mt_ref, mp_ref, w_ref, dec_ref):
    L = M_CHUNK
    ig = ig_ref[...]
    fg = fg_ref[...]
    logf = -(jnp.maximum(-fg, 0.0) + jnp.log1p(jnp.exp(-jnp.abs(fg))))
    b = _lane_scan(logf, jnp.add, 0.0)
    u = ig - b
    cm = _lane_scan(u, jnp.maximum, NEG_INF)
    lane = lax.broadcasted_iota(jnp.int32, ig.shape, 1)
    btot = jnp.broadcast_to(jnp.sum(jnp.where(lane == L - 1, b, 0.0), axis=1, keepdims=True), ig.shape)
    umax = jnp.broadcast_to(jnp.max(cm, axis=1, keepdims=True), ig.shape)
    chunk = lax.broadcasted_iota(jnp.int32, ig.shape, 0) % N_CHUNKS
    m = jnp.zeros(ig.shape, F32)
    for c in range(N_CHUNKS - 1):
        nxt = pltpu.roll(btot + jnp.maximum(m, umax), 1, axis=0)
        m = jnp.where(chunk == c + 1, nxt, m)
    g = jnp.maximum(m, cm)
    glast = jnp.max(g, axis=1, keepdims=True)
    u_ref[...] = u * LOG2E
    g_ref[...] = g * LOG2E
    mt_ref[...] = (b + g) * LOG2E
    mp_ref[...] = m * LOG2E
    w_ref[...] = jnp.exp(u - glast)
    dec_ref[...] = jnp.broadcast_to(jnp.exp(m - glast), ig.shape)


def _gates(gt):
    rows = BATCH * M_HEADS * N_CHUNKS
    ig = gt[:, :M_HEADS].reshape(rows, M_CHUNK)
    fg = gt[:, M_HEADS:].reshape(rows, M_CHUNK)
    spec = pl.BlockSpec((rows, M_CHUNK), lambda: (0, 0))
    return pl.pallas_call(
        _gates_kernel,
        in_specs=[spec, spec],
        out_specs=[spec] * 6,
        out_shape=[jax.ShapeDtypeStruct((rows, M_CHUNK), F32)] * 6,
        name="gates",
    )(ig, fg)


def _mlstm_kernel(q_ref, kt_ref, v_ref, o_ref, z_ref, u_ref, g_ref, mt_ref, mp_ref, w_ref, dec_ref,
                  out_ref):
    L = M_CHUNK
    ti = lax.broadcasted_iota(jnp.int32, (L, L), 0)
    si = lax.broadcasted_iota(jnp.int32, (L, L), 1)
    causal = ti >= si
    ones = jnp.ones((L, LANES), BF16)

    state = jnp.zeros((M_HEAD_DIM, M_HEAD_DIM + LANES), F32)
    for c in range(N_CHUNKS):
        rs = slice(c * L, (c + 1) * L)
        cs = slice(c, c + 1)
        qc = q_ref[rs, :]
        v1 = jnp.concatenate([v_ref[rs, :], ones], axis=1)
        ktc = kt_ref[0, 0, c]
        u_r = u_ref[cs, :]
        gcol = jnp.broadcast_to(g_ref[cs, :], (LANES, L)).T
        mcol = jnp.broadcast_to(mt_ref[cs, :], (LANES, L)).T
        dmat = jnp.where(causal, jnp.exp2(u_r - jnp.concatenate([gcol, gcol], axis=1)), 0.0)

        s = jnp.dot(qc, ktc, preferred_element_type=F32) * dmat
        tot = jnp.dot(s.astype(BF16), v1, preferred_element_type=F32)
        if c > 0:
            inter = jnp.exp2(mp_ref[cs, :LANES] - gcol)
            tot = tot + jnp.concatenate([inter, inter, inter], axis=1) \
                * jnp.dot(qc, state.astype(BF16), preferred_element_type=F32)
        den = tot[:, M_HEAD_DIM:]
        rinv = 1.0 / jnp.maximum(jnp.abs(den), jnp.exp2(-mcol))
        hval = tot[:, :M_HEAD_DIM] * jnp.concatenate([rinv, rinv], axis=1)

        hg = hval * o_ref[rs, :].astype(F32)
        ms = jnp.mean(hg * hg, axis=1, keepdims=True)
        hn = hg * lax.rsqrt(ms + NORM_EPS)
        out_ref[rs, :] = (hn * z_ref[rs, :].astype(F32)).astype(BF16)

        if c + 1 < N_CHUNKS:
            ktw = (ktc.astype(F32) * w_ref[cs, :]).astype(BF16)
            dec = dec_ref[cs, :]
            dec = jnp.concatenate([dec, dec[:, :LANES]], axis=1)
            state = dec * state + jnp.dot(ktw, v1, preferred_element_type=F32)


def _mlstm(mall, kt, gate_stats):
    blk = (SEQ, M_HEAD_DIM)

    def seg(k):
        return pl.BlockSpec(blk, lambda b, h, k=k: (b, k * M_HEADS + h))

    stat = pl.BlockSpec((N_CHUNKS, M_CHUNK), lambda b, h: (b * M_HEADS + h, 0))
    return pl.pallas_call(
        _mlstm_kernel,
        grid=(BATCH, M_HEADS),
        in_specs=[
            seg(0),
            pl.BlockSpec((1, 1, N_CHUNKS, M_HEAD_DIM, M_CHUNK), lambda b, h: (b, h, 0, 0, 0)),
            seg(2), seg(3), seg(4),
        ] + [stat] * 6,
        out_specs=pl.BlockSpec(blk, lambda b, h: (b, h)),
        out_shape=jax.ShapeDtypeStruct((BATCH * SEQ, M_WIDTH), BF16),
        compiler_params=pltpu.CompilerParams(dimension_semantics=("parallel", "parallel"),
                                             vmem_limit_bytes=VMEM_LIMIT),
        name="mlstm",
    )(mall, kt, mall, mall, mall, *gate_stats)


GATHERED = (2,)
GATHER_PITCH = A_BLOCK + 8
ATTN_WAVE = 12


def _attn_block_order():
    order = []
    for g in reversed(range(1, N_GROUPS)):
        nblk = SEQ // A_GROUPS[g][1] // A_BLOCK
        order += [(r * nblk + n, g) for n in range(nblk) for r in range(A_GROUPS[g][1])]
    order += [(i, 0) for i in range(SEQ // A_BLOCK)]
    return order


def _attn_kernel(q0, k0, v0, q1, k1, v1, q2, k2, v2, out_ref,
                 a0, a1, a2, m0, m1, m2, l0, l1, l2):
    BLK = A_BLOCK
    ti = lax.broadcasted_iota(jnp.int32, (BLK, BLK), 0)
    ji = lax.broadcasted_iota(jnp.int32, (BLK, BLK), 1)
    cur_ok = ji <= ti
    ti2 = lax.broadcasted_iota(jnp.int32, (BLK, 2 * BLK), 0)
    krel = lax.broadcasted_iota(jnp.int32, (BLK, 2 * BLK), 1) - BLK
    band_ok = jnp.logical_and(krel <= ti2, krel >= ti2 - BLK)

    qs = (q0, q1, q2)
    ks = (k0, k1, k2)
    vs = (v0, v1, v2)
    accs = (a0, a1, a2)
    ms = (m0, m1, m2)
    ls = (l0, l1, l2)

    RB = 256

    def merge(rb):
        sl = slice(rb * RB, (rb + 1) * RB)

        def rows(refs, g):
            if g in GATHERED:
                dil = A_GROUPS[g][1]
                j0 = rb * RB // dil
                return jnp.concatenate([refs[g][pl.ds(j0 + jj, dil, stride=GATHER_PITCH), :]
                                        for jj in range(RB // dil)], axis=0)
            return refs[g][sl, :]

        mg = [rows(ms, g) for g in range(N_GROUPS)]
        mmax = jnp.maximum(jnp.maximum(mg[0], mg[1]), mg[2])
        num = None
        den = None
        for g in range(N_GROUPS):
            e = jnp.exp2(mg[g] - mmax)
            num = e * rows(accs, g) if num is None else num + e * rows(accs, g)
            den = e * rows(ls, g) if den is None else den + e * rows(ls, g)
        out_ref[sl, :] = (num * (1.0 / den)).astype(BF16)

    blocks = _attn_block_order()
    done_rows = [set() for _ in range(N_GROUPS)]
    merged = 0
    for w0 in range(0, len(blocks), ATTN_WAVE):
        wave = []
        for i, g in blocks[w0:w0 + ATTN_WAVE]:
            dil = A_GROUPS[g][1]
            r, n = divmod(i, SEQ // dil // BLK)
            rows = slice(i * BLK, (i + 1) * BLK)
            keys = slice((i - 1) * BLK, (i + 1) * BLK) if n > 0 else rows
            valid = band_ok if n > 0 else cur_ok
            s = lax.dot_general(qs[g][0, 0, rows, :], ks[g][0, 0, keys, :], (((1,), (1,)), ((), ())),
                                preferred_element_type=F32)
            s = jnp.where(valid, s, NEG_INF)
            m = jnp.max(s, axis=1, keepdims=True)
            tok0 = n * BLK * dil + r
            if g in GATHERED:
                idx = pl.ds(r * GATHER_PITCH, BLK)
            else:
                idx = pl.ds(tok0, BLK, stride=dil) if dil > 1 else pl.ds(tok0, BLK)
            wave.append((g, keys, valid, idx, s, m))
        for g, keys, valid, idx, s, m in wave:
            p = jnp.where(valid, jnp.exp2(s - m), 0.0)
            nkeys = keys.stop - keys.start
            v1 = jnp.concatenate([vs[g][0, 0, keys, :], jnp.ones((nkeys, LANES), BF16)], axis=1)
            acc = jnp.dot(p.astype(BF16), v1, preferred_element_type=F32)
            accs[g][idx, :] = acc[:, :LANES]
            ms[g][idx, :] = jnp.broadcast_to(m, (BLK, LANES))
            ls[g][idx, :] = acc[:, LANES:]
        for i, g in blocks[w0:w0 + ATTN_WAVE]:
            dil = A_GROUPS[g][1]
            r, n = divmod(i, SEQ // dil // BLK)
            span = BLK * dil
            if r == dil - 1:
                done_rows[g].update(range(n * span // RB, (n + 1) * span // RB))
        while merged < SEQ // RB and all(merged in d for d in done_rows):
            merge(merged)
            merged += 1


def _attn(qkv):
    in_specs = []
    args = []
    for g in range(N_GROUPS):
        for t in range(3):
            in_specs.append(pl.BlockSpec((1, 1, SEQ, A_HEAD_DIM), lambda b, h, t=t: (b, t, 0, h)))
            args.append(qkv[g])
    return pl.pallas_call(
        _attn_kernel,
        grid=(BATCH, A_HEADS),
        in_specs=in_specs,
        out_specs=pl.BlockSpec((SEQ, A_HEAD_DIM), lambda b, h: (b, h)),
        out_shape=jax.ShapeDtypeStruct((BATCH * SEQ, A_OUT_WIDTH), BF16),
        scratch_shapes=[pltpu.VMEM((A_GROUPS[g][1] * GATHER_PITCH if g in GATHERED else SEQ, LANES), F32)
                        for _ in range(3) for g in range(N_GROUPS)],
        compiler_params=pltpu.CompilerParams(dimension_semantics=("parallel", "parallel"),
                                             vmem_limit_bytes=VMEM_LIMIT),
        name="attn",
    )(*args)


O_TM = 1024


def _out_kernel(x_ref, prew_ref, hm_ref, att_ref, wg_ref, wpm_ref, wpa_ref, wout_ref, postw_ref,
                y_ref, wgb_ref):
    @pl.when(pl.program_id(0) == 0)
    def _():
        wgb_ref[...] = wg_ref[...].astype(BF16)

    x = x_ref[...]
    ms = jnp.mean(x * x, axis=-1, keepdims=True)
    h = (x * lax.rsqrt(ms + NORM_EPS) * prew_ref[...]).astype(BF16)
    gates = _dot_nt(h, wgb_ref[...])
    za = _silu(gates[:, :A_OUT_WIDTH])
    sgm = _sigmoid(gates[:, A_OUT_WIDTH:A_OUT_WIDTH + D_MODEL])
    sga = _sigmoid(gates[:, A_OUT_WIDTH + D_MODEL:])
    a = (att_ref[...].astype(F32) * za).astype(BF16)
    bm = jnp.dot(hm_ref[...], wpm_ref[...], preferred_element_type=F32)
    ba = jnp.dot(a, wpa_ref[...], preferred_element_type=F32)
    merged = (sgm * bm + sga * ba).astype(BF16)
    y = jnp.dot(merged, wout_ref[...], preferred_element_type=F32)
    ms2 = jnp.mean(y * y, axis=-1, keepdims=True)
    y_ref[...] = x + y * lax.rsqrt(ms2 + NORM_EPS) * postw_ref[...]


def _out(x2d, pre_w, hm, att, wt, wpm, wpa, wout, post_w):
    n = x2d.shape[0]
    const = lambda i: (0, 0)
    return pl.pallas_call(
        _out_kernel,
        grid=(n // O_TM,),
        in_specs=[
            pl.BlockSpec((O_TM, D_MODEL), lambda i: (i, 0)),
            pl.BlockSpec((1, D_MODEL), const),
            pl.BlockSpec((O_TM, M_WIDTH), lambda i: (i, 0)),
            pl.BlockSpec((O_TM, A_OUT_WIDTH), lambda i: (i, 0)),
            pl.BlockSpec((pl.Element(IN_WIDTH - OFF_ZA), pl.Element(D_MODEL)), lambda i: (OFF_ZA, 0),
                         pipeline_mode=pl.Buffered(1)),
            pl.BlockSpec((M_WIDTH, D_MODEL), const),
            pl.BlockSpec((A_OUT_WIDTH, D_MODEL), const),
            pl.BlockSpec((D_MODEL, D_MODEL), const),
            pl.BlockSpec((1, D_MODEL), const),
        ],
        out_specs=pl.BlockSpec((O_TM, D_MODEL), lambda i: (i, 0)),
        out_shape=jax.ShapeDtypeStruct((n, D_MODEL), F32),
        scratch_shapes=[pltpu.VMEM((IN_WIDTH - OFF_ZA, D_MODEL), BF16)],
        compiler_params=pltpu.CompilerParams(dimension_semantics=("arbitrary",),
                                             vmem_limit_bytes=VMEM_LIMIT),
        name="outproj",
    )(x2d, pre_w.reshape(1, D_MODEL), hm, att, wt, wpm, wpa, wout, post_w.reshape(1, D_MODEL))


def _rope_tables():
    pos = np.arange(SEQ, dtype=np.float64)
    inv_freq = ROPE_THETA ** (-np.arange(0, ROPE_DIM, 2, dtype=np.float64) / ROPE_DIM)
    ang = pos[:, None] * inv_freq[None, :]
    cos, sin = np.cos(ang), np.sin(ang)
    half = ROPE_DIM // 2
    pad = A_HEAD_DIM - ROPE_DIM
    cos_t = np.concatenate([cos, cos, np.ones((SEQ, pad))], axis=1)
    sin_up = np.concatenate([np.zeros((SEQ, half)), sin, np.zeros((SEQ, pad))], axis=1)
    sin_dn = np.concatenate([-sin, np.zeros((SEQ, half + pad))], axis=1)
    tab = np.stack([cos_t, sin_up, sin_dn]).astype(np.float32)
    out = []
    for _, dil in A_GROUPS:
        ls = SEQ // dil
        out.append(tab.reshape(3, ls, dil, A_HEAD_DIM).transpose(0, 2, 1, 3).reshape(3, SEQ, A_HEAD_DIM))
    return jnp.asarray(np.stack(out))


def _layer(x, pre_w, w_in, b_if, conv_w, conv_b, m_norm_w, w_pm, w_pa, w_out, post_w):
    x2d = x.reshape(BATCH * SEQ, D_MODEL)
    wt = w_in.T
    wg = jnp.pad(wt[OFF_GATES:OFF_QA], ((0, LANES - 2 * M_HEADS), (0, 0))).astype(BF16)
    tabs = _rope_tables()

    mall, kt, gt, h = _mproj(x2d, pre_w, wt, wg, b_if.reshape(2 * M_HEADS, 1), conv_w,
                             conv_b.reshape(1, 2 * M_WIDTH))
    qkv = [_aproj(h, wt, tabs[g], g) for g in range(N_GROUPS)]
    hm = _mlstm(mall, kt, _gates(gt))
    att = _attn(qkv)
    w_pm_n = (m_norm_w[:, None] * w_pm).astype(BF16)
    y = _out(x2d, pre_w, hm, att, wt, w_pm_n, w_pa.astype(BF16), w_out.astype(BF16), post_w)
    return y.reshape(BATCH, SEQ, D_MODEL)


@jax.jit
def kernel(x, pre_norm_w, w_in, b_if, conv_w, conv_b, m_norm_w, w_proj_m, w_proj_a, w_out, post_norm_w):
    for layer in range(pre_norm_w.shape[0]):
        x = _layer(x, pre_norm_w[layer], w_in[layer], b_if[layer], conv_w[layer], conv_b[layer],
                   m_norm_w[layer], w_proj_m[layer], w_proj_a[layer], w_out[layer], post_norm_w[layer])
    return x
```

```python
import functools

import jax
import jax.numpy as jnp
import numpy as np
from jax import lax
from jax.experimental import pallas as pl
from jax.experimental.pallas import tpu as pltpu

D_MODEL = 1024
BATCH = 8
SEQ = 2048
M_WIDTH = 1024
M_HEADS = 4
M_HEAD_DIM = 256
CONV_WIDTH = 4
A_GROUPS = ((128, 1), (512, 4), (2048, 16))
N_GROUPS = 3
A_HEADS = 4
A_HEAD_DIM = 128
A_OUT_WIDTH = 512
A_QKV_WIDTH = 1536
A_BLOCK = 128
ROPE_DIM = 32
ROPE_THETA = 500000.0
NORM_EPS = 1e-6
NEG_INF = -1e30
LOG2E = 1.4426950408889634

OFF_GATES = 5 * M_WIDTH
OFF_QA = OFF_GATES + 2 * M_HEADS
OFF_KA = OFF_QA + A_QKV_WIDTH
OFF_VA = OFF_KA + A_QKV_WIDTH
OFF_ZA = OFF_VA + A_QKV_WIDTH
OFF_GM = OFF_ZA + A_OUT_WIDTH
IN_WIDTH = OFF_GM + 2 * D_MODEL

LANES = 128
M_CHUNK = 256
N_CHUNKS = SEQ // M_CHUNK
V7X_VMEM_BYTES = 64 * 1024 * 1024
VMEM_LIMIT = V7X_VMEM_BYTES * 3 // 4

F32 = jnp.float32
BF16 = jnp.bfloat16


def _dot_nt(a, bt):
    return lax.dot_general(a, bt, (((1,), (1,)), ((), ())), preferred_element_type=F32)


def _sigmoid(x):
    return 1.0 / (1.0 + jnp.exp(-x))


def _silu(x):
    return x * _sigmoid(x)


M_TN = 512


SUBLANES = 8


def _shift_rows(x3, s, sub, carry):
    rot = pltpu.roll(x3, s, axis=1)
    prev = jnp.concatenate([carry, rot[:-1]], axis=0)
    return jnp.where(sub < s, prev, rot), rot[-1:]


def _conv_tile(h_ref, w, cw, cb, emit):
    cols = w.shape[0]
    sub = lax.broadcasted_iota(jnp.int32, (1, SUBLANES, cols), 1)
    zero = jnp.zeros((1, SUBLANES, cols), F32)
    carry = [zero, zero]

    def matmul(p):
        return _dot_nt(h_ref[p * M_CHUNK:(p + 1) * M_CHUNK, :], w)

    def conv(acc):
        x3 = acc.reshape(M_CHUNK // SUBLANES, SUBLANES, cols)
        d1, carry[0] = _shift_rows(x3, 1, sub, carry[0])
        near = x3 * cw[3] + d1 * cw[2] + cb[0]
        far = x3 * cw[1] + d1 * cw[0]
        d2, carry[1] = _shift_rows(far, 2, sub, carry[1])
        return (near + d2).reshape(M_CHUNK, cols)

    acc = matmul(0)
    for p in range(N_CHUNKS):
        nxt = matmul(p + 1) if p + 1 < N_CHUNKS else None
        emit(p, conv(acc))
        acc = nxt


M_TILES = 5 * M_WIDTH // M_TN


def _mproj_kernel(x_ref, prew_ref, w_ref, wg_ref, bif_ref, cw_ref, cb_ref,
                  mall_ref, kt_ref, gt_ref, h_ref):
    j = pl.program_id(1)

    def matmul():
        return _dot_nt(h_ref[...], w_ref[...].astype(BF16))

    @pl.when(j == 0)
    def _():
        x = x_ref[...]
        ms = jnp.mean(x * x, axis=-1, keepdims=True)
        h_ref[...] = (x * lax.rsqrt(ms + NORM_EPS) * prew_ref[...]).astype(BF16)
        g = _dot_nt(h_ref[...], wg_ref[...])
        gt = g.T[:2 * M_HEADS, :] + bif_ref[...]
        for c in range(N_CHUNKS):
            gt_ref[0, :, c, :] = gt[:, c * M_CHUNK:(c + 1) * M_CHUNK]

    @pl.when(j < 2)
    def _():
        def emit(p, y):
            mall_ref[p * M_CHUNK:(p + 1) * M_CHUNK, :] = _silu(y).astype(BF16)
        _conv_tile(h_ref, w_ref[...].astype(BF16), cw_ref[...], cb_ref[...], emit)

    @pl.when(jnp.logical_and(j >= 2, j < 4))
    def _():
        def emit(p, y):
            y = _silu(y) * (M_HEAD_DIM ** -0.5)
            mall_ref[p * M_CHUNK:(p + 1) * M_CHUNK, :] = y.astype(BF16)
            for hh in range(M_TN // M_HEAD_DIM):
                kt_ref[0, hh, p] = y[:, hh * M_HEAD_DIM:(hh + 1) * M_HEAD_DIM].T.astype(BF16)
        _conv_tile(h_ref, w_ref[...].astype(BF16), cw_ref[...], cb_ref[...], emit)

    @pl.when(jnp.logical_and(j >= 4, j < 6))
    def _():
        mall_ref[...] = matmul().astype(BF16)

    @pl.when(jnp.logical_and(j >= 6, j < 8))
    def _():
        mall_ref[...] = _sigmoid(matmul()).astype(BF16)

    @pl.when(j >= 8)
    def _():
        mall_ref[...] = _silu(matmul()).astype(BF16)


def _mproj(x2d, pre_w, wt, wg, bif, conv_w, conv_b):
    return pl.pallas_call(
        _mproj_kernel,
        grid=(BATCH, M_TILES),
        in_specs=[
            pl.BlockSpec((SEQ, D_MODEL), lambda i, j: (i, 0)),
            pl.BlockSpec((1, D_MODEL), lambda i, j: (0, 0)),
            pl.BlockSpec((M_TN, D_MODEL), lambda i, j: (j, 0)),
            pl.BlockSpec((LANES, D_MODEL), lambda i, j: (0, 0)),
            pl.BlockSpec((2 * M_HEADS, 1), lambda i, j: (0, 0)),
            pl.BlockSpec((CONV_WIDTH, M_TN), lambda i, j: (0, jnp.minimum(j, 3))),
            pl.BlockSpec((1, M_TN), lambda i, j: (0, jnp.minimum(j, 3))),
        ],
        out_specs=[
            pl.BlockSpec((SEQ, M_TN), lambda i, j: (i, j)),
            pl.BlockSpec((1, M_TN // M_HEAD_DIM, N_CHUNKS, M_HEAD_DIM, M_CHUNK),
                         lambda i, j: (i, jnp.clip(j - 2, 0, 1), 0, 0, 0)),
            pl.BlockSpec((1, 2 * M_HEADS, N_CHUNKS, M_CHUNK), lambda i, j: (i, 0, 0, 0)),
            pl.BlockSpec((SEQ, D_MODEL), lambda i, j: (i, 0)),
        ],
        out_shape=[
            jax.ShapeDtypeStruct((BATCH * SEQ, 5 * M_WIDTH), BF16),
            jax.ShapeDtypeStruct((BATCH, M_HEADS, N_CHUNKS, M_HEAD_DIM, M_CHUNK), BF16),
            jax.ShapeDtypeStruct((BATCH, 2 * M_HEADS, N_CHUNKS, M_CHUNK), F32),
            jax.ShapeDtypeStruct((BATCH * SEQ, D_MODEL), BF16),
        ],
        compiler_params=pltpu.CompilerParams(dimension_semantics=("parallel", "arbitrary"),
                                             vmem_limit_bytes=VMEM_LIMIT),
        name="mproj",
    )(x2d, pre_w.reshape(1, D_MODEL), wt, wg, bif, conv_w, conv_b)


A_TILES = 3
EASY_STRIDE = 4


def _aproj_kernel(dil, h_ref, w_ref, tab_ref, o_ref, acc0_ref, acc1_ref, tmp_ref):
    s = pl.program_id(0)
    last = BATCH * A_TILES
    pt = (jnp.maximum(s, 1) - 1) % A_TILES

    accs = (acc0_ref, acc1_ref)

    def matmul(slot):
        res = _dot_nt(h_ref[...], w_ref[...].astype(BF16))
        for hd in range(A_HEADS):
            accs[slot][hd] = res[:, hd * A_HEAD_DIM:(hd + 1) * A_HEAD_DIM]

    ls = SEQ // dil

    two_pass = dil == EASY_STRIDE * EASY_STRIDE

    def stage(slot, hd):
        if two_pass:
            n4 = SEQ // EASY_STRIDE
            for r4 in range(EASY_STRIDE):
                tmp_ref[hd, r4 * n4:(r4 + 1) * n4, :] = accs[slot][hd, pl.ds(r4, n4, stride=EASY_STRIDE), :]

    def residue_rows(slot, hd, r):
        if dil == 1:
            return accs[slot][hd]
        if two_pass:
            q, r4 = divmod(r, EASY_STRIDE)
            return tmp_ref[hd, pl.ds(r4 * (SEQ // EASY_STRIDE) + q, ls, stride=EASY_STRIDE), :]
        return accs[slot][hd, pl.ds(r, ls, stride=dil), :]

    def rope_epilogue(slot):
        scale = jnp.where(pt == 0, A_HEAD_DIM ** -0.5 * LOG2E, 1.0).astype(F32)
        half = ROPE_DIM // 2
        for hd in range(A_HEADS):
            stage(slot, hd)
            for r in range(dil):
                rows = slice(r * ls, (r + 1) * ls)
                xh = residue_rows(slot, hd, r)
                y = (xh * tab_ref[0, rows, :] + pltpu.roll(xh, half, axis=1) * tab_ref[1, rows, :]
                     + pltpu.roll(xh, A_HEAD_DIM - half, axis=1) * tab_ref[2, rows, :])
                o_ref[0, 0, rows, hd * A_HEAD_DIM:(hd + 1) * A_HEAD_DIM] = (y * scale).astype(BF16)

    def cast_epilogue(slot):
        for hd in range(A_HEADS):
            stage(slot, hd)
            for r in range(dil):
                rows = slice(r * ls, (r + 1) * ls)
                o_ref[0, 0, rows, hd * A_HEAD_DIM:(hd + 1) * A_HEAD_DIM] = \
                    residue_rows(slot, hd, r).astype(BF16)

    mid = jnp.logical_and(s > 0, s < last)

    @pl.when(s == 0)
    def _():
        matmul(0)

    for slot in range(2):
        mine = jnp.logical_and(mid, s % 2 == slot)

        @pl.when(jnp.logical_and(mine, pt < 2))
        def _(slot=slot):
            rope_epilogue(1 - slot)
            matmul(slot)

        @pl.when(jnp.logical_and(mine, pt == 2))
        def _(slot=slot):
            cast_epilogue(1 - slot)
            matmul(slot)

    @pl.when(s == last)
    def _():
        cast_epilogue((last - 1) % 2)


def _aproj(h, wt, tabs_g, g):
    dil = A_GROUPS[g][1]

    def w_row(s):
        return OFF_QA + (cur(s) % A_TILES) * A_QKV_WIDTH + g * A_OUT_WIDTH
    last = BATCH * A_TILES

    def cur(s):
        return jnp.minimum(s, last - 1)

    def prev(s):
        return jnp.maximum(s, 1) - 1

    return pl.pallas_call(
        functools.partial(_aproj_kernel, dil),
        grid=(last + 1,),
        in_specs=[
            pl.BlockSpec((SEQ, D_MODEL), lambda s: (cur(s) // A_TILES, 0)),
            pl.BlockSpec((pl.Element(A_OUT_WIDTH), pl.Element(D_MODEL)),
                         lambda s: (pl.multiple_of(w_row(s), SUBLANES), 0)),
            pl.BlockSpec((3, SEQ, LANES), lambda s: (0, 0, 0)),
        ],
        out_specs=pl.BlockSpec((1, 1, SEQ, A_OUT_WIDTH),
                               lambda s: (prev(s) // A_TILES, prev(s) % A_TILES, 0, 0)),
        out_shape=jax.ShapeDtypeStruct((BATCH, A_TILES, SEQ, A_OUT_WIDTH), BF16),
        scratch_shapes=[pltpu.VMEM((A_HEADS, SEQ, A_HEAD_DIM), F32) for _ in range(3)],
        compiler_params=pltpu.CompilerParams(dimension_semantics=("arbitrary",),
                                             vmem_limit_bytes=VMEM_LIMIT),
        name="aproj_d%d" % dil,
    )(h, wt, tabs_g)


def _lane_scan(x, op, fill):
    lane = lax.broadcasted_iota(jnp.int32, x.shape, 1)
    sh = 1
    while sh < x.shape[1]:
        shifted = jnp.where(lane >= sh, pltpu.roll(x, sh, axis=1), fill)
        x = op(x, shifted)
        sh *= 2
    return x


def _gates_kernel(ig_ref, fg_ref, u_ref, g_ref, mt_ref, mp_ref, w_ref, dec_ref):
    L = M_CHUNK
    ig = ig_ref[...]
    fg = fg_ref[...]
    logf = -(jnp.maximum(-fg, 0.0) + jnp.log1p(jnp.exp(-jnp.abs(fg))))
    b = _lane_scan(logf, jnp.add, 0.0)
    u = ig - b
    cm = _lane_scan(u, jnp.maximum, NEG_INF)
    lane = lax.broadcasted_iota(jnp.int32, ig.shape, 1)
    btot = jnp.broadcast_to(jnp.sum(jnp.where(lane == L - 1, b, 0.0), axis=1, keepdims=True), ig.shape)
    umax = jnp.broadcast_to(jnp.max(cm, axis=1, keepdims=True), ig.shape)
    chunk = lax.broadcasted_iota(jnp.int32, ig.shape, 0) % N_CHUNKS
    m = jnp.zeros(ig.shape, F32)
    for c in range(N_CHUNKS - 1):
        nxt = pltpu.roll(btot + jnp.maximum(m, umax), 1, axis=0)
        m = jnp.where(chunk == c + 1, nxt, m)
    g = jnp.maximum(m, cm)
    glast = jnp.max(g, axis=1, keepdims=True)
    u_ref[...] = u * LOG2E
    g_ref[...] = g * LOG2E
    mt_ref[...] = (b + g) * LOG2E
    mp_ref[...] = m * LOG2E
    w_ref[...] = jnp.exp(u - glast)
    dec_ref[...] = jnp.broadcast_to(jnp.exp(m - glast), ig.shape)


def _gates(gt):
    rows = BATCH * M_HEADS * N_CHUNKS
    ig = gt[:, :M_HEADS].reshape(rows, M_CHUNK)
    fg = gt[:, M_HEADS:].reshape(rows, M_CHUNK)
    spec = pl.BlockSpec((rows, M_CHUNK), lambda: (0, 0))
    return pl.pallas_call(
        _gates_kernel,
        in_specs=[spec, spec],
        out_specs=[spec] * 6,
        out_shape=[jax.ShapeDtypeStruct((rows, M_CHUNK), F32)] * 6,
        name="gates",
    )(ig, fg)


def _mlstm_kernel(q_ref, kt_ref, v_ref, o_ref, z_ref, u_ref, g_ref, mt_ref, mp_ref, w_ref, dec_ref,
                  out_ref):
    L = M_CHUNK
    ti = lax.broadcasted_iota(jnp.int32, (L, L), 0)
    si = lax.broadcasted_iota(jnp.int32, (L, L), 1)
    causal = ti >= si
    ones = jnp.ones((L, LANES), BF16)

    state = jnp.zeros((M_HEAD_DIM, M_HEAD_DIM + LANES), F32)
    for c in range(N_CHUNKS):
        rs = slice(c * L, (c + 1) * L)
        cs = slice(c, c + 1)
        qc = q_ref[rs, :]
        v1 = jnp.concatenate([v_ref[rs, :], ones], axis=1)
        ktc = kt_ref[0, 0, c]
        u_r = u_ref[cs, :]
        gcol = jnp.broadcast_to(g_ref[cs, :], (LANES, L)).T
        mcol = jnp.broadcast_to(mt_ref[cs, :], (LANES, L)).T
        dmat = jnp.where(causal, jnp.exp2(u_r - jnp.concatenate([gcol, gcol], axis=1)), 0.0)

        s = jnp.dot(qc, ktc, preferred_element_type=F32) * dmat
        tot = jnp.dot(s.astype(BF16), v1, preferred_element_type=F32)
        if c > 0:
            inter = jnp.exp2(mp_ref[cs, :LANES] - gcol)
            tot = tot + jnp.concatenate([inter, inter, inter], axis=1) \
                * jnp.dot(qc, state.astype(BF16), preferred_element_type=F32)
        den = tot[:, M_HEAD_DIM:]
        rinv = 1.0 / jnp.maximum(jnp.abs(den), jnp.exp2(-mcol))
        hval = tot[:, :M_HEAD_DIM] * jnp.concatenate([rinv, rinv], axis=1)

        hg = hval * o_ref[rs, :].astype(F32)
        ms = jnp.mean(hg * hg, axis=1, keepdims=True)
        hn = hg * lax.rsqrt(ms + NORM_EPS)
        out_ref[rs, :] = (hn * z_ref[rs, :].astype(F32)).astype(BF16)

        if c + 1 < N_CHUNKS:
            ktw = (ktc.astype(F32) * w_ref[cs, :]).astype(BF16)
            dec = dec_ref[cs, :]
            dec = jnp.concatenate([dec, dec[:, :LANES]], axis=1)
            state = dec * state + jnp.dot(ktw, v1, preferred_element_type=F32)


def _mlstm(mall, kt, gate_stats):
    blk = (SEQ, M_HEAD_DIM)

    def seg(k):
        return pl.BlockSpec(blk, lambda b, h, k=k: (b, k * M_HEADS + h))

    stat = pl.BlockSpec((N_CHUNKS, M_CHUNK), lambda b, h: (b * M_HEADS + h, 0))
    return pl.pallas_call(
        _mlstm_kernel,
        grid=(BATCH, M_HEADS),
        in_specs=[
            seg(0),
            pl.BlockSpec((1, 1, N_CHUNKS, M_HEAD_DIM, M_CHUNK), lambda b, h: (b, h, 0, 0, 0)),
            seg(2), seg(3), seg(4),
        ] + [stat] * 6,
        out_specs=pl.BlockSpec(blk, lambda b, h: (b, h)),
        out_shape=jax.ShapeDtypeStruct((BATCH * SEQ, M_WIDTH), BF16),
        compiler_params=pltpu.CompilerParams(dimension_semantics=("parallel", "parallel"),
                                             vmem_limit_bytes=VMEM_LIMIT),
        name="mlstm",
    )(mall, kt, mall, mall, mall, *gate_stats)


GATHERED = (2,)
GATHER_PITCH = A_BLOCK + 8
ATTN_WAVE = 12


def _attn_block_order():
    order = [(r, 2) for r in range(A_GROUPS[2][1])]
    nblk1 = SEQ // A_GROUPS[1][1] // A_BLOCK
    per_span = SEQ // A_BLOCK // nblk1
    for n in range(nblk1):
        order += [(r * nblk1 + n, 1) for r in range(A_GROUPS[1][1])]
        order += [(n * per_span + i, 0) for i in range(per_span)]
    return order


def _attn_kernel(q0, k0, v0, q1, k1, v1, q2, k2, v2, out_ref,
                 a0, a1, a2, m0, m1, m2, l0, l1, l2):
    BLK = A_BLOCK
    ti = lax.broadcasted_iota(jnp.int32, (BLK, BLK), 0)
    ji = lax.broadcasted_iota(jnp.int32, (BLK, BLK), 1)
    cur_ok = ji <= ti
    ti2 = lax.broadcasted_iota(jnp.int32, (BLK, 2 * BLK), 0)
    krel = lax.broadcasted_iota(jnp.int32, (BLK, 2 * BLK), 1) - BLK
    band_ok = jnp.logical_and(krel <= ti2, krel >= ti2 - BLK)

    qs = (q0, q1, q2)
    ks = (k0, k1, k2)
    vs = (v0, v1, v2)
    accs = (a0, a1, a2)
    ms = (m0, m1, m2)
    ls = (l0, l1, l2)

    RB = 256

    def merge(rb):
        sl = slice(rb * RB, (rb + 1) * RB)

        def rows(refs, g):
            if g in GATHERED:
                dil = A_GROUPS[g][1]
                j0 = rb * RB // dil
                return jnp.concatenate([refs[g][pl.ds(j0 + jj, dil, stride=GATHER_PITCH), :]
                                        for jj in range(RB // dil)], axis=0)
            return refs[g][sl, :]

        mg = [rows(ms, g) for g in range(N_GROUPS)]
        mmax = jnp.maximum(jnp.maximum(mg[0], mg[1]), mg[2])
        num = None
        den = None
        for g in range(N_GROUPS):
            e = jnp.exp2(mg[g] - mmax)
            num = e * rows(accs, g) if num is None else num + e * rows(accs, g)
            den = e * rows(ls, g) if den is None else den + e * rows(ls, g)
        out_ref[sl, :] = (num * (1.0 / den)).astype(BF16)

    blocks = _attn_block_order()
    done_rows = [set() for _ in range(N_GROUPS)]
    merged = 0
    for w0 in range(0, len(blocks), ATTN_WAVE):
        wave = []
        for i, g in blocks[w0:w0 + ATTN_WAVE]:
            dil = A_GROUPS[g][1]
            r, n = divmod(i, SEQ // dil // BLK)
            rows = slice(i * BLK, (i + 1) * BLK)
            keys = slice((i - 1) * BLK, (i + 1) * BLK) if n > 0 else rows
            valid = band_ok if n > 0 else cur_ok
            s = lax.dot_general(qs[g][0, 0, rows, :], ks[g][0, 0, keys, :], (((1,), (1,)), ((), ())),
                                preferred_element_type=F32)
            s = jnp.where(valid, s, NEG_INF)
            m = jnp.max(s, axis=1, keepdims=True)
            tok0 = n * BLK * dil + r
            if g in GATHERED:
                idx = pl.ds(r * GATHER_PITCH, BLK)
            else:
                idx = pl.ds(tok0, BLK, stride=dil) if dil > 1 else pl.ds(tok0, BLK)
            wave.append((g, keys, valid, idx, s, m))
        for g, keys, valid, idx, s, m in wave:
            p = jnp.where(valid, jnp.exp2(s - m), 0.0)
            nkeys = keys.stop - keys.start
            v1 = jnp.concatenate([vs[g][0, 0, keys, :], jnp.ones((nkeys, LANES), BF16)], axis=1)
            acc = jnp.dot(p.astype(BF16), v1, preferred_element_type=F32)
            accs[g][idx, :] = acc[:, :LANES]
            ms[g][idx, :] = jnp.broadcast_to(m, (BLK, LANES))
            ls[g][idx, :] = acc[:, LANES:]
        for i, g in blocks[w0:w0 + ATTN_WAVE]:
            dil = A_GROUPS[g][1]
            r, n = divmod(i, SEQ // dil // BLK)
            span = BLK * dil
            if r == dil - 1:
                done_rows[g].update(range(n * span // RB, (n + 1) * span // RB))
        while merged < SEQ // RB and all(merged in d for d in done_rows):
            merge(merged)
            merged += 1


def _attn(qkv):
    in_specs = []
    args = []
    for g in range(N_GROUPS):
        for t in range(3):
            in_specs.append(pl.BlockSpec((1, 1, SEQ, A_HEAD_DIM), lambda b, h, t=t: (b, t, 0, h)))
            args.append(qkv[g])
    return pl.pallas_call(
        _attn_kernel,
        grid=(BATCH, A_HEADS),
        in_specs=in_specs,
        out_specs=pl.BlockSpec((SEQ, A_HEAD_DIM), lambda b, h: (b, h)),
        out_shape=jax.ShapeDtypeStruct((BATCH * SEQ, A_OUT_WIDTH), BF16),
        scratch_shapes=[pltpu.VMEM((A_GROUPS[g][1] * GATHER_PITCH if g in GATHERED else SEQ, LANES), F32)
                        for _ in range(3) for g in range(N_GROUPS)],
        compiler_params=pltpu.CompilerParams(dimension_semantics=("parallel", "parallel"),
                                             vmem_limit_bytes=VMEM_LIMIT),
        name="attn",
    )(*args)


O_TM = 1024


def _out_kernel(x_ref, prew_ref, hm_ref, att_ref, wg_ref, wpm_ref, wpa_ref, wout_ref, postw_ref,
                y_ref, wgb_ref):
    @pl.when(pl.program_id(0) == 0)
    def _():
        wgb_ref[...] = wg_ref[...].astype(BF16)

    x = x_ref[...]
    ms = jnp.mean(x * x, axis=-1, keepdims=True)
    h = (x * lax.rsqrt(ms + NORM_EPS) * prew_ref[...]).astype(BF16)
    gates = _dot_nt(h, wgb_ref[...])
    za = _silu(gates[:, :A_OUT_WIDTH])
    sgm = _sigmoid(gates[:, A_OUT_WIDTH:A_OUT_WIDTH + D_MODEL])
    sga = _sigmoid(gates[:, A_OUT_WIDTH + D_MODEL:])
    a = (att_ref[...].astype(F32) * za).astype(BF16)
    bm = jnp.dot(hm_ref[...], wpm_ref[...], preferred_element_type=F32)
    ba = jnp.dot(a, wpa_ref[...], preferred_element_type=F32)
    merged = (sgm * bm + sga * ba).astype(BF16)
    y = jnp.dot(merged, wout_ref[...], preferred_element_type=F32)
    ms2 = jnp.mean(y * y, axis=-1, keepdims=True)
    y_ref[...] = x + y * lax.rsqrt(ms2 + NORM_EPS) * postw_ref[...]


def _out(x2d, pre_w, hm, att, wt, wpm, wpa, wout, post_w):
    n = x2d.shape[0]
    const = lambda i: (0, 0)
    return pl.pallas_call(
        _out_kernel,
        grid=(n // O_TM,),
        in_specs=[
            pl.BlockSpec((O_TM, D_MODEL), lambda i: (i, 0)),
            pl.BlockSpec((1, D_MODEL), const),
            pl.BlockSpec((O_TM, M_WIDTH), lambda i: (i, 0)),
            pl.BlockSpec((O_TM, A_OUT_WIDTH), lambda i: (i, 0)),
            pl.BlockSpec((pl.Element(IN_WIDTH - OFF_ZA), pl.Element(D_MODEL)), lambda i: (OFF_ZA, 0),
                         pipeline_mode=pl.Buffered(1)),
            pl.BlockSpec((M_WIDTH, D_MODEL), const),
            pl.BlockSpec((A_OUT_WIDTH, D_MODEL), const),
            pl.BlockSpec((D_MODEL, D_MODEL), const),
            pl.BlockSpec((1, D_MODEL), const),
        ],
        out_specs=pl.BlockSpec((O_TM, D_MODEL), lambda i: (i, 0)),
        out_shape=jax.ShapeDtypeStruct((n, D_MODEL), F32),
        scratch_shapes=[pltpu.VMEM((IN_WIDTH - OFF_ZA, D_MODEL), BF16)],
        compiler_params=pltpu.CompilerParams(dimension_semantics=("arbitrary",),
                                             vmem_limit_bytes=VMEM_LIMIT),
        name="outproj",
    )(x2d, pre_w.reshape(1, D_MODEL), hm, att, wt, wpm, wpa, wout, post_w.reshape(1, D_MODEL))


def _rope_tables():
    pos = np.arange(SEQ, dtype=np.float64)
    inv_freq = ROPE_THETA ** (-np.arange(0, ROPE_DIM, 2, dtype=np.float64) / ROPE_DIM)
    ang = pos[:, None] * inv_freq[None, :]
    cos, sin = np.cos(ang), np.sin(ang)
    half = ROPE_DIM // 2
    pad = A_HEAD_DIM - ROPE_DIM
    cos_t = np.concatenate([cos, cos, np.ones((SEQ, pad))], axis=1)
    sin_up = np.concatenate([np.zeros((SEQ, half)), sin, np.zeros((SEQ, pad))], axis=1)
    sin_dn = np.concatenate([-sin, np.zeros((SEQ, half + pad))], axis=1)
    tab = np.stack([cos_t, sin_up, sin_dn]).astype(np.float32)
    out = []
    for _, dil in A_GROUPS:
        ls = SEQ // dil
        out.append(tab.reshape(3, ls, dil, A_HEAD_DIM).transpose(0, 2, 1, 3).reshape(3, SEQ, A_HEAD_DIM))
    return jnp.asarray(np.stack(out))


def _layer(x, pre_w, w_in, b_if, conv_w, conv_b, m_norm_w, w_pm, w_pa, w_out, post_w):
    x2d = x.reshape(BATCH * SEQ, D_MODEL)
    wt = w_in.T
    wg = jnp.pad(wt[OFF_GATES:OFF_QA], ((0, LANES - 2 * M_HEADS), (0, 0))).astype(BF16)
    tabs = _rope_tables()

    mall, kt, gt, h = _mproj(x2d, pre_w, wt, wg, b_if.reshape(2 * M_HEADS, 1), conv_w,
                             conv_b.reshape(1, 2 * M_WIDTH))
    qkv = [_aproj(h, wt, tabs[g], g) for g in range(N_GROUPS)]
    hm = _mlstm(mall, kt, _gates(gt))
    att = _attn(qkv)
    w_pm_n = (m_norm_w[:, None] * w_pm).astype(BF16)
    y = _out(x2d, pre_w, hm, att, wt, w_pm_n, w_pa.astype(BF16), w_out.astype(BF16), post_w)
    return y.reshape(BATCH, SEQ, D_MODEL)


@jax.jit
def kernel(x, pre_norm_w, w_in, b_if, conv_w, conv_b, m_norm_w, w_proj_m, w_proj_a, w_out, post_norm_w):
    for layer in range(pre_norm_w.shape[0]):
        x = _layer(x, pre_norm_w[layer], w_in[layer], b_if[layer], conv_w[layer], conv_b[layer],
                   m_norm_w[layer], w_proj_m[layer], w_proj_a[layer], w_out[layer], post_norm_w[layer])
    return x
```

```python
import functools

import jax
import jax.numpy as jnp
import numpy as np
from jax import lax
from jax.experimental import pallas as pl
from jax.experimental.pallas import tpu as pltpu

D_MODEL = 1024
BATCH = 8
SEQ = 2048
M_WIDTH = 1024
M_HEADS = 4
M_HEAD_DIM = 256
CONV_WIDTH = 4
A_GROUPS = ((128, 1), (512, 4), (2048, 16))
N_GROUPS = 3
A_HEADS = 4
A_HEAD_DIM = 128
A_OUT_WIDTH = 512
A_QKV_WIDTH = 1536
A_BLOCK = 128
ROPE_DIM = 32
ROPE_THETA = 500000.0
NORM_EPS = 1e-6
NEG_INF = -1e30
LOG2E = 1.4426950408889634

OFF_GATES = 5 * M_WIDTH
OFF_QA = OFF_GATES + 2 * M_HEADS
OFF_KA = OFF_QA + A_QKV_WIDTH
OFF_VA = OFF_KA + A_QKV_WIDTH
OFF_ZA = OFF_VA + A_QKV_WIDTH
OFF_GM = OFF_ZA + A_OUT_WIDTH
IN_WIDTH = OFF_GM + 2 * D_MODEL

LANES = 128
M_CHUNK = 256
N_CHUNKS = SEQ // M_CHUNK
V7X_VMEM_BYTES = 64 * 1024 * 1024
VMEM_LIMIT = V7X_VMEM_BYTES * 3 // 4

F32 = jnp.float32
BF16 = jnp.bfloat16


def _dot_nt(a, bt):
    return lax.dot_general(a, bt, (((1,), (1,)), ((), ())), preferred_element_type=F32)


def _sigmoid(x):
    return 1.0 / (1.0 + jnp.exp(-x))


def _silu(x):
    return x * _sigmoid(x)


M_TN = 512


SUBLANES = 8


def _shift_rows(x3, s, sub, carry):
    rot = pltpu.roll(x3, s, axis=1)
    prev = jnp.concatenate([carry, rot[:-1]], axis=0)
    return jnp.where(sub < s, prev, rot), rot[-1:]


def _conv_tile(h_ref, w, cw, cb, emit):
    cols = w.shape[0]
    sub = lax.broadcasted_iota(jnp.int32, (1, SUBLANES, cols), 1)
    zero = jnp.zeros((1, SUBLANES, cols), F32)
    carry = [zero, zero]

    def matmul(p):
        return _dot_nt(h_ref[p * M_CHUNK:(p + 1) * M_CHUNK, :], w)

    def conv(acc):
        x3 = acc.reshape(M_CHUNK // SUBLANES, SUBLANES, cols)
        d1, carry[0] = _shift_rows(x3, 1, sub, carry[0])
        near = x3 * cw[3] + d1 * cw[2] + cb[0]
        far = x3 * cw[1] + d1 * cw[0]
        d2, carry[1] = _shift_rows(far, 2, sub, carry[1])
        return (near + d2).reshape(M_CHUNK, cols)

    acc = matmul(0)
    for p in range(N_CHUNKS):
        nxt = matmul(p + 1) if p + 1 < N_CHUNKS else None
        emit(p, conv(acc))
        acc = nxt


M_TILES = 5 * M_WIDTH // M_TN


def _mproj_kernel(x_ref, prew_ref, w_ref, wg_ref, bif_ref, cw_ref, cb_ref,
                  mall_ref, kt_ref, gt_ref, h_ref):
    j = pl.program_id(1)

    def matmul():
        return _dot_nt(h_ref[...], w_ref[...].astype(BF16))

    @pl.when(j == 0)
    def _():
        x = x_ref[...]
        ms = jnp.mean(x * x, axis=-1, keepdims=True)
        h_ref[...] = (x * lax.rsqrt(ms + NORM_EPS) * prew_ref[...]).astype(BF16)
        g = _dot_nt(h_ref[...], wg_ref[...])
        gt = g.T[:2 * M_HEADS, :] + bif_ref[...]
        for c in range(N_CHUNKS):
            gt_ref[0, :, c, :] = gt[:, c * M_CHUNK:(c + 1) * M_CHUNK]

    @pl.when(j < 2)
    def _():
        def emit(p, y):
            mall_ref[p * M_CHUNK:(p + 1) * M_CHUNK, :] = _silu(y).astype(BF16)
        _conv_tile(h_ref, w_ref[...].astype(BF16), cw_ref[...], cb_ref[...], emit)

    @pl.when(jnp.logical_and(j >= 2, j < 4))
    def _():
        def emit(p, y):
            y = _silu(y) * (M_HEAD_DIM ** -0.5)
            mall_ref[p * M_CHUNK:(p + 1) * M_CHUNK, :] = y.astype(BF16)
            for hh in range(M_TN // M_HEAD_DIM):
                kt_ref[0, hh, p] = y[:, hh * M_HEAD_DIM:(hh + 1) * M_HEAD_DIM].T.astype(BF16)
        _conv_tile(h_ref, w_ref[...].astype(BF16), cw_ref[...], cb_ref[...], emit)

    @pl.when(jnp.logical_and(j >= 4, j < 6))
    def _():
        mall_ref[...] = matmul().astype(BF16)

    @pl.when(jnp.logical_and(j >= 6, j < 8))
    def _():
        mall_ref[...] = _sigmoid(matmul()).astype(BF16)

    @pl.when(j >= 8)
    def _():
        mall_ref[...] = _silu(matmul()).astype(BF16)


def _mproj(x2d, pre_w, wt, wg, bif, conv_w, conv_b):
    return pl.pallas_call(
        _mproj_kernel,
        grid=(BATCH, M_TILES),
        in_specs=[
            pl.BlockSpec((SEQ, D_MODEL), lambda i, j: (i, 0)),
            pl.BlockSpec((1, D_MODEL), lambda i, j: (0, 0)),
            pl.BlockSpec((M_TN, D_MODEL), lambda i, j: (j, 0)),
            pl.BlockSpec((LANES, D_MODEL), lambda i, j: (0, 0)),
            pl.BlockSpec((2 * M_HEADS, 1), lambda i, j: (0, 0)),
            pl.BlockSpec((CONV_WIDTH, M_TN), lambda i, j: (0, jnp.minimum(j, 3))),
            pl.BlockSpec((1, M_TN), lambda i, j: (0, jnp.minimum(j, 3))),
        ],
        out_specs=[
            pl.BlockSpec((SEQ, M_TN), lambda i, j: (i, j)),
            pl.BlockSpec((1, M_TN // M_HEAD_DIM, N_CHUNKS, M_HEAD_DIM, M_CHUNK),
                         lambda i, j: (i, jnp.clip(j - 2, 0, 1), 0, 0, 0)),
            pl.BlockSpec((1, 2 * M_HEADS, N_CHUNKS, M_CHUNK), lambda i, j: (i, 0, 0, 0)),
            pl.BlockSpec((SEQ, D_MODEL), lambda i, j: (i, 0)),
        ],
        out_shape=[
            jax.ShapeDtypeStruct((BATCH * SEQ, 5 * M_WIDTH), BF16),
            jax.ShapeDtypeStruct((BATCH, M_HEADS, N_CHUNKS, M_HEAD_DIM, M_CHUNK), BF16),
            jax.ShapeDtypeStruct((BATCH, 2 * M_HEADS, N_CHUNKS, M_CHUNK), F32),
            jax.ShapeDtypeStruct((BATCH * SEQ, D_MODEL), BF16),
        ],
        compiler_params=pltpu.CompilerParams(dimension_semantics=("parallel", "arbitrary"),
                                             vmem_limit_bytes=VMEM_LIMIT),
        name="mproj",
    )(x2d, pre_w.reshape(1, D_MODEL), wt, wg, bif, conv_w, conv_b)


A_TILES = 3
EASY_STRIDE = 4


def _aproj_kernel(dil, h_ref, w_ref, tab_ref, o_ref, acc0_ref, acc1_ref, tmp_ref):
    s = pl.program_id(0)
    last = BATCH * A_TILES
    pt = (jnp.maximum(s, 1) - 1) % A_TILES

    accs = (acc0_ref, acc1_ref)

    def matmul(slot):
        res = _dot_nt(h_ref[...], w_ref[...].astype(BF16))
        for hd in range(A_HEADS):
            accs[slot][hd] = res[:, hd * A_HEAD_DIM:(hd + 1) * A_HEAD_DIM]

    ls = SEQ // dil

    two_pass = dil == EASY_STRIDE * EASY_STRIDE

    def stage(slot, hd):
        if two_pass:
            n4 = SEQ // EASY_STRIDE
            for r4 in range(EASY_STRIDE):
                tmp_ref[hd, r4 * n4:(r4 + 1) * n4, :] = accs[slot][hd, pl.ds(r4, n4, stride=EASY_STRIDE), :]

    def residue_rows(slot, hd, r):
        if dil == 1:
            return accs[slot][hd]
        if two_pass:
            q, r4 = divmod(r, EASY_STRIDE)
            return tmp_ref[hd, pl.ds(r4 * (SEQ // EASY_STRIDE) + q, ls, stride=EASY_STRIDE), :]
        return accs[slot][hd, pl.ds(r, ls, stride=dil), :]

    def rope_epilogue(slot):
        scale = jnp.where(pt == 0, A_HEAD_DIM ** -0.5 * LOG2E, 1.0).astype(F32)
        half = ROPE_DIM // 2
        for hd in range(A_HEADS):
            stage(slot, hd)
            for r in range(dil):
                rows = slice(r * ls, (r + 1) * ls)
                xh = residue_rows(slot, hd, r)
                y = (xh * tab_ref[0, rows, :] + pltpu.roll(xh, half, axis=1) * tab_ref[1, rows, :]
                     + pltpu.roll(xh, A_HEAD_DIM - half, axis=1) * tab_ref[2, rows, :])
                o_ref[0, 0, rows, hd * A_HEAD_DIM:(hd + 1) * A_HEAD_DIM] = (y * scale).astype(BF16)

    def cast_epilogue(slot):
        for hd in range(A_HEADS):
            stage(slot, hd)
            for r in range(dil):
                rows = slice(r * ls, (r + 1) * ls)
                o_ref[0, 0, rows, hd * A_HEAD_DIM:(hd + 1) * A_HEAD_DIM] = \
                    residue_rows(slot, hd, r).astype(BF16)

    mid = jnp.logical_and(s > 0, s < last)

    @pl.when(s == 0)
    def _():
        matmul(0)

    for slot in range(2):
        mine = jnp.logical_and(mid, s % 2 == slot)

        @pl.when(jnp.logical_and(mine, pt < 2))
        def _(slot=slot):
            rope_epilogue(1 - slot)
            matmul(slot)

        @pl.when(jnp.logical_and(mine, pt == 2))
        def _(slot=slot):
            cast_epilogue(1 - slot)
            matmul(slot)

    @pl.when(s == last)
    def _():
        cast_epilogue((last - 1) % 2)


def _aproj(h, wt, tabs_g, g):
    dil = A_GROUPS[g][1]

    def w_row(s):
        return OFF_QA + (cur(s) % A_TILES) * A_QKV_WIDTH + g * A_OUT_WIDTH
    last = BATCH * A_TILES

    def cur(s):
        return jnp.minimum(s, last - 1)

    def prev(s):
        return jnp.maximum(s, 1) - 1

    return pl.pallas_call(
        functools.partial(_aproj_kernel, dil),
        grid=(last + 1,),
        in_specs=[
            pl.BlockSpec((SEQ, D_MODEL), lambda s: (cur(s) // A_TILES, 0)),
            pl.BlockSpec((pl.Element(A_OUT_WIDTH), pl.Element(D_MODEL)),
                         lambda s: (pl.multiple_of(w_row(s), SUBLANES), 0)),
            pl.BlockSpec((3, SEQ, LANES), lambda s: (0, 0, 0)),
        ],
        out_specs=pl.BlockSpec((1, 1, SEQ, A_OUT_WIDTH),
                               lambda s: (prev(s) // A_TILES, prev(s) % A_TILES, 0, 0)),
        out_shape=jax.ShapeDtypeStruct((BATCH, A_TILES, SEQ, A_OUT_WIDTH), BF16),
        scratch_shapes=[pltpu.VMEM((A_HEADS, SEQ, A_HEAD_DIM), F32) for _ in range(3)],
        compiler_params=pltpu.CompilerParams(dimension_semantics=("arbitrary",),
                                             vmem_limit_bytes=VMEM_LIMIT),
        name="aproj_d%d" % dil,
    )(h, wt, tabs_g)


def _lane_scan(x, op, fill):
    lane = lax.broadcasted_iota(jnp.int32, x.shape, 1)
    sh = 1
    while sh < x.shape[1]:
        shifted = jnp.where(lane >= sh, pltpu.roll(x, sh, axis=1), fill)
        x = op(x, shifted)
        sh *= 2
    return x


def _gates_kernel(ig_ref, fg_ref, u_ref, g_ref, mt_ref, mp_ref, w_ref, dec_ref):
    L = M_CHUNK
    ig = ig_ref[...]
    fg = fg_ref[...]
    logf = -(jnp.maximum(-fg, 0.0) + jnp.log1p(jnp.exp(-jnp.abs(fg))))
    b = _lane_scan(logf, jnp.add, 0.0)
    u = ig - b
    cm = _lane_scan(u, jnp.maximum, NEG_INF)
    lane = lax.broadcasted_iota(jnp.int32, ig.shape, 1)
    btot = jnp.broadcast_to(jnp.sum(jnp.where(lane == L - 1, b, 0.0), axis=1, keepdims=True), ig.shape)
    umax = jnp.broadcast_to(jnp.max(cm, axis=1, keepdims=True), ig.shape)
    chunk = lax.broadcasted_iota(jnp.int32, ig.shape, 0) % N_CHUNKS
    m = jnp.zeros(ig.shape, F32)
    for c in range(N_CHUNKS - 1):
        nxt = pltpu.roll(btot + jnp.maximum(m, umax), 1, axis=0)
        m = jnp.where(chunk == c + 1, nxt, m)
    g = jnp.maximum(m, cm)
    glast = jnp.max(g, axis=1, keepdims=True)
    u_ref[...] = u * LOG2E
    g_ref[...] = g * LOG2E
    mt_ref[...] = (b + g) * LOG2E
    mp_ref[...] = m * LOG2E
    w_ref[...] = jnp.exp(u - glast)
    dec_ref[...] = jnp.broadcast_to(jnp.exp(m - glast), ig.shape)


def _gates(gt):
    rows = BATCH * M_HEADS * N_CHUNKS
    ig = gt[:, :M_HEADS].reshape(rows, M_CHUNK)
    fg = gt[:, M_HEADS:].reshape(rows, M_CHUNK)
    spec = pl.BlockSpec((rows, M_CHUNK), lambda: (0, 0))
    return pl.pallas_call(
        _gates_kernel,
        in_specs=[spec, spec],
        out_specs=[spec] * 6,
        out_shape=[jax.ShapeDtypeStruct((rows, M_CHUNK), F32)] * 6,
        name="gates",
    )(ig, fg)


M_HPS = 2


def _mlstm_kernel(q_ref, kt_ref, v_ref, o_ref, z_ref, u_ref, g_ref, mt_ref, mp_ref, w_ref, dec_ref,
                  out_ref):
    L = M_CHUNK
    ti = lax.broadcasted_iota(jnp.int32, (L, L), 0)
    si = lax.broadcasted_iota(jnp.int32, (L, L), 1)
    causal = ti >= si
    ones = jnp.ones((L, LANES), BF16)

    states = [jnp.zeros((M_HEAD_DIM, M_HEAD_DIM + LANES), F32) for _ in range(M_HPS)]
    for c in range(N_CHUNKS):
        rs = slice(c * L, (c + 1) * L)
        for hh in range(M_HPS):
            hs = slice(hh * M_HEAD_DIM, (hh + 1) * M_HEAD_DIM)
            cs = slice(hh * N_CHUNKS + c, hh * N_CHUNKS + c + 1)
            state = states[hh]
            qc = q_ref[rs, hs]
            v1 = jnp.concatenate([v_ref[rs, hs], ones], axis=1)
            ktc = kt_ref[0, hh, c]
            u_r = u_ref[cs, :]
            gcol = jnp.broadcast_to(g_ref[cs, :], (LANES, L)).T
            mcol = jnp.broadcast_to(mt_ref[cs, :], (LANES, L)).T
            dmat = jnp.where(causal, jnp.exp2(u_r - jnp.concatenate([gcol, gcol], axis=1)), 0.0)

            s = jnp.dot(qc, ktc, preferred_element_type=F32) * dmat
            tot = jnp.dot(s.astype(BF16), v1, preferred_element_type=F32)
            if c > 0:
                inter = jnp.exp2(mp_ref[cs, :LANES] - gcol)
                tot = tot + jnp.concatenate([inter, inter, inter], axis=1) \
                    * jnp.dot(qc, state.astype(BF16), preferred_element_type=F32)
            den = tot[:, M_HEAD_DIM:]
            rinv = 1.0 / jnp.maximum(jnp.abs(den), jnp.exp2(-mcol))
            hval = tot[:, :M_HEAD_DIM] * jnp.concatenate([rinv, rinv], axis=1)

            hg = hval * o_ref[rs, hs].astype(F32)
            ms = jnp.mean(hg * hg, axis=1, keepdims=True)
            hn = hg * lax.rsqrt(ms + NORM_EPS)
            out_ref[rs, hs] = (hn * z_ref[rs, hs].astype(F32)).astype(BF16)

            if c + 1 < N_CHUNKS:
                ktw = (ktc.astype(F32) * w_ref[cs, :]).astype(BF16)
                dec = dec_ref[cs, :]
                dec = jnp.concatenate([dec, dec[:, :LANES]], axis=1)
                states[hh] = dec * state + jnp.dot(ktw, v1, preferred_element_type=F32)


def _mlstm(mall, kt, gate_stats):
    blk = (SEQ, M_HPS * M_HEAD_DIM)
    groups = M_HEADS // M_HPS

    def seg(k):
        return pl.BlockSpec(blk, lambda b, h, k=k: (b, k * groups + h))

    stat = pl.BlockSpec((M_HPS * N_CHUNKS, M_CHUNK), lambda b, h: (b * groups + h, 0))
    return pl.pallas_call(
        _mlstm_kernel,
        grid=(BATCH, groups),
        in_specs=[
            seg(0),
            pl.BlockSpec((1, M_HPS, N_CHUNKS, M_HEAD_DIM, M_CHUNK), lambda b, h: (b, h, 0, 0, 0)),
            seg(2), seg(3), seg(4),
        ] + [stat] * 6,
        out_specs=pl.BlockSpec(blk, lambda b, h: (b, h)),
        out_shape=jax.ShapeDtypeStruct((BATCH * SEQ, M_WIDTH), BF16),
        compiler_params=pltpu.CompilerParams(dimension_semantics=("parallel", "parallel"),
                                             vmem_limit_bytes=VMEM_LIMIT),
        name="mlstm",
    )(mall, kt, mall, mall, mall, *gate_stats)


GATHERED = (2,)
GATHER_PITCH = A_BLOCK + 8
ATTN_WAVE = 12


def _attn_block_order():
    order = [(r, 2) for r in range(A_GROUPS[2][1])]
    nblk1 = SEQ // A_GROUPS[1][1] // A_BLOCK
    per_span = SEQ // A_BLOCK // nblk1
    for n in range(nblk1):
        order += [(r * nblk1 + n, 1) for r in range(A_GROUPS[1][1])]
        order += [(n * per_span + i, 0) for i in range(per_span)]
    return order


def _attn_kernel(q0, k0, v0, q1, k1, v1, q2, k2, v2, out_ref,
                 a0, a1, a2, m0, m1, m2, l0, l1, l2):
    BLK = A_BLOCK
    ti = lax.broadcasted_iota(jnp.int32, (BLK, BLK), 0)
    ji = lax.broadcasted_iota(jnp.int32, (BLK, BLK), 1)
    cur_ok = ji <= ti
    ti2 = lax.broadcasted_iota(jnp.int32, (BLK, 2 * BLK), 0)
    krel = lax.broadcasted_iota(jnp.int32, (BLK, 2 * BLK), 1) - BLK
    band_ok = jnp.logical_and(krel <= ti2, krel >= ti2 - BLK)

    qs = (q0, q1, q2)
    ks = (k0, k1, k2)
    vs = (v0, v1, v2)
    accs = (a0, a1, a2)
    ms = (m0, m1, m2)
    ls = (l0, l1, l2)

    RB = 256

    def merge(rb):
        sl = slice(rb * RB, (rb + 1) * RB)

        def rows(refs, g):
            if g in GATHERED:
                dil = A_GROUPS[g][1]
                j0 = rb * RB // dil
                return jnp.concatenate([refs[g][pl.ds(j0 + jj, dil, stride=GATHER_PITCH), :]
                                        for jj in range(RB // dil)], axis=0)
            return refs[g][sl, :]

        mg = [rows(ms, g) for g in range(N_GROUPS)]
        mmax = jnp.maximum(jnp.maximum(mg[0], mg[1]), mg[2])
        num = None
        den = None
        for g in range(N_GROUPS):
            e = jnp.exp2(mg[g] - mmax)
            num = e * rows(accs, g) if num is None else num + e * rows(accs, g)
            den = e * rows(ls, g) if den is None else den + e * rows(ls, g)
        out_ref[sl, :] = (num * (1.0 / den)).astype(BF16)

    blocks = _attn_block_order()
    done_rows = [set() for _ in range(N_GROUPS)]
    merged = 0
    for w0 in range(0, len(blocks), ATTN_WAVE):
        wave = []
        for i, g in blocks[w0:w0 + ATTN_WAVE]:
            dil = A_GROUPS[g][1]
            r, n = divmod(i, SEQ // dil // BLK)
            rows = slice(i * BLK, (i + 1) * BLK)
            keys = slice((i - 1) * BLK, (i + 1) * BLK) if n > 0 else rows
            valid = band_ok if n > 0 else cur_ok
            s = lax.dot_general(qs[g][0, 0, rows, :], ks[g][0, 0, keys, :], (((1,), (1,)), ((), ())),
                                preferred_element_type=F32)
            s = jnp.where(valid, s, NEG_INF)
            m = jnp.max(s, axis=1, keepdims=True)
            tok0 = n * BLK * dil + r
            if g in GATHERED:
                idx = pl.ds(r * GATHER_PITCH, BLK)
            else:
                idx = pl.ds(tok0, BLK, stride=dil) if dil > 1 else pl.ds(tok0, BLK)
            wave.append((g, keys, valid, idx, s, m))
        for g, keys, valid, idx, s, m in wave:
            p = jnp.where(valid, jnp.exp2(s - m), 0.0)
            nkeys = keys.stop - keys.start
            v1 = jnp.concatenate([vs[g][0, 0, keys, :], jnp.ones((nkeys, LANES), BF16)], axis=1)
            acc = jnp.dot(p.astype(BF16), v1, preferred_element_type=F32)
            accs[g][idx, :] = acc[:, :LANES]
            ms[g][idx, :] = jnp.broadcast_to(m, (BLK, LANES))
            ls[g][idx, :] = acc[:, LANES:]
        for i, g in blocks[w0:w0 + ATTN_WAVE]:
            dil = A_GROUPS[g][1]
            r, n = divmod(i, SEQ // dil // BLK)
            span = BLK * dil
            if r == dil - 1:
                done_rows[g].update(range(n * span // RB, (n + 1) * span // RB))
        while merged < SEQ // RB and all(merged in d for d in done_rows):
            merge(merged)
            merged += 1


def _attn(qkv):
    in_specs = []
    args = []
    for g in range(N_GROUPS):
        for t in range(3):
            in_specs.append(pl.BlockSpec((1, 1, SEQ, A_HEAD_DIM), lambda b, h, t=t: (b, t, 0, h)))
            args.append(qkv[g])
    return pl.pallas_call(
        _attn_kernel,
        grid=(BATCH, A_HEADS),
        in_specs=in_specs,
        out_specs=pl.BlockSpec((SEQ, A_HEAD_DIM), lambda b, h: (b, h)),
        out_shape=jax.ShapeDtypeStruct((BATCH * SEQ, A_OUT_WIDTH), BF16),
        scratch_shapes=[pltpu.VMEM((A_GROUPS[g][1] * GATHER_PITCH if g in GATHERED else SEQ, LANES), F32)
                        for _ in range(3) for g in range(N_GROUPS)],
        compiler_params=pltpu.CompilerParams(dimension_semantics=("parallel", "parallel"),
                                             vmem_limit_bytes=VMEM_LIMIT),
        name="attn",
    )(*args)


O_TM = 1024


def _out_kernel(x_ref, prew_ref, hm_ref, att_ref, wg_ref, wpm_ref, wpa_ref, wout_ref, postw_ref,
                y_ref, wgb_ref):
    @pl.when(pl.program_id(0) == 0)
    def _():
        wgb_ref[...] = wg_ref[...].astype(BF16)

    x = x_ref[...]
    ms = jnp.mean(x * x, axis=-1, keepdims=True)
    h = (x * lax.rsqrt(ms + NORM_EPS) * prew_ref[...]).astype(BF16)
    gates = _dot_nt(h, wgb_ref[...])
    za = _silu(gates[:, :A_OUT_WIDTH])
    sgm = _sigmoid(gates[:, A_OUT_WIDTH:A_OUT_WIDTH + D_MODEL])
    sga = _sigmoid(gates[:, A_OUT_WIDTH + D_MODEL:])
    a = (att_ref[...].astype(F32) * za).astype(BF16)
    bm = jnp.dot(hm_ref[...], wpm_ref[...], preferred_element_type=F32)
    ba = jnp.dot(a, wpa_ref[...], preferred_element_type=F32)
    merged = (sgm * bm + sga * ba).astype(BF16)
    y = jnp.dot(merged, wout_ref[...], preferred_element_type=F32)
    ms2 = jnp.mean(y * y, axis=-1, keepdims=True)
    y_ref[...] = x + y * lax.rsqrt(ms2 + NORM_EPS) * postw_ref[...]


def _out(x2d, pre_w, hm, att, wt, wpm, wpa, wout, post_w):
    n = x2d.shape[0]
    const = lambda i: (0, 0)
    return pl.pallas_call(
        _out_kernel,
        grid=(n // O_TM,),
        in_specs=[
            pl.BlockSpec((O_TM, D_MODEL), lambda i: (i, 0)),
            pl.BlockSpec((1, D_MODEL), const),
            pl.BlockSpec((O_TM, M_WIDTH), lambda i: (i, 0)),
            pl.BlockSpec((O_TM, A_OUT_WIDTH), lambda i: (i, 0)),
            pl.BlockSpec((pl.Element(IN_WIDTH - OFF_ZA), pl.Element(D_MODEL)), lambda i: (OFF_ZA, 0),
                         pipeline_mode=pl.Buffered(1)),
            pl.BlockSpec((M_WIDTH, D_MODEL), const),
            pl.BlockSpec((A_OUT_WIDTH, D_MODEL), const),
            pl.BlockSpec((D_MODEL, D_MODEL), const),
            pl.BlockSpec((1, D_MODEL), const),
        ],
        out_specs=pl.BlockSpec((O_TM, D_MODEL), lambda i: (i, 0)),
        out_shape=jax.ShapeDtypeStruct((n, D_MODEL), F32),
        scratch_shapes=[pltpu.VMEM((IN_WIDTH - OFF_ZA, D_MODEL), BF16)],
        compiler_params=pltpu.CompilerParams(dimension_semantics=("arbitrary",),
                                             vmem_limit_bytes=VMEM_LIMIT),
        name="outproj",
    )(x2d, pre_w.reshape(1, D_MODEL), hm, att, wt, wpm, wpa, wout, post_w.reshape(1, D_MODEL))


def _rope_tables():
    pos = np.arange(SEQ, dtype=np.float64)
    inv_freq = ROPE_THETA ** (-np.arange(0, ROPE_DIM, 2, dtype=np.float64) / ROPE_DIM)
    ang = pos[:, None] * inv_freq[None, :]
    cos, sin = np.cos(ang), np.sin(ang)
    half = ROPE_DIM // 2
    pad = A_HEAD_DIM - ROPE_DIM
    cos_t = np.concatenate([cos, cos, np.ones((SEQ, pad))], axis=1)
    sin_up = np.concatenate([np.zeros((SEQ, half)), sin, np.zeros((SEQ, pad))], axis=1)
    sin_dn = np.concatenate([-sin, np.zeros((SEQ, half + pad))], axis=1)
    tab = np.stack([cos_t, sin_up, sin_dn]).astype(np.float32)
    out = []
    for _, dil in A_GROUPS:
        ls = SEQ // dil
        out.append(tab.reshape(3, ls, dil, A_HEAD_DIM).transpose(0, 2, 1, 3).reshape(3, SEQ, A_HEAD_DIM))
    return jnp.asarray(np.stack(out))


def _layer(x, pre_w, w_in, b_if, conv_w, conv_b, m_norm_w, w_pm, w_pa, w_out, post_w):
    x2d = x.reshape(BATCH * SEQ, D_MODEL)
    wt = w_in.T
    wg = jnp.pad(wt[OFF_GATES:OFF_QA], ((0, LANES - 2 * M_HEADS), (0, 0))).astype(BF16)
    tabs = _rope_tables()

    mall, kt, gt, h = _mproj(x2d, pre_w, wt, wg, b_if.reshape(2 * M_HEADS, 1), conv_w,
                             conv_b.reshape(1, 2 * M_WIDTH))
    qkv = [_aproj(h, wt, tabs[g], g) for g in range(N_GROUPS)]
    hm = _mlstm(mall, kt, _gates(gt))
    att = _attn(qkv)
    w_pm_n = (m_norm_w[:, None] * w_pm).astype(BF16)
    y = _out(x2d, pre_w, hm, att, wt, w_pm_n, w_pa.astype(BF16), w_out.astype(BF16), post_w)
    return y.reshape(BATCH, SEQ, D_MODEL)


@jax.jit
def kernel(x, pre_norm_w, w_in, b_if, conv_w, conv_b, m_norm_w, w_proj_m, w_proj_a, w_out, post_norm_w):
    for layer in range(pre_norm_w.shape[0]):
        x = _layer(x, pre_norm_w[layer], w_in[layer], b_if[layer], conv_w[layer], conv_b[layer],
                   m_norm_w[layer], w_proj_m[layer], w_proj_a[layer], w_out[layer], post_norm_w[layer])
    return x
```

```python
import functools

import jax
import jax.numpy as jnp
import numpy as np
from jax import lax
from jax.experimental import pallas as pl
from jax.experimental.pallas import tpu as pltpu

D_MODEL = 1024
BATCH = 8
SEQ = 2048
M_WIDTH = 1024
M_HEADS = 4
M_HEAD_DIM = 256
CONV_WIDTH = 4
A_GROUPS = ((128, 1), (512, 4), (2048, 16))
N_GROUPS = 3
A_HEADS = 4
A_HEAD_DIM = 128
A_OUT_WIDTH = 512
A_QKV_WIDTH = 1536
A_BLOCK = 128
ROPE_DIM = 32
ROPE_THETA = 500000.0
NORM_EPS = 1e-6
NEG_INF = -1e30
LOG2E = 1.4426950408889634

OFF_GATES = 5 * M_WIDTH
OFF_QA = OFF_GATES + 2 * M_HEADS
OFF_KA = OFF_QA + A_QKV_WIDTH
OFF_VA = OFF_KA + A_QKV_WIDTH
OFF_ZA = OFF_VA + A_QKV_WIDTH
OFF_GM = OFF_ZA + A_OUT_WIDTH
IN_WIDTH = OFF_GM + 2 * D_MODEL

LANES = 128
M_CHUNK = 256
N_CHUNKS = SEQ // M_CHUNK
V7X_VMEM_BYTES = 64 * 1024 * 1024
VMEM_LIMIT = V7X_VMEM_BYTES * 3 // 4

F32 = jnp.float32
BF16 = jnp.bfloat16


def _dot_nt(a, bt):
    return lax.dot_general(a, bt, (((1,), (1,)), ((), ())), preferred_element_type=F32)


def _sigmoid(x):
    return 1.0 / (1.0 + jnp.exp(-x))


def _silu(x):
    return x * _sigmoid(x)


M_TN = 512


SUBLANES = 8


def _shift_rows(x3, s, sub, carry):
    rot = pltpu.roll(x3, s, axis=1)
    prev = jnp.concatenate([carry, rot[:-1]], axis=0)
    return jnp.where(sub < s, prev, rot), rot[-1:]


def _conv_tile(h_ref, w, cw, cb, emit):
    cols = w.shape[0]
    sub = lax.broadcasted_iota(jnp.int32, (1, SUBLANES, cols), 1)
    zero = jnp.zeros((1, SUBLANES, cols), F32)
    carry = [zero, zero]

    def matmul(p):
        return _dot_nt(h_ref[p * M_CHUNK:(p + 1) * M_CHUNK, :], w)

    def conv(acc):
        x3 = acc.reshape(M_CHUNK // SUBLANES, SUBLANES, cols)
        d1, carry[0] = _shift_rows(x3, 1, sub, carry[0])
        near = x3 * cw[3] + d1 * cw[2] + cb[0]
        far = x3 * cw[1] + d1 * cw[0]
        d2, carry[1] = _shift_rows(far, 2, sub, carry[1])
        return (near + d2).reshape(M_CHUNK, cols)

    acc = matmul(0)
    for p in range(N_CHUNKS):
        nxt = matmul(p + 1) if p + 1 < N_CHUNKS else None
        emit(p, conv(acc))
        acc = nxt


M_TILES = 5 * M_WIDTH // M_TN


def _mproj_kernel(x_ref, prew_ref, w_ref, wg_ref, bif_ref, cw_ref, cb_ref,
                  mall_ref, kt_ref, gt_ref, h_ref):
    j = pl.program_id(1)

    def matmul():
        return _dot_nt(h_ref[...], w_ref[...].astype(BF16))

    @pl.when(j == 0)
    def _():
        x = x_ref[...]
        ms = jnp.mean(x * x, axis=-1, keepdims=True)
        h_ref[...] = (x * lax.rsqrt(ms + NORM_EPS) * prew_ref[...]).astype(BF16)
        g = _dot_nt(h_ref[...], wg_ref[...])
        gt = g.T[:2 * M_HEADS, :] + bif_ref[...]
        for c in range(N_CHUNKS):
            gt_ref[0, :, c, :] = gt[:, c * M_CHUNK:(c + 1) * M_CHUNK]

    @pl.when(j < 2)
    def _():
        def emit(p, y):
            mall_ref[p * M_CHUNK:(p + 1) * M_CHUNK, :] = _silu(y).astype(BF16)
        _conv_tile(h_ref, w_ref[...].astype(BF16), cw_ref[...], cb_ref[...], emit)

    @pl.when(jnp.logical_and(j >= 2, j < 4))
    def _():
        def emit(p, y):
            y = _silu(y) * (M_HEAD_DIM ** -0.5)
            mall_ref[p * M_CHUNK:(p + 1) * M_CHUNK, :] = y.astype(BF16)
            for hh in range(M_TN // M_HEAD_DIM):
                kt_ref[0, hh, p] = y[:, hh * M_HEAD_DIM:(hh + 1) * M_HEAD_DIM].T.astype(BF16)
        _conv_tile(h_ref, w_ref[...].astype(BF16), cw_ref[...], cb_ref[...], emit)

    @pl.when(jnp.logical_and(j >= 4, j < 6))
    def _():
        mall_ref[...] = matmul().astype(BF16)

    @pl.when(jnp.logical_and(j >= 6, j < 8))
    def _():
        mall_ref[...] = _sigmoid(matmul()).astype(BF16)

    @pl.when(j >= 8)
    def _():
        mall_ref[...] = _silu(matmul()).astype(BF16)


def _mproj(x2d, pre_w, wt, wg, bif, conv_w, conv_b):
    return pl.pallas_call(
        _mproj_kernel,
        grid=(BATCH, M_TILES),
        in_specs=[
            pl.BlockSpec((SEQ, D_MODEL), lambda i, j: (i, 0)),
            pl.BlockSpec((1, D_MODEL), lambda i, j: (0, 0)),
            pl.BlockSpec((M_TN, D_MODEL), lambda i, j: (j, 0)),
            pl.BlockSpec((LANES, D_MODEL), lambda i, j: (0, 0)),
            pl.BlockSpec((2 * M_HEADS, 1), lambda i, j: (0, 0)),
            pl.BlockSpec((CONV_WIDTH, M_TN), lambda i, j: (0, jnp.minimum(j, 3))),
            pl.BlockSpec((1, M_TN), lambda i, j: (0, jnp.minimum(j, 3))),
        ],
        out_specs=[
            pl.BlockSpec((SEQ, M_TN), lambda i, j: (i, j)),
            pl.BlockSpec((1, M_TN // M_HEAD_DIM, N_CHUNKS, M_HEAD_DIM, M_CHUNK),
                         lambda i, j: (i, jnp.clip(j - 2, 0, 1), 0, 0, 0)),
            pl.BlockSpec((1, 2 * M_HEADS, N_CHUNKS, M_CHUNK), lambda i, j: (i, 0, 0, 0)),
            pl.BlockSpec((SEQ, D_MODEL), lambda i, j: (i, 0)),
        ],
        out_shape=[
            jax.ShapeDtypeStruct((BATCH * SEQ, 5 * M_WIDTH), BF16),
            jax.ShapeDtypeStruct((BATCH, M_HEADS, N_CHUNKS, M_HEAD_DIM, M_CHUNK), BF16),
            jax.ShapeDtypeStruct((BATCH, 2 * M_HEADS, N_CHUNKS, M_CHUNK), F32),
            jax.ShapeDtypeStruct((BATCH * SEQ, D_MODEL), BF16),
        ],
        compiler_params=pltpu.CompilerParams(dimension_semantics=("parallel", "arbitrary"),
                                             vmem_limit_bytes=VMEM_LIMIT),
        name="mproj",
    )(x2d, pre_w.reshape(1, D_MODEL), wt, wg, bif, conv_w, conv_b)


A_TILES = 3
EASY_STRIDE = 4


def _aproj_kernel(dil, h_ref, w_ref, tab_ref, o_ref, acc0_ref, acc1_ref, tmp_ref):
    s = pl.program_id(0)
    last = BATCH * A_TILES
    pt = (jnp.maximum(s, 1) - 1) % A_TILES

    accs = (acc0_ref, acc1_ref)

    def matmul(slot):
        res = _dot_nt(h_ref[...], w_ref[...].astype(BF16))
        for hd in range(A_HEADS):
            accs[slot][hd] = res[:, hd * A_HEAD_DIM:(hd + 1) * A_HEAD_DIM]

    ls = SEQ // dil

    two_pass = dil == EASY_STRIDE * EASY_STRIDE

    def stage(slot, hd):
        if two_pass:
            n4 = SEQ // EASY_STRIDE
            for r4 in range(EASY_STRIDE):
                tmp_ref[hd, r4 * n4:(r4 + 1) * n4, :] = accs[slot][hd, pl.ds(r4, n4, stride=EASY_STRIDE), :]

    def residue_rows(slot, hd, r):
        if dil == 1:
            return accs[slot][hd]
        if two_pass:
            q, r4 = divmod(r, EASY_STRIDE)
            return tmp_ref[hd, pl.ds(r4 * (SEQ // EASY_STRIDE) + q, ls, stride=EASY_STRIDE), :]
        return accs[slot][hd, pl.ds(r, ls, stride=dil), :]

    def rope_epilogue(slot):
        scale = jnp.where(pt == 0, A_HEAD_DIM ** -0.5 * LOG2E, 1.0).astype(F32)
        half = ROPE_DIM // 2
        for hd in range(A_HEADS):
            stage(slot, hd)
            for r in range(dil):
                rows = slice(r * ls, (r + 1) * ls)
                xh = residue_rows(slot, hd, r)
                y = (xh * tab_ref[0, rows, :] + pltpu.roll(xh, half, axis=1) * tab_ref[1, rows, :]
                     + pltpu.roll(xh, A_HEAD_DIM - half, axis=1) * tab_ref[2, rows, :])
                o_ref[0, 0, rows, hd * A_HEAD_DIM:(hd + 1) * A_HEAD_DIM] = (y * scale).astype(BF16)

    def cast_epilogue(slot):
        for hd in range(A_HEADS):
            stage(slot, hd)
            for r in range(dil):
                rows = slice(r * ls, (r + 1) * ls)
                o_ref[0, 0, rows, hd * A_HEAD_DIM:(hd + 1) * A_HEAD_DIM] = \
                    residue_rows(slot, hd, r).astype(BF16)

    mid = jnp.logical_and(s > 0, s < last)

    @pl.when(s == 0)
    def _():
        matmul(0)

    for slot in range(2):
        mine = jnp.logical_and(mid, s % 2 == slot)

        @pl.when(jnp.logical_and(mine, pt < 2))
        def _(slot=slot):
            rope_epilogue(1 - slot)
            matmul(slot)

        @pl.when(jnp.logical_and(mine, pt == 2))
        def _(slot=slot):
            cast_epilogue(1 - slot)
            matmul(slot)

    @pl.when(s == last)
    def _():
        cast_epilogue((last - 1) % 2)


def _aproj(h, wt, tabs_g, g):
    dil = A_GROUPS[g][1]

    def w_row(s):
        return OFF_QA + (cur(s) % A_TILES) * A_QKV_WIDTH + g * A_OUT_WIDTH
    last = BATCH * A_TILES

    def cur(s):
        return jnp.minimum(s, last - 1)

    def prev(s):
        return jnp.maximum(s, 1) - 1

    return pl.pallas_call(
        functools.partial(_aproj_kernel, dil),
        grid=(last + 1,),
        in_specs=[
            pl.BlockSpec((SEQ, D_MODEL), lambda s: (cur(s) // A_TILES, 0)),
            pl.BlockSpec((pl.Element(A_OUT_WIDTH), pl.Element(D_MODEL)),
                         lambda s: (pl.multiple_of(w_row(s), SUBLANES), 0)),
            pl.BlockSpec((3, SEQ, LANES), lambda s: (0, 0, 0)),
        ],
        out_specs=pl.BlockSpec((1, 1, SEQ, A_OUT_WIDTH),
                               lambda s: (prev(s) // A_TILES, prev(s) % A_TILES, 0, 0)),
        out_shape=jax.ShapeDtypeStruct((BATCH, A_TILES, SEQ, A_OUT_WIDTH), BF16),
        scratch_shapes=[pltpu.VMEM((A_HEADS, SEQ, A_HEAD_DIM), F32) for _ in range(3)],
        compiler_params=pltpu.CompilerParams(dimension_semantics=("arbitrary",),
                                             vmem_limit_bytes=VMEM_LIMIT),
        name="aproj_d%d" % dil,
    )(h, wt, tabs_g)


def _lane_scan(x, op, fill):
    lane = lax.broadcasted_iota(jnp.int32, x.shape, 1)
    sh = 1
    while sh < x.shape[1]:
        shifted = jnp.where(lane >= sh, pltpu.roll(x, sh, axis=1), fill)
        x = op(x, shifted)
        sh *= 2
    return x


def _gates_kernel(ig_ref, fg_ref, u_ref, g_ref, mt_ref, mp_ref, w_ref, dec_ref):
    L = M_CHUNK
    ig = ig_ref[...]
    fg = fg_ref[...]
    logf = -(jnp.maximum(-fg, 0.0) + jnp.log1p(jnp.exp(-jnp.abs(fg))))
    b = _lane_scan(logf, jnp.add, 0.0)
    u = ig - b
    cm = _lane_scan(u, jnp.maximum, NEG_INF)
    lane = lax.broadcasted_iota(jnp.int32, ig.shape, 1)
    btot = jnp.broadcast_to(jnp.sum(jnp.where(lane == L - 1, b, 0.0), axis=1, keepdims=True), ig.shape)
    umax = jnp.broadcast_to(jnp.max(cm, axis=1, keepdims=True), ig.shape)
    chunk = lax.broadcasted_iota(jnp.int32, ig.shape, 0) % N_CHUNKS
    m = jnp.zeros(ig.shape, F32)
    for c in range(N_CHUNKS - 1):
        nxt = pltpu.roll(btot + jnp.maximum(m, umax), 1, axis=0)
        m = jnp.where(chunk == c + 1, nxt, m)
    g = jnp.maximum(m, cm)
    glast = jnp.max(g, axis=1, keepdims=True)
    u_ref[...] = u * LOG2E
    g_ref[...] = g * LOG2E
    mt_ref[...] = (b + g) * LOG2E
    mp_ref[...] = m * LOG2E
    w_ref[...] = jnp.exp(u - glast)
    dec_ref[...] = jnp.broadcast_to(jnp.exp(m - glast), ig.shape)


def _gates(gt):
    rows = BATCH * M_HEADS * N_CHUNKS
    ig = gt[:, :M_HEADS].reshape(rows, M_CHUNK)
    fg = gt[:, M_HEADS:].reshape(rows, M_CHUNK)
    spec = pl.BlockSpec((rows, M_CHUNK), lambda: (0, 0))
    return pl.pallas_call(
        _gates_kernel,
        in_specs=[spec, spec],
        out_specs=[spec] * 6,
        out_shape=[jax.ShapeDtypeStruct((rows, M_CHUNK), F32)] * 6,
        name="gates",
    )(ig, fg)


M_HPS = 2


def _mlstm_kernel(q_ref, kt_ref, v_ref, o_ref, z_ref, u_ref, g_ref, mt_ref, mp_ref, w_ref, dec_ref,
                  out_ref):
    L = M_CHUNK
    ti = lax.broadcasted_iota(jnp.int32, (L, L), 0)
    si = lax.broadcasted_iota(jnp.int32, (L, L), 1)
    causal = ti >= si
    ones = jnp.ones((L, LANES), BF16)

    states = [jnp.zeros((M_HEAD_DIM, M_HEAD_DIM + LANES), F32) for _ in range(M_HPS)]
    for c in range(N_CHUNKS):
        rs = slice(c * L, (c + 1) * L)
        for hh in range(M_HPS):
            hs = slice(hh * M_HEAD_DIM, (hh + 1) * M_HEAD_DIM)
            cs = slice(hh * N_CHUNKS + c, hh * N_CHUNKS + c + 1)
            state = states[hh]
            qc = q_ref[rs, hs]
            v1 = jnp.concatenate([v_ref[rs, hs], ones], axis=1)
            ktc = kt_ref[0, hh, c]
            u_r = u_ref[cs, :]
            gcol = jnp.broadcast_to(g_ref[cs, :], (LANES, L)).T
            mcol = jnp.broadcast_to(mt_ref[cs, :], (LANES, L)).T
            dmat = jnp.where(causal, jnp.exp2(u_r - jnp.concatenate([gcol, gcol], axis=1)), 0.0)

            s = jnp.dot(qc, ktc, preferred_element_type=F32) * dmat
            tot = jnp.dot(s.astype(BF16), v1, preferred_element_type=F32)
            if c > 0:
                inter = jnp.exp2(mp_ref[cs, :LANES] - gcol)
                tot = tot + jnp.concatenate([inter, inter, inter], axis=1) \
                    * jnp.dot(qc, state.astype(BF16), preferred_element_type=F32)
            den = tot[:, M_HEAD_DIM:]
            rinv = 1.0 / jnp.maximum(jnp.abs(den), jnp.exp2(-mcol))
            hval = tot[:, :M_HEAD_DIM] * jnp.concatenate([rinv, rinv], axis=1)

            hg = hval * o_ref[rs, hs].astype(F32)
            ms = jnp.mean(hg * hg, axis=1, keepdims=True)
            hn = hg * lax.rsqrt(ms + NORM_EPS)
            out_ref[rs, hs] = (hn * z_ref[rs, hs].astype(F32)).astype(BF16)

            if c + 1 < N_CHUNKS:
                ktw = (ktc.astype(F32) * w_ref[cs, :]).astype(BF16)
                dec = dec_ref[cs, :]
                dec = jnp.concatenate([dec, dec[:, :LANES]], axis=1)
                states[hh] = dec * state + jnp.dot(ktw, v1, preferred_element_type=F32)


def _mlstm(mall, kt, gate_stats):
    blk = (SEQ, M_HPS * M_HEAD_DIM)
    groups = M_HEADS // M_HPS

    def seg(k):
        return pl.BlockSpec(blk, lambda b, h, k=k: (b, k * groups + h))

    stat = pl.BlockSpec((M_HPS * N_CHUNKS, M_CHUNK), lambda b, h: (b * groups + h, 0))
    return pl.pallas_call(
        _mlstm_kernel,
        grid=(BATCH, groups),
        in_specs=[
            seg(0),
            pl.BlockSpec((1, M_HPS, N_CHUNKS, M_HEAD_DIM, M_CHUNK), lambda b, h: (b, h, 0, 0, 0)),
            seg(2), seg(3), seg(4),
        ] + [stat] * 6,
        out_specs=pl.BlockSpec(blk, lambda b, h: (b, h)),
        out_shape=jax.ShapeDtypeStruct((BATCH * SEQ, M_WIDTH), BF16),
        compiler_params=pltpu.CompilerParams(dimension_semantics=("parallel", "parallel"),
                                             vmem_limit_bytes=VMEM_LIMIT),
        name="mlstm",
    )(mall, kt, mall, mall, mall, *gate_stats)


GATHERED = (2,)
GATHER_PITCH = A_BLOCK + 8
ATTN_WAVE = 12


def _attn_block_order():
    order = [(r, 2) for r in range(A_GROUPS[2][1])]
    nblk1 = SEQ // A_GROUPS[1][1] // A_BLOCK
    per_span = SEQ // A_BLOCK // nblk1
    for n in range(nblk1):
        order += [(r * nblk1 + n, 1) for r in range(A_GROUPS[1][1])]
        order += [(n * per_span + i, 0) for i in range(per_span)]
    return order


A_HPS = 2


def _attn_kernel(q0, k0, v0, q1, k1, v1, q2, k2, v2, out_ref, *scratch):
    BLK = A_BLOCK
    ti = lax.broadcasted_iota(jnp.int32, (BLK, BLK), 0)
    ji = lax.broadcasted_iota(jnp.int32, (BLK, BLK), 1)
    cur_ok = ji <= ti
    ti2 = lax.broadcasted_iota(jnp.int32, (BLK, 2 * BLK), 0)
    krel = lax.broadcasted_iota(jnp.int32, (BLK, 2 * BLK), 1) - BLK
    band_ok = jnp.logical_and(krel <= ti2, krel >= ti2 - BLK)

    qs = (q0, q1, q2)
    ks = (k0, k1, k2)
    vs = (v0, v1, v2)
    per_head = 3 * N_GROUPS
    accs = [scratch[hh * per_head:hh * per_head + N_GROUPS] for hh in range(A_HPS)]
    ms = [scratch[hh * per_head + N_GROUPS:hh * per_head + 2 * N_GROUPS] for hh in range(A_HPS)]
    ls = [scratch[hh * per_head + 2 * N_GROUPS:(hh + 1) * per_head] for hh in range(A_HPS)]

    RB = 256

    def merge(rb, hh):
        sl = slice(rb * RB, (rb + 1) * RB)

        def rows(refs, g):
            if g in GATHERED:
                dil = A_GROUPS[g][1]
                j0 = rb * RB // dil
                return jnp.concatenate([refs[g][pl.ds(j0 + jj, dil, stride=GATHER_PITCH), :]
                                        for jj in range(RB // dil)], axis=0)
            return refs[g][sl, :]

        hs = slice(hh * A_HEAD_DIM, (hh + 1) * A_HEAD_DIM)
        mg = [rows(ms[hh], g) for g in range(N_GROUPS)]
        mmax = jnp.maximum(jnp.maximum(mg[0], mg[1]), mg[2])
        num = None
        den = None
        for g in range(N_GROUPS):
            e = jnp.exp2(mg[g] - mmax)
            num = e * rows(accs[hh], g) if num is None else num + e * rows(accs[hh], g)
            den = e * rows(ls[hh], g) if den is None else den + e * rows(ls[hh], g)
        out_ref[sl, hs] = (num * (1.0 / den)).astype(BF16)

    blocks = [(i, g, hh) for i, g in _attn_block_order() for hh in range(A_HPS)]
    done_rows = [[set() for _ in range(N_GROUPS)] for _ in range(A_HPS)]
    merged = [0] * A_HPS
    for w0 in range(0, len(blocks), ATTN_WAVE):
        wave = []
        for i, g, hh in blocks[w0:w0 + ATTN_WAVE]:
            hs = slice(hh * A_HEAD_DIM, (hh + 1) * A_HEAD_DIM)
            dil = A_GROUPS[g][1]
            r, n = divmod(i, SEQ // dil // BLK)
            rows = slice(i * BLK, (i + 1) * BLK)
            keys = slice((i - 1) * BLK, (i + 1) * BLK) if n > 0 else rows
            valid = band_ok if n > 0 else cur_ok
            s = lax.dot_general(qs[g][0, 0, rows, hs], ks[g][0, 0, keys, hs], (((1,), (1,)), ((), ())),
                                preferred_element_type=F32)
            s = jnp.where(valid, s, NEG_INF)
            m = jnp.max(s, axis=1, keepdims=True)
            tok0 = n * BLK * dil + r
            if g in GATHERED:
                idx = pl.ds(r * GATHER_PITCH, BLK)
            else:
                idx = pl.ds(tok0, BLK, stride=dil) if dil > 1 else pl.ds(tok0, BLK)
            wave.append((g, hh, hs, keys, valid, idx, s, m))
        for g, hh, hs, keys, valid, idx, s, m in wave:
            p = jnp.where(valid, jnp.exp2(s - m), 0.0)
            nkeys = keys.stop - keys.start
            v1 = jnp.concatenate([vs[g][0, 0, keys, hs], jnp.ones((nkeys, LANES), BF16)], axis=1)
            acc = jnp.dot(p.astype(BF16), v1, preferred_element_type=F32)
            accs[hh][g][idx, :] = acc[:, :LANES]
            ms[hh][g][idx, :] = jnp.broadcast_to(m, (BLK, LANES))
            ls[hh][g][idx, :] = acc[:, LANES:]
        for i, g, hh in blocks[w0:w0 + ATTN_WAVE]:
            dil = A_GROUPS[g][1]
            r, n = divmod(i, SEQ // dil // BLK)
            span = BLK * dil
            if r == dil - 1:
                done_rows[hh][g].update(range(n * span // RB, (n + 1) * span // RB))
        for hh in range(A_HPS):
            while merged[hh] < SEQ // RB and all(merged[hh] in d for d in done_rows[hh]):
                merge(merged[hh], hh)
                merged[hh] += 1


def _attn(qkv):
    in_specs = []
    args = []
    for g in range(N_GROUPS):
        for t in range(3):
            in_specs.append(pl.BlockSpec((1, 1, SEQ, A_HPS * A_HEAD_DIM), lambda b, h, t=t: (b, t, 0, h)))
            args.append(qkv[g])
    return pl.pallas_call(
        _attn_kernel,
        grid=(BATCH, A_HEADS // A_HPS),
        in_specs=in_specs,
        out_specs=pl.BlockSpec((SEQ, A_HPS * A_HEAD_DIM), lambda b, h: (b, h)),
        out_shape=jax.ShapeDtypeStruct((BATCH * SEQ, A_OUT_WIDTH), BF16),
        scratch_shapes=[pltpu.VMEM((A_GROUPS[g][1] * GATHER_PITCH if g in GATHERED else SEQ, LANES), F32)
                        for _ in range(A_HPS) for _ in range(3) for g in range(N_GROUPS)],
        compiler_params=pltpu.CompilerParams(dimension_semantics=("parallel", "parallel"),
                                             vmem_limit_bytes=VMEM_LIMIT),
        name="attn",
    )(*args)


O_TM = 1024


def _out_kernel(x_ref, prew_ref, hm_ref, att_ref, wg_ref, wpm_ref, wpa_ref, wout_ref, postw_ref,
                y_ref, wgb_ref):
    @pl.when(pl.program_id(0) == 0)
    def _():
        wgb_ref[...] = wg_ref[...].astype(BF16)

    x = x_ref[...]
    ms = jnp.mean(x * x, axis=-1, keepdims=True)
    h = (x * lax.rsqrt(ms + NORM_EPS) * prew_ref[...]).astype(BF16)
    gates = _dot_nt(h, wgb_ref[...])
    za = _silu(gates[:, :A_OUT_WIDTH])
    sgm = _sigmoid(gates[:, A_OUT_WIDTH:A_OUT_WIDTH + D_MODEL])
    sga = _sigmoid(gates[:, A_OUT_WIDTH + D_MODEL:])
    a = (att_ref[...].astype(F32) * za).astype(BF16)
    bm = jnp.dot(hm_ref[...], wpm_ref[...], preferred_element_type=F32)
    ba = jnp.dot(a, wpa_ref[...], preferred_element_type=F32)
    merged = (sgm * bm + sga * ba).astype(BF16)
    y = jnp.dot(merged, wout_ref[...], preferred_element_type=F32)
    ms2 = jnp.mean(y * y, axis=-1, keepdims=True)
    y_ref[...] = x + y * lax.rsqrt(ms2 + NORM_EPS) * postw_ref[...]


def _out(x2d, pre_w, hm, att, wt, wpm, wpa, wout, post_w):
    n = x2d.shape[0]
    const = lambda i: (0, 0)
    return pl.pallas_call(
        _out_kernel,
        grid=(n // O_TM,),
        in_specs=[
            pl.BlockSpec((O_TM, D_MODEL), lambda i: (i, 0)),
            pl.BlockSpec((1, D_MODEL), const),
            pl.BlockSpec((O_TM, M_WIDTH), lambda i: (i, 0)),
            pl.BlockSpec((O_TM, A_OUT_WIDTH), lambda i: (i, 0)),
            pl.BlockSpec((pl.Element(IN_WIDTH - OFF_ZA), pl.Element(D_MODEL)), lambda i: (OFF_ZA, 0),
                         pipeline_mode=pl.Buffered(1)),
            pl.BlockSpec((M_WIDTH, D_MODEL), const),
            pl.BlockSpec((A_OUT_WIDTH, D_MODEL), const),
            pl.BlockSpec((D_MODEL, D_MODEL), const),
            pl.BlockSpec((1, D_MODEL), const),
        ],
        out_specs=pl.BlockSpec((O_TM, D_MODEL), lambda i: (i, 0)),
        out_shape=jax.ShapeDtypeStruct((n, D_MODEL), F32),
        scratch_shapes=[pltpu.VMEM((IN_WIDTH - OFF_ZA, D_MODEL), BF16)],
        compiler_params=pltpu.CompilerParams(dimension_semantics=("arbitrary",),
                                             vmem_limit_bytes=VMEM_LIMIT),
        name="outproj",
    )(x2d, pre_w.reshape(1, D_MODEL), hm, att, wt, wpm, wpa, wout, post_w.reshape(1, D_MODEL))


def _rope_tables():
    pos = np.arange(SEQ, dtype=np.float64)
    inv_freq = ROPE_THETA ** (-np.arange(0, ROPE_DIM, 2, dtype=np.float64) / ROPE_DIM)
    ang = pos[:, None] * inv_freq[None, :]
    cos, sin = np.cos(ang), np.sin(ang)
    half = ROPE_DIM // 2
    pad = A_HEAD_DIM - ROPE_DIM
    cos_t = np.concatenate([cos, cos, np.ones((SEQ, pad))], axis=1)
    sin_up = np.concatenate([np.zeros((SEQ, half)), sin, np.zeros((SEQ, pad))], axis=1)
    sin_dn = np.concatenate([-sin, np.zeros((SEQ, half + pad))], axis=1)
    tab = np.stack([cos_t, sin_up, sin_dn]).astype(np.float32)
    out = []
    for _, dil in A_GROUPS:
        ls = SEQ // dil
        out.append(tab.reshape(3, ls, dil, A_HEAD_DIM).transpose(0, 2, 1, 3).reshape(3, SEQ, A_HEAD_DIM))
    return jnp.asarray(np.stack(out))


def _layer(x, pre_w, w_in, b_if, conv_w, conv_b, m_norm_w, w_pm, w_pa, w_out, post_w):
    x2d = x.reshape(BATCH * SEQ, D_MODEL)
    wt = w_in.T
    wg = jnp.pad(wt[OFF_GATES:OFF_QA], ((0, LANES - 2 * M_HEADS), (0, 0))).astype(BF16)
    tabs = _rope_tables()

    mall, kt, gt, h = _mproj(x2d, pre_w, wt, wg, b_if.reshape(2 * M_HEADS, 1), conv_w,
                             conv_b.reshape(1, 2 * M_WIDTH))
    qkv = [_aproj(h, wt, tabs[g], g) for g in range(N_GROUPS)]
    hm = _mlstm(mall, kt, _gates(gt))
    att = _attn(qkv)
    w_pm_n = (m_norm_w[:, None] * w_pm).astype(BF16)
    y = _out(x2d, pre_w, hm, att, wt, w_pm_n, w_pa.astype(BF16), w_out.astype(BF16), post_w)
    return y.reshape(BATCH, SEQ, D_MODEL)


@jax.jit
def kernel(x, pre_norm_w, w_in, b_if, conv_w, conv_b, m_norm_w, w_proj_m, w_proj_a, w_out, post_norm_w):
    for layer in range(pre_norm_w.shape[0]):
        x = _layer(x, pre_norm_w[layer], w_in[layer], b_if[layer], conv_w[layer], conv_b[layer],
                   m_norm_w[layer], w_proj_m[layer], w_proj_a[layer], w_out[layer], post_norm_w[layer])
    return x
```

```python
import functools

import jax
import jax.numpy as jnp
import numpy as np
from jax import lax
from jax.experimental import pallas as pl
from jax.experimental.pallas import tpu as pltpu

D_MODEL = 1024
BATCH = 8
SEQ = 2048
M_WIDTH = 1024
M_HEADS = 4
M_HEAD_DIM = 256
CONV_WIDTH = 4
A_GROUPS = ((128, 1), (512, 4), (2048, 16))
N_GROUPS = 3
A_HEADS = 4
A_HEAD_DIM = 128
A_OUT_WIDTH = 512
A_QKV_WIDTH = 1536
A_BLOCK = 128
ROPE_DIM = 32
ROPE_THETA = 500000.0
NORM_EPS = 1e-6
NEG_INF = -1e30
LOG2E = 1.4426950408889634

OFF_GATES = 5 * M_WIDTH
OFF_QA = OFF_GATES + 2 * M_HEADS
OFF_KA = OFF_QA + A_QKV_WIDTH
OFF_VA = OFF_KA + A_QKV_WIDTH
OFF_ZA = OFF_VA + A_QKV_WIDTH
OFF_GM = OFF_ZA + A_OUT_WIDTH
IN_WIDTH = OFF_GM + 2 * D_MODEL

LANES = 128
M_CHUNK = 256
N_CHUNKS = SEQ // M_CHUNK
V7X_VMEM_BYTES = 64 * 1024 * 1024
VMEM_LIMIT = V7X_VMEM_BYTES * 3 // 4

F32 = jnp.float32
BF16 = jnp.bfloat16


def _dot_nt(a, bt):
    return lax.dot_general(a, bt, (((1,), (1,)), ((), ())), preferred_element_type=F32)


def _sigmoid(x):
    return 1.0 / (1.0 + jnp.exp(-x))


def _silu(x):
    return x * _sigmoid(x)


M_TN = 512


SUBLANES = 8


def _shift_rows(x3, s, sub, carry):
    rot = pltpu.roll(x3, s, axis=1)
    prev = jnp.concatenate([carry, rot[:-1]], axis=0)
    return jnp.where(sub < s, prev, rot), rot[-1:]


def _conv_tile(h_ref, w, cw, cb, emit):
    cols = w.shape[0]
    sub = lax.broadcasted_iota(jnp.int32, (1, SUBLANES, cols), 1)
    zero = jnp.zeros((1, SUBLANES, cols), F32)
    carry = [zero, zero]

    def matmul(p):
        return _dot_nt(h_ref[p * M_CHUNK:(p + 1) * M_CHUNK, :], w)

    def conv(acc):
        x3 = acc.reshape(M_CHUNK // SUBLANES, SUBLANES, cols)
        d1, carry[0] = _shift_rows(x3, 1, sub, carry[0])
        near = x3 * cw[3] + d1 * cw[2] + cb[0]
        far = x3 * cw[1] + d1 * cw[0]
        d2, carry[1] = _shift_rows(far, 2, sub, carry[1])
        return (near + d2).reshape(M_CHUNK, cols)

    acc = matmul(0)
    for p in range(N_CHUNKS):
        nxt = matmul(p + 1) if p + 1 < N_CHUNKS else None
        emit(p, conv(acc))
        acc = nxt


M_TILES = 5 * M_WIDTH // M_TN


def _mproj_kernel(x_hbm, prew_ref, w_ref, wg_ref, bif_ref, cw_ref, cb_ref,
                  mall_ref, kt_ref, gt_ref, h_ref, xbuf_ref, xsem_ref):
    b = pl.program_id(0)
    j = pl.program_id(1)

    def x_copy(bb):
        rows = pl.ds(pl.multiple_of(bb * SEQ, SEQ), SEQ)
        return pltpu.make_async_copy(x_hbm.at[rows], xbuf_ref.at[bb % 2], xsem_ref.at[bb % 2])

    def matmul():
        return _dot_nt(h_ref[...], w_ref[...].astype(BF16))

    @pl.when(jnp.logical_and(b == 0, j == 0))
    def _():
        x_copy(b).start()

    @pl.when(j == 0)
    def _():
        x_copy(b).wait()

        @pl.when(b + 1 < pl.num_programs(0))
        def _():
            x_copy(b + 1).start()

        x = xbuf_ref[b % 2]
        ms = jnp.mean(x * x, axis=-1, keepdims=True)
        h_ref[...] = (x * lax.rsqrt(ms + NORM_EPS) * prew_ref[...]).astype(BF16)
        g = _dot_nt(h_ref[...], wg_ref[...])
        gt = g.T[:2 * M_HEADS, :] + bif_ref[...]
        for c in range(N_CHUNKS):
            gt_ref[0, :, c, :] = gt[:, c * M_CHUNK:(c + 1) * M_CHUNK]

    @pl.when(j < 2)
    def _():
        def emit(p, y):
            mall_ref[p * M_CHUNK:(p + 1) * M_CHUNK, :] = _silu(y).astype(BF16)
        _conv_tile(h_ref, w_ref[...].astype(BF16), cw_ref[...], cb_ref[...], emit)

    @pl.when(jnp.logical_and(j >= 2, j < 4))
    def _():
        def emit(p, y):
            y = _silu(y) * (M_HEAD_DIM ** -0.5)
            mall_ref[p * M_CHUNK:(p + 1) * M_CHUNK, :] = y.astype(BF16)
            for hh in range(M_TN // M_HEAD_DIM):
                kt_ref[0, hh, p] = y[:, hh * M_HEAD_DIM:(hh + 1) * M_HEAD_DIM].T.astype(BF16)
        _conv_tile(h_ref, w_ref[...].astype(BF16), cw_ref[...], cb_ref[...], emit)

    @pl.when(jnp.logical_and(j >= 4, j < 6))
    def _():
        mall_ref[...] = matmul().astype(BF16)

    @pl.when(jnp.logical_and(j >= 6, j < 8))
    def _():
        mall_ref[...] = _sigmoid(matmul()).astype(BF16)

    @pl.when(j >= 8)
    def _():
        mall_ref[...] = _silu(matmul()).astype(BF16)


def _mproj(x2d, pre_w, wt, wg, bif, conv_w, conv_b):
    return pl.pallas_call(
        _mproj_kernel,
        grid=(BATCH, M_TILES),
        in_specs=[
            pl.BlockSpec(memory_space=pl.ANY),
            pl.BlockSpec((1, D_MODEL), lambda i, j: (0, 0)),
            pl.BlockSpec((M_TN, D_MODEL), lambda i, j: (j, 0)),
            pl.BlockSpec((LANES, D_MODEL), lambda i, j: (0, 0)),
            pl.BlockSpec((2 * M_HEADS, 1), lambda i, j: (0, 0)),
            pl.BlockSpec((CONV_WIDTH, M_TN), lambda i, j: (0, jnp.minimum(j, 3))),
            pl.BlockSpec((1, M_TN), lambda i, j: (0, jnp.minimum(j, 3))),
        ],
        out_specs=[
            pl.BlockSpec((SEQ, M_TN), lambda i, j: (i, j)),
            pl.BlockSpec((1, M_TN // M_HEAD_DIM, N_CHUNKS, M_HEAD_DIM, M_CHUNK),
                         lambda i, j: (i, jnp.clip(j - 2, 0, 1), 0, 0, 0)),
            pl.BlockSpec((1, 2 * M_HEADS, N_CHUNKS, M_CHUNK), lambda i, j: (i, 0, 0, 0)),
            pl.BlockSpec((SEQ, D_MODEL), lambda i, j: (i, 0)),
        ],
        out_shape=[
            jax.ShapeDtypeStruct((BATCH * SEQ, 5 * M_WIDTH), BF16),
            jax.ShapeDtypeStruct((BATCH, M_HEADS, N_CHUNKS, M_HEAD_DIM, M_CHUNK), BF16),
            jax.ShapeDtypeStruct((BATCH, 2 * M_HEADS, N_CHUNKS, M_CHUNK), F32),
            jax.ShapeDtypeStruct((BATCH * SEQ, D_MODEL), BF16),
        ],
        scratch_shapes=[pltpu.VMEM((2, SEQ, D_MODEL), F32), pltpu.SemaphoreType.DMA((2,))],
        compiler_params=pltpu.CompilerParams(dimension_semantics=("arbitrary", "arbitrary"),
                                             vmem_limit_bytes=VMEM_LIMIT),
        name="mproj",
    )(x2d, pre_w.reshape(1, D_MODEL), wt, wg, bif, conv_w, conv_b)


A_TILES = 3
EASY_STRIDE = 4


def _aproj_kernel(dil, h_ref, w_ref, tab_ref, o_ref, acc0_ref, acc1_ref, tmp_ref):
    s = pl.program_id(0)
    last = BATCH * A_TILES
    pt = (jnp.maximum(s, 1) - 1) % A_TILES

    accs = (acc0_ref, acc1_ref)

    def matmul(slot):
        res = _dot_nt(h_ref[...], w_ref[...].astype(BF16))
        for hd in range(A_HEADS):
            accs[slot][hd] = res[:, hd * A_HEAD_DIM:(hd + 1) * A_HEAD_DIM]

    ls = SEQ // dil

    two_pass = dil == EASY_STRIDE * EASY_STRIDE

    def stage(slot, hd):
        if two_pass:
            n4 = SEQ // EASY_STRIDE
            for r4 in range(EASY_STRIDE):
                tmp_ref[hd, r4 * n4:(r4 + 1) * n4, :] = accs[slot][hd, pl.ds(r4, n4, stride=EASY_STRIDE), :]

    def residue_rows(slot, hd, r):
        if dil == 1:
            return accs[slot][hd]
        if two_pass:
            q, r4 = divmod(r, EASY_STRIDE)
            return tmp_ref[hd, pl.ds(r4 * (SEQ // EASY_STRIDE) + q, ls, stride=EASY_STRIDE), :]
        return accs[slot][hd, pl.ds(r, ls, stride=dil), :]

    def rope_epilogue(slot):
        scale = jnp.where(pt == 0, A_HEAD_DIM ** -0.5 * LOG2E, 1.0).astype(F32)
        half = ROPE_DIM // 2
        for hd in range(A_HEADS):
            stage(slot, hd)
            for r in range(dil):
                rows = slice(r * ls, (r + 1) * ls)
                xh = residue_rows(slot, hd, r)
                y = (xh * tab_ref[0, rows, :] + pltpu.roll(xh, half, axis=1) * tab_ref[1, rows, :]
                     + pltpu.roll(xh, A_HEAD_DIM - half, axis=1) * tab_ref[2, rows, :])
                o_ref[0, 0, rows, hd * A_HEAD_DIM:(hd + 1) * A_HEAD_DIM] = (y * scale).astype(BF16)

    def cast_epilogue(slot):
        for hd in range(A_HEADS):
            stage(slot, hd)
            for r in range(dil):
                rows = slice(r * ls, (r + 1) * ls)
                o_ref[0, 0, rows, hd * A_HEAD_DIM:(hd + 1) * A_HEAD_DIM] = \
                    residue_rows(slot, hd, r).astype(BF16)

    mid = jnp.logical_and(s > 0, s < last)

    @pl.when(s == 0)
    def _():
        matmul(0)

    for slot in range(2):
        mine = jnp.logical_and(mid, s % 2 == slot)

        @pl.when(jnp.logical_and(mine, pt < 2))
        def _(slot=slot):
            rope_epilogue(1 - slot)
            matmul(slot)

        @pl.when(jnp.logical_and(mine, pt == 2))
        def _(slot=slot):
            cast_epilogue(1 - slot)
            matmul(slot)

    @pl.when(s == last)
    def _():
        cast_epilogue((last - 1) % 2)


def _aproj(h, wt, tabs_g, g):
    dil = A_GROUPS[g][1]

    def w_row(s):
        return OFF_QA + (cur(s) % A_TILES) * A_QKV_WIDTH + g * A_OUT_WIDTH
    last = BATCH * A_TILES

    def cur(s):
        return jnp.minimum(s, last - 1)

    def prev(s):
        return jnp.maximum(s, 1) - 1

    return pl.pallas_call(
        functools.partial(_aproj_kernel, dil),
        grid=(last + 1,),
        in_specs=[
            pl.BlockSpec((SEQ, D_MODEL), lambda s: (cur(s) // A_TILES, 0)),
            pl.BlockSpec((pl.Element(A_OUT_WIDTH), pl.Element(D_MODEL)),
                         lambda s: (pl.multiple_of(w_row(s), SUBLANES), 0)),
            pl.BlockSpec((3, SEQ, LANES), lambda s: (0, 0, 0)),
        ],
        out_specs=pl.BlockSpec((1, 1, SEQ, A_OUT_WIDTH),
                               lambda s: (prev(s) // A_TILES, prev(s) % A_TILES, 0, 0)),
        out_shape=jax.ShapeDtypeStruct((BATCH, A_TILES, SEQ, A_OUT_WIDTH), BF16),
        scratch_shapes=[pltpu.VMEM((A_HEADS, SEQ, A_HEAD_DIM), F32) for _ in range(3)],
        compiler_params=pltpu.CompilerParams(dimension_semantics=("arbitrary",),
                                             vmem_limit_bytes=VMEM_LIMIT),
        name="aproj_d%d" % dil,
    )(h, wt, tabs_g)


def _lane_scan(x, op, fill):
    lane = lax.broadcasted_iota(jnp.int32, x.shape, 1)
    sh = 1
    while sh < x.shape[1]:
        shifted = jnp.where(lane >= sh, pltpu.roll(x, sh, axis=1), fill)
        x = op(x, shifted)
        sh *= 2
    return x


def _gates_kernel(ig_ref, fg_ref, u_ref, g_ref, mt_ref, mp_ref, w_ref, dec_ref):
    L = M_CHUNK
    ig = ig_ref[...]
    fg = fg_ref[...]
    logf = -(jnp.maximum(-fg, 0.0) + jnp.log1p(jnp.exp(-jnp.abs(fg))))
    b = _lane_scan(logf, jnp.add, 0.0)
    u = ig - b
    cm = _lane_scan(u, jnp.maximum, NEG_INF)
    lane = lax.broadcasted_iota(jnp.int32, ig.shape, 1)
    btot = jnp.broadcast_to(jnp.sum(jnp.where(lane == L - 1, b, 0.0), axis=1, keepdims=True), ig.shape)
    umax = jnp.broadcast_to(jnp.max(cm, axis=1, keepdims=True), ig.shape)
    chunk = lax.broadcasted_iota(jnp.int32, ig.shape, 0) % N_CHUNKS
    m = jnp.zeros(ig.shape, F32)
    for c in range(N_CHUNKS - 1):
        nxt = pltpu.roll(btot + jnp.maximum(m, umax), 1, axis=0)
        m = jnp.where(chunk == c + 1, nxt, m)
    g = jnp.maximum(m, cm)
    glast = jnp.max(g, axis=1, keepdims=True)
    u_ref[...] = u * LOG2E
    g_ref[...] = g * LOG2E
    mt_ref[...] = (b + g) * LOG2E
    mp_ref[...] = m * LOG2E
    w_ref[...] = jnp.exp(u - glast)
    dec_ref[...] = jnp.broadcast_to(jnp.exp(m - glast), ig.shape)


def _gates(gt):
    rows = BATCH * M_HEADS * N_CHUNKS
    ig = gt[:, :M_HEADS].reshape(rows, M_CHUNK)
    fg = gt[:, M_HEADS:].reshape(rows, M_CHUNK)
    spec = pl.BlockSpec((rows, M_CHUNK), lambda: (0, 0))
    return pl.pallas_call(
        _gates_kernel,
        in_specs=[spec, spec],
        out_specs=[spec] * 6,
        out_shape=[jax.ShapeDtypeStruct((rows, M_CHUNK), F32)] * 6,
        name="gates",
    )(ig, fg)


M_HPS = 2


def _mlstm_kernel(q_ref, kt_ref, v_ref, o_ref, z_ref, u_ref, g_ref, mt_ref, mp_ref, w_ref, dec_ref,
                  out_ref):
    L = M_CHUNK
    ti = lax.broadcasted_iota(jnp.int32, (L, L), 0)
    si = lax.broadcasted_iota(jnp.int32, (L, L), 1)
    causal = ti >= si
    ones = jnp.ones((L, LANES), BF16)

    states = [jnp.zeros((M_HEAD_DIM, M_HEAD_DIM + LANES), F32) for _ in range(M_HPS)]
    for c in range(N_CHUNKS):
        rs = slice(c * L, (c + 1) * L)
        for hh in range(M_HPS):
            hs = slice(hh * M_HEAD_DIM, (hh + 1) * M_HEAD_DIM)
            cs = slice(hh * N_CHUNKS + c, hh * N_CHUNKS + c + 1)
            state = states[hh]
            qc = q_ref[rs, hs]
            v1 = jnp.concatenate([v_ref[rs, hs], ones], axis=1)
            ktc = kt_ref[0, hh, c]
            u_r = u_ref[cs, :]
            gcol = jnp.broadcast_to(g_ref[cs, :], (LANES, L)).T
            mcol = jnp.broadcast_to(mt_ref[cs, :], (LANES, L)).T
            dmat = jnp.where(causal, jnp.exp2(u_r - jnp.concatenate([gcol, gcol], axis=1)), 0.0)

            s = jnp.dot(qc, ktc, preferred_element_type=F32) * dmat
            tot = jnp.dot(s.astype(BF16), v1, preferred_element_type=F32)
            if c > 0:
                inter = jnp.exp2(mp_ref[cs, :LANES] - gcol)
                tot = tot + jnp.concatenate([inter, inter, inter], axis=1) \
                    * jnp.dot(qc, state.astype(BF16), preferred_element_type=F32)
            den = tot[:, M_HEAD_DIM:]
            rinv = 1.0 / jnp.maximum(jnp.abs(den), jnp.exp2(-mcol))
            hval = tot[:, :M_HEAD_DIM] * jnp.concatenate([rinv, rinv], axis=1)

            hg = hval * o_ref[rs, hs].astype(F32)
            ms = jnp.mean(hg * hg, axis=1, keepdims=True)
            hn = hg * lax.rsqrt(ms + NORM_EPS)
            out_ref[rs, hs] = (hn * z_ref[rs, hs].astype(F32)).astype(BF16)

            if c + 1 < N_CHUNKS:
                ktw = (ktc.astype(F32) * w_ref[cs, :]).astype(BF16)
                dec = dec_ref[cs, :]
                dec = jnp.concatenate([dec, dec[:, :LANES]], axis=1)
                states[hh] = dec * state + jnp.dot(ktw, v1, preferred_element_type=F32)


def _mlstm(mall, kt, gate_stats):
    blk = (SEQ, M_HPS * M_HEAD_DIM)
    groups = M_HEADS // M_HPS

    def seg(k):
        return pl.BlockSpec(blk, lambda b, h, k=k: (b, k * groups + h))

    stat = pl.BlockSpec((M_HPS * N_CHUNKS, M_CHUNK), lambda b, h: (b * groups + h, 0))
    return pl.pallas_call(
        _mlstm_kernel,
        grid=(BATCH, groups),
        in_specs=[
            seg(0),
            pl.BlockSpec((1, M_HPS, N_CHUNKS, M_HEAD_DIM, M_CHUNK), lambda b, h: (b, h, 0, 0, 0)),
            seg(2), seg(3), seg(4),
        ] + [stat] * 6,
        out_specs=pl.BlockSpec(blk, lambda b, h: (b, h)),
        out_shape=jax.ShapeDtypeStruct((BATCH * SEQ, M_WIDTH), BF16),
        compiler_params=pltpu.CompilerParams(dimension_semantics=("parallel", "parallel"),
                                             vmem_limit_bytes=VMEM_LIMIT),
        name="mlstm",
    )(mall, kt, mall, mall, mall, *gate_stats)


GATHERED = (2,)
GATHER_PITCH = A_BLOCK + 8
ATTN_WAVE = 12


def _attn_block_order():
    order = [(r, 2) for r in range(A_GROUPS[2][1])]
    nblk1 = SEQ // A_GROUPS[1][1] // A_BLOCK
    per_span = SEQ // A_BLOCK // nblk1
    for n in range(nblk1):
        order += [(r * nblk1 + n, 1) for r in range(A_GROUPS[1][1])]
        order += [(n * per_span + i, 0) for i in range(per_span)]
    return order


def _attn_kernel(q0, k0, v0, q1, k1, v1, q2, k2, v2, out_ref,
                 a0, a1, a2, m0, m1, m2, l0, l1, l2):
    BLK = A_BLOCK
    ti = lax.broadcasted_iota(jnp.int32, (BLK, BLK), 0)
    ji = lax.broadcasted_iota(jnp.int32, (BLK, BLK), 1)
    cur_ok = ji <= ti
    ti2 = lax.broadcasted_iota(jnp.int32, (BLK, 2 * BLK), 0)
    krel = lax.broadcasted_iota(jnp.int32, (BLK, 2 * BLK), 1) - BLK
    band_ok = jnp.logical_and(krel <= ti2, krel >= ti2 - BLK)

    qs = (q0, q1, q2)
    ks = (k0, k1, k2)
    vs = (v0, v1, v2)
    accs = (a0, a1, a2)
    ms = (m0, m1, m2)
    ls = (l0, l1, l2)

    RB = 256

    def merge(rb):
        sl = slice(rb * RB, (rb + 1) * RB)

        def rows(refs, g):
            if g in GATHERED:
                dil = A_GROUPS[g][1]
                j0 = rb * RB // dil
                return jnp.concatenate([refs[g][pl.ds(j0 + jj, dil, stride=GATHER_PITCH), :]
                                        for jj in range(RB // dil)], axis=0)
            return refs[g][sl, :]

        mg = [rows(ms, g) for g in range(N_GROUPS)]
        mmax = jnp.maximum(jnp.maximum(mg[0], mg[1]), mg[2])
        num = None
        den = None
        for g in range(N_GROUPS):
            e = jnp.exp2(mg[g] - mmax)
            num = e * rows(accs, g) if num is None else num + e * rows(accs, g)
            den = e * rows(ls, g) if den is None else den + e * rows(ls, g)
        out_ref[sl, :] = (num * (1.0 / den)).astype(BF16)

    blocks = _attn_block_order()
    done_rows = [set() for _ in range(N_GROUPS)]
    merged = 0
    for w0 in range(0, len(blocks), ATTN_WAVE):
        wave = []
        for i, g in blocks[w0:w0 + ATTN_WAVE]:
            dil = A_GROUPS[g][1]
            r, n = divmod(i, SEQ // dil // BLK)
            rows = slice(i * BLK, (i + 1) * BLK)
            keys = slice((i - 1) * BLK, (i + 1) * BLK) if n > 0 else rows
            valid = band_ok if n > 0 else cur_ok
            s = lax.dot_general(qs[g][0, 0, rows, :], ks[g][0, 0, keys, :], (((1,), (1,)), ((), ())),
                                preferred_element_type=F32)
            s = jnp.where(valid, s, NEG_INF)
            m = jnp.max(s, axis=1, keepdims=True)
            tok0 = n * BLK * dil + r
            if g in GATHERED:
                idx = pl.ds(r * GATHER_PITCH, BLK)
            else:
                idx = pl.ds(tok0, BLK, stride=dil) if dil > 1 else pl.ds(tok0, BLK)
            wave.append((g, keys, valid, idx, s, m))
        for g, keys, valid, idx, s, m in wave:
            p = jnp.where(valid, jnp.exp2(s - m), 0.0)
            nkeys = keys.stop - keys.start
            v1 = jnp.concatenate([vs[g][0, 0, keys, :], jnp.ones((nkeys, LANES), BF16)], axis=1)
            acc = jnp.dot(p.astype(BF16), v1, preferred_element_type=F32)
            accs[g][idx, :] = acc[:, :LANES]
            ms[g][idx, :] = jnp.broadcast_to(m, (BLK, LANES))
            ls[g][idx, :] = acc[:, LANES:]
        for i, g in blocks[w0:w0 + ATTN_WAVE]:
            dil = A_GROUPS[g][1]
            r, n = divmod(i, SEQ // dil // BLK)
            span = BLK * dil
            if r == dil - 1:
                done_rows[g].update(range(n * span // RB, (n + 1) * span // RB))
        while merged < SEQ // RB and all(merged in d for d in done_rows):
            merge(merged)
            merged += 1


def _attn(qkv):
    in_specs = []
    args = []
    for g in range(N_GROUPS):
        for t in range(3):
            in_specs.append(pl.BlockSpec((1, 1, SEQ, A_HEAD_DIM), lambda b, h, t=t: (b, t, 0, h)))
            args.append(qkv[g])
    return pl.pallas_call(
        _attn_kernel,
        grid=(BATCH, A_HEADS),
        in_specs=in_specs,
        out_specs=pl.BlockSpec((SEQ, A_HEAD_DIM), lambda b, h: (b, h)),
        out_shape=jax.ShapeDtypeStruct((BATCH * SEQ, A_OUT_WIDTH), BF16),
        scratch_shapes=[pltpu.VMEM((A_GROUPS[g][1] * GATHER_PITCH if g in GATHERED else SEQ, LANES), F32)
                        for _ in range(3) for g in range(N_GROUPS)],
        compiler_params=pltpu.CompilerParams(dimension_semantics=("parallel", "parallel"),
                                             vmem_limit_bytes=VMEM_LIMIT),
        name="attn",
    )(*args)


O_TM = 1024


def _out_kernel(x_ref, prew_ref, hm_ref, att_ref, wg_ref, wpm_ref, wpa_ref, wout_ref, postw_ref,
                y_ref, wgb_ref):
    @pl.when(pl.program_id(0) == 0)
    def _():
        wgb_ref[...] = wg_ref[...].astype(BF16)

    x = x_ref[...]
    ms = jnp.mean(x * x, axis=-1, keepdims=True)
    h = (x * lax.rsqrt(ms + NORM_EPS) * prew_ref[...]).astype(BF16)
    gates = _dot_nt(h, wgb_ref[...])
    za = _silu(gates[:, :A_OUT_WIDTH])
    sgm = _sigmoid(gates[:, A_OUT_WIDTH:A_OUT_WIDTH + D_MODEL])
    sga = _sigmoid(gates[:, A_OUT_WIDTH + D_MODEL:])
    a = (att_ref[...].astype(F32) * za).astype(BF16)
    bm = jnp.dot(hm_ref[...], wpm_ref[...], preferred_element_type=F32)
    ba = jnp.dot(a, wpa_ref[...], preferred_element_type=F32)
    merged = (sgm * bm + sga * ba).astype(BF16)
    y = jnp.dot(merged, wout_ref[...], preferred_element_type=F32)
    ms2 = jnp.mean(y * y, axis=-1, keepdims=True)
    y_ref[...] = x + y * lax.rsqrt(ms2 + NORM_EPS) * postw_ref[...]


def _out(x2d, pre_w, hm, att, wt, wpm, wpa, wout, post_w):
    n = x2d.shape[0]
    const = lambda i: (0, 0)
    return pl.pallas_call(
        _out_kernel,
        grid=(n // O_TM,),
        in_specs=[
            pl.BlockSpec((O_TM, D_MODEL), lambda i: (i, 0)),
            pl.BlockSpec((1, D_MODEL), const),
            pl.BlockSpec((O_TM, M_WIDTH), lambda i: (i, 0)),
            pl.BlockSpec((O_TM, A_OUT_WIDTH), lambda i: (i, 0)),
            pl.BlockSpec((pl.Element(IN_WIDTH - OFF_ZA), pl.Element(D_MODEL)), lambda i: (OFF_ZA, 0),
                         pipeline_mode=pl.Buffered(1)),
            pl.BlockSpec((M_WIDTH, D_MODEL), const),
            pl.BlockSpec((A_OUT_WIDTH, D_MODEL), const),
            pl.BlockSpec((D_MODEL, D_MODEL), const),
            pl.BlockSpec((1, D_MODEL), const),
        ],
        out_specs=pl.BlockSpec((O_TM, D_MODEL), lambda i: (i, 0)),
        out_shape=jax.ShapeDtypeStruct((n, D_MODEL), F32),
        scratch_shapes=[pltpu.VMEM((IN_WIDTH - OFF_ZA, D_MODEL), BF16)],
        compiler_params=pltpu.CompilerParams(dimension_semantics=("arbitrary",),
                                             vmem_limit_bytes=VMEM_LIMIT),
        name="outproj",
    )(x2d, pre_w.reshape(1, D_MODEL), hm, att, wt, wpm, wpa, wout, post_w.reshape(1, D_MODEL))


def _rope_tables():
    pos = np.arange(SEQ, dtype=np.float64)
    inv_freq = ROPE_THETA ** (-np.arange(0, ROPE_DIM, 2, dtype=np.float64) / ROPE_DIM)
    ang = pos[:, None] * inv_freq[None, :]
    cos, sin = np.cos(ang), np.sin(ang)
    half = ROPE_DIM // 2
    pad = A_HEAD_DIM - ROPE_DIM
    cos_t = np.concatenate([cos, cos, np.ones((SEQ, pad))], axis=1)
    sin_up = np.concatenate([np.zeros((SEQ, half)), sin, np.zeros((SEQ, pad))], axis=1)
    sin_dn = np.concatenate([-sin, np.zeros((SEQ, half + pad))], axis=1)
    tab = np.stack([cos_t, sin_up, sin_dn]).astype(np.float32)
    out = []
    for _, dil in A_GROUPS:
        ls = SEQ // dil
        out.append(tab.reshape(3, ls, dil, A_HEAD_DIM).transpose(0, 2, 1, 3).reshape(3, SEQ, A_HEAD_DIM))
    return jnp.asarray(np.stack(out))


def _layer(x, pre_w, w_in, b_if, conv_w, conv_b, m_norm_w, w_pm, w_pa, w_out, post_w):
    x2d = x.reshape(BATCH * SEQ, D_MODEL)
    wt = w_in.T
    wg = jnp.pad(wt[OFF_GATES:OFF_QA], ((0, LANES - 2 * M_HEADS), (0, 0))).astype(BF16)
    tabs = _rope_tables()

    mall, kt, gt, h = _mproj(x2d, pre_w, wt, wg, b_if.reshape(2 * M_HEADS, 1), conv_w,
                             conv_b.reshape(1, 2 * M_WIDTH))
    qkv = [_aproj(h, wt, tabs[g], g) for g in range(N_GROUPS)]
    hm = _mlstm(mall, kt, _gates(gt))
    att = _attn(qkv)
    w_pm_n = (m_norm_w[:, None] * w_pm).astype(BF16)
    y = _out(x2d, pre_w, hm, att, wt, w_pm_n, w_pa.astype(BF16), w_out.astype(BF16), post_w)
    return y.reshape(BATCH, SEQ, D_MODEL)


@jax.jit
def kernel(x, pre_norm_w, w_in, b_if, conv_w, conv_b, m_norm_w, w_proj_m, w_proj_a, w_out, post_norm_w):
    for layer in range(pre_norm_w.shape[0]):
        x = _layer(x, pre_norm_w[layer], w_in[layer], b_if[layer], conv_w[layer], conv_b[layer],
                   m_norm_w[layer], w_proj_m[layer], w_proj_a[layer], w_out[layer], post_norm_w[layer])
    return x
```

```python
import functools

import jax
import jax.numpy as jnp
import numpy as np
from jax import lax
from jax.experimental import pallas as pl
from jax.experimental.pallas import tpu as pltpu

D_MODEL = 1024
BATCH = 8
SEQ = 2048
M_WIDTH = 1024
M_HEADS = 4
M_HEAD_DIM = 256
CONV_WIDTH = 4
A_GROUPS = ((128, 1), (512, 4), (2048, 16))
N_GROUPS = 3
A_HEADS = 4
A_HEAD_DIM = 128
A_OUT_WIDTH = 512
A_QKV_WIDTH = 1536
A_BLOCK = 128
ROPE_DIM = 32
ROPE_THETA = 500000.0
NORM_EPS = 1e-6
NEG_INF = -1e30
LOG2E = 1.4426950408889634

OFF_GATES = 5 * M_WIDTH
OFF_QA = OFF_GATES + 2 * M_HEADS
OFF_KA = OFF_QA + A_QKV_WIDTH
OFF_VA = OFF_KA + A_QKV_WIDTH
OFF_ZA = OFF_VA + A_QKV_WIDTH
OFF_GM = OFF_ZA + A_OUT_WIDTH
IN_WIDTH = OFF_GM + 2 * D_MODEL

LANES = 128
M_CHUNK = 256
N_CHUNKS = SEQ // M_CHUNK
V7X_VMEM_BYTES = 64 * 1024 * 1024
VMEM_LIMIT = V7X_VMEM_BYTES * 3 // 4

F32 = jnp.float32
BF16 = jnp.bfloat16


def _dot_nt(a, bt):
    return lax.dot_general(a, bt, (((1,), (1,)), ((), ())), preferred_element_type=F32)


def _sigmoid(x):
    return 1.0 / (1.0 + jnp.exp(-x))


def _silu(x):
    return x * _sigmoid(x)


M_TN = 512


SUBLANES = 8


def _shift_rows(x3, s, sub, carry):
    rot = pltpu.roll(x3, s, axis=1)
    prev = jnp.concatenate([carry, rot[:-1]], axis=0)
    return jnp.where(sub < s, prev, rot), rot[-1:]


def _conv_tile(h_ref, w, cw, cb, emit):
    cols = w.shape[0]
    sub = lax.broadcasted_iota(jnp.int32, (1, SUBLANES, cols), 1)
    zero = jnp.zeros((1, SUBLANES, cols), F32)
    carry = [zero, zero]

    def matmul(p):
        return _dot_nt(h_ref[p * M_CHUNK:(p + 1) * M_CHUNK, :], w)

    def conv(acc):
        x3 = acc.reshape(M_CHUNK // SUBLANES, SUBLANES, cols)
        d1, carry[0] = _shift_rows(x3, 1, sub, carry[0])
        near = x3 * cw[3] + d1 * cw[2] + cb[0]
        far = x3 * cw[1] + d1 * cw[0]
        d2, carry[1] = _shift_rows(far, 2, sub, carry[1])
        return (near + d2).reshape(M_CHUNK, cols)

    acc = matmul(0)
    for p in range(N_CHUNKS):
        nxt = matmul(p + 1) if p + 1 < N_CHUNKS else None
        emit(p, conv(acc))
        acc = nxt


M_TILES = 5 * M_WIDTH // M_TN


def _mproj_kernel(x_hbm, prew_ref, w_ref, wg_ref, bif_ref, cw_ref, cb_ref,
                  mall_ref, kt_ref, gt_ref, h_ref, xbuf_ref, xsem_ref):
    b = pl.program_id(0)
    j = pl.program_id(1)

    def x_copy(bb):
        rows = pl.ds(pl.multiple_of(bb * SEQ, SEQ), SEQ)
        return pltpu.make_async_copy(x_hbm.at[rows], xbuf_ref.at[bb % 2], xsem_ref.at[bb % 2])

    def matmul():
        return _dot_nt(h_ref[...], w_ref[...].astype(BF16))

    @pl.when(jnp.logical_and(b == 0, j == 0))
    def _():
        x_copy(b).start()

    @pl.when(j == 0)
    def _():
        x_copy(b).wait()

        @pl.when(b + 1 < pl.num_programs(0))
        def _():
            x_copy(b + 1).start()

        x = xbuf_ref[b % 2]
        ms = jnp.mean(x * x, axis=-1, keepdims=True)
        h_ref[...] = (x * lax.rsqrt(ms + NORM_EPS) * prew_ref[...]).astype(BF16)
        g = _dot_nt(h_ref[...], wg_ref[...])
        gt = g.T[:2 * M_HEADS, :] + bif_ref[...]
        for c in range(N_CHUNKS):
            gt_ref[0, :, c, :] = gt[:, c * M_CHUNK:(c + 1) * M_CHUNK]

    @pl.when(j < 2)
    def _():
        def emit(p, y):
            mall_ref[p * M_CHUNK:(p + 1) * M_CHUNK, :] = _silu(y).astype(BF16)
        _conv_tile(h_ref, w_ref[...].astype(BF16), cw_ref[...], cb_ref[...], emit)

    @pl.when(jnp.logical_and(j >= 2, j < 4))
    def _():
        def emit(p, y):
            y = _silu(y) * (M_HEAD_DIM ** -0.5)
            mall_ref[p * M_CHUNK:(p + 1) * M_CHUNK, :] = y.astype(BF16)
            for hh in range(M_TN // M_HEAD_DIM):
                kt_ref[0, hh, p] = y[:, hh * M_HEAD_DIM:(hh + 1) * M_HEAD_DIM].T.astype(BF16)
        _conv_tile(h_ref, w_ref[...].astype(BF16), cw_ref[...], cb_ref[...], emit)

    @pl.when(jnp.logical_and(j >= 4, j < 6))
    def _():
        mall_ref[...] = matmul().astype(BF16)

    @pl.when(jnp.logical_and(j >= 6, j < 8))
    def _():
        mall_ref[...] = _sigmoid(matmul()).astype(BF16)

    @pl.when(j >= 8)
    def _():
        mall_ref[...] = _silu(matmul()).astype(BF16)


def _mproj(x2d, pre_w, wt, wg, bif, conv_w, conv_b):
    return pl.pallas_call(
        _mproj_kernel,
        grid=(BATCH, M_TILES),
        in_specs=[
            pl.BlockSpec(memory_space=pl.ANY),
            pl.BlockSpec((1, D_MODEL), lambda i, j: (0, 0)),
            pl.BlockSpec((M_TN, D_MODEL), lambda i, j: (j, 0)),
            pl.BlockSpec((LANES, D_MODEL), lambda i, j: (0, 0)),
            pl.BlockSpec((2 * M_HEADS, 1), lambda i, j: (0, 0)),
            pl.BlockSpec((CONV_WIDTH, M_TN), lambda i, j: (0, jnp.minimum(j, 3))),
            pl.BlockSpec((1, M_TN), lambda i, j: (0, jnp.minimum(j, 3))),
        ],
        out_specs=[
            pl.BlockSpec((SEQ, M_TN), lambda i, j: (i, j)),
            pl.BlockSpec((1, M_TN // M_HEAD_DIM, N_CHUNKS, M_HEAD_DIM, M_CHUNK),
                         lambda i, j: (i, jnp.clip(j - 2, 0, 1), 0, 0, 0)),
            pl.BlockSpec((1, 2 * M_HEADS, N_CHUNKS, M_CHUNK), lambda i, j: (i, 0, 0, 0)),
            pl.BlockSpec((SEQ, D_MODEL), lambda i, j: (i, 0)),
        ],
        out_shape=[
            jax.ShapeDtypeStruct((BATCH * SEQ, 5 * M_WIDTH), BF16),
            jax.ShapeDtypeStruct((BATCH, M_HEADS, N_CHUNKS, M_HEAD_DIM, M_CHUNK), BF16),
            jax.ShapeDtypeStruct((BATCH, 2 * M_HEADS, N_CHUNKS, M_CHUNK), F32),
            jax.ShapeDtypeStruct((BATCH * SEQ, D_MODEL), BF16),
        ],
        scratch_shapes=[pltpu.VMEM((2, SEQ, D_MODEL), F32), pltpu.SemaphoreType.DMA((2,))],
        compiler_params=pltpu.CompilerParams(dimension_semantics=("arbitrary", "arbitrary"),
                                             vmem_limit_bytes=VMEM_LIMIT),
        name="mproj",
    )(x2d, pre_w.reshape(1, D_MODEL), wt, wg, bif, conv_w, conv_b)


A_TILES = 3
EASY_STRIDE = 4


def _aproj_kernel(dil, h_hbm, w_ref, tab_ref, o_ref, acc0_ref, acc1_ref, tmp_ref, hbuf_ref, hsem_ref):
    s = pl.program_id(0)
    last = BATCH * A_TILES
    pt = (jnp.maximum(s, 1) - 1) % A_TILES

    accs = (acc0_ref, acc1_ref)

    b = jnp.minimum(s, last - 1) // A_TILES

    def h_copy(bb):
        rows = pl.ds(pl.multiple_of(bb * SEQ, SEQ), SEQ)
        return pltpu.make_async_copy(h_hbm.at[rows], hbuf_ref.at[bb % 2], hsem_ref.at[bb % 2])

    @pl.when(s == 0)
    def _():
        h_copy(b).start()

    @pl.when(jnp.logical_and(s % A_TILES == 0, s < last))
    def _():
        h_copy(b).wait()

        @pl.when(b + 1 < BATCH)
        def _():
            h_copy(b + 1).start()

    def matmul(slot):
        res = _dot_nt(hbuf_ref[b % 2], w_ref[...].astype(BF16))
        for hd in range(A_HEADS):
            accs[slot][hd] = res[:, hd * A_HEAD_DIM:(hd + 1) * A_HEAD_DIM]

    ls = SEQ // dil

    two_pass = dil == EASY_STRIDE * EASY_STRIDE

    def stage(slot, hd):
        if two_pass:
            n4 = SEQ // EASY_STRIDE
            for r4 in range(EASY_STRIDE):
                tmp_ref[hd, r4 * n4:(r4 + 1) * n4, :] = accs[slot][hd, pl.ds(r4, n4, stride=EASY_STRIDE), :]

    def residue_rows(slot, hd, r):
        if dil == 1:
            return accs[slot][hd]
        if two_pass:
            q, r4 = divmod(r, EASY_STRIDE)
            return tmp_ref[hd, pl.ds(r4 * (SEQ // EASY_STRIDE) + q, ls, stride=EASY_STRIDE), :]
        return accs[slot][hd, pl.ds(r, ls, stride=dil), :]

    def rope_epilogue(slot):
        scale = jnp.where(pt == 0, A_HEAD_DIM ** -0.5 * LOG2E, 1.0).astype(F32)
        half = ROPE_DIM // 2
        for hd in range(A_HEADS):
            stage(slot, hd)
            for r in range(dil):
                rows = slice(r * ls, (r + 1) * ls)
                xh = residue_rows(slot, hd, r)
                y = (xh * tab_ref[0, rows, :] + pltpu.roll(xh, half, axis=1) * tab_ref[1, rows, :]
                     + pltpu.roll(xh, A_HEAD_DIM - half, axis=1) * tab_ref[2, rows, :])
                o_ref[0, 0, rows, hd * A_HEAD_DIM:(hd + 1) * A_HEAD_DIM] = (y * scale).astype(BF16)

    def cast_epilogue(slot):
        for hd in range(A_HEADS):
            stage(slot, hd)
            for r in range(dil):
                rows = slice(r * ls, (r + 1) * ls)
                o_ref[0, 0, rows, hd * A_HEAD_DIM:(hd + 1) * A_HEAD_DIM] = \
                    residue_rows(slot, hd, r).astype(BF16)

    mid = jnp.logical_and(s > 0, s < last)

    @pl.when(s == 0)
    def _():
        matmul(0)

    for slot in range(2):
        mine = jnp.logical_and(mid, s % 2 == slot)

        @pl.when(jnp.logical_and(mine, pt < 2))
        def _(slot=slot):
            rope_epilogue(1 - slot)
            matmul(slot)

        @pl.when(jnp.logical_and(mine, pt == 2))
        def _(slot=slot):
            cast_epilogue(1 - slot)
            matmul(slot)

    @pl.when(s == last)
    def _():
        cast_epilogue((last - 1) % 2)


def _aproj(h, wt, tabs_g, g):
    dil = A_GROUPS[g][1]

    def w_row(s):
        return OFF_QA + (cur(s) % A_TILES) * A_QKV_WIDTH + g * A_OUT_WIDTH
    last = BATCH * A_TILES

    def cur(s):
        return jnp.minimum(s, last - 1)

    def prev(s):
        return jnp.maximum(s, 1) - 1

    return pl.pallas_call(
        functools.partial(_aproj_kernel, dil),
        grid=(last + 1,),
        in_specs=[
            pl.BlockSpec(memory_space=pl.ANY),
            pl.BlockSpec((pl.Element(A_OUT_WIDTH), pl.Element(D_MODEL)),
                         lambda s: (pl.multiple_of(w_row(s), SUBLANES), 0)),
            pl.BlockSpec((3, SEQ, LANES), lambda s: (0, 0, 0)),
        ],
        out_specs=pl.BlockSpec((1, 1, SEQ, A_OUT_WIDTH),
                               lambda s: (prev(s) // A_TILES, prev(s) % A_TILES, 0, 0)),
        out_shape=jax.ShapeDtypeStruct((BATCH, A_TILES, SEQ, A_OUT_WIDTH), BF16),
        scratch_shapes=[pltpu.VMEM((A_HEADS, SEQ, A_HEAD_DIM), F32) for _ in range(3)]
        + [pltpu.VMEM((2, SEQ, D_MODEL), BF16), pltpu.SemaphoreType.DMA((2,))],
        compiler_params=pltpu.CompilerParams(dimension_semantics=("arbitrary",),
                                             vmem_limit_bytes=VMEM_LIMIT),
        name="aproj_d%d" % dil,
    )(h, wt, tabs_g)


def _lane_scan(x, op, fill):
    lane = lax.broadcasted_iota(jnp.int32, x.shape, 1)
    sh = 1
    while sh < x.shape[1]:
        shifted = jnp.where(lane >= sh, pltpu.roll(x, sh, axis=1), fill)
        x = op(x, shifted)
        sh *= 2
    return x


def _gates_kernel(ig_ref, fg_ref, u_ref, g_ref, mt_ref, mp_ref, w_ref, dec_ref):
    L = M_CHUNK
    ig = ig_ref[...]
    fg = fg_ref[...]
    logf = -(jnp.maximum(-fg, 0.0) + jnp.log1p(jnp.exp(-jnp.abs(fg))))
    b = _lane_scan(logf, jnp.add, 0.0)
    u = ig - b
    cm = _lane_scan(u, jnp.maximum, NEG_INF)
    lane = lax.broadcasted_iota(jnp.int32, ig.shape, 1)
    btot = jnp.broadcast_to(jnp.sum(jnp.where(lane == L - 1, b, 0.0), axis=1, keepdims=True), ig.shape)
    umax = jnp.broadcast_to(jnp.max(cm, axis=1, keepdims=True), ig.shape)
    chunk = lax.broadcasted_iota(jnp.int32, ig.shape, 0) % N_CHUNKS
    m = jnp.zeros(ig.shape, F32)
    for c in range(N_CHUNKS - 1):
        nxt = pltpu.roll(btot + jnp.maximum(m, umax), 1, axis=0)
        m = jnp.where(chunk == c + 1, nxt, m)
    g = jnp.maximum(m, cm)
    glast = jnp.max(g, axis=1, keepdims=True)
    u_ref[...] = u * LOG2E
    g_ref[...] = g * LOG2E
    mt_ref[...] = (b + g) * LOG2E
    mp_ref[...] = m * LOG2E
    w_ref[...] = jnp.exp(u - glast)
    dec_ref[...] = jnp.broadcast_to(jnp.exp(m - glast), ig.shape)


def _gates(gt):
    rows = BATCH * M_HEADS * N_CHUNKS
    ig = gt[:, :M_HEADS].reshape(rows, M_CHUNK)
    fg = gt[:, M_HEADS:].reshape(rows, M_CHUNK)
    spec = pl.BlockSpec((rows, M_CHUNK), lambda: (0, 0))
    return pl.pallas_call(
        _gates_kernel,
        in_specs=[spec, spec],
        out_specs=[spec] * 6,
        out_shape=[jax.ShapeDtypeStruct((rows, M_CHUNK), F32)] * 6,
        name="gates",
    )(ig, fg)


M_HPS = 2


def _mlstm_kernel(q_ref, kt_ref, v_ref, o_ref, z_ref, u_ref, g_ref, mt_ref, mp_ref, w_ref, dec_ref,
                  out_ref):
    L = M_CHUNK
    ti = lax.broadcasted_iota(jnp.int32, (L, L), 0)
    si = lax.broadcasted_iota(jnp.int32, (L, L), 1)
    causal = ti >= si
    ones = jnp.ones((L, LANES), BF16)

    states = [jnp.zeros((M_HEAD_DIM, M_HEAD_DIM + LANES), F32) for _ in range(M_HPS)]
    for c in range(N_CHUNKS):
        rs = slice(c * L, (c + 1) * L)
        for hh in range(M_HPS):
            hs = slice(hh * M_HEAD_DIM, (hh + 1) * M_HEAD_DIM)
            cs = slice(hh * N_CHUNKS + c, hh * N_CHUNKS + c + 1)
            state = states[hh]
            qc = q_ref[rs, hs]
            v1 = jnp.concatenate([v_ref[rs, hs], ones], axis=1)
            ktc = kt_ref[0, hh, c]
            u_r = u_ref[cs, :]
            gcol = jnp.broadcast_to(g_ref[cs, :], (LANES, L)).T
            mcol = jnp.broadcast_to(mt_ref[cs, :], (LANES, L)).T
            dmat = jnp.where(causal, jnp.exp2(u_r - jnp.concatenate([gcol, gcol], axis=1)), 0.0)

            s = jnp.dot(qc, ktc, preferred_element_type=F32) * dmat
            tot = jnp.dot(s.astype(BF16), v1, preferred_element_type=F32)
            if c > 0:
                inter = jnp.exp2(mp_ref[cs, :LANES] - gcol)
                tot = tot + jnp.concatenate([inter, inter, inter], axis=1) \
                    * jnp.dot(qc, state.astype(BF16), preferred_element_type=F32)
            den = tot[:, M_HEAD_DIM:]
            rinv = 1.0 / jnp.maximum(jnp.abs(den), jnp.exp2(-mcol))
            hval = tot[:, :M_HEAD_DIM] * jnp.concatenate([rinv, rinv], axis=1)

            hg = hval * o_ref[rs, hs].astype(F32)
            ms = jnp.mean(hg * hg, axis=1, keepdims=True)
            hn = hg * lax.rsqrt(ms + NORM_EPS)
            out_ref[rs, hs] = (hn * z_ref[rs, hs].astype(F32)).astype(BF16)

            if c + 1 < N_CHUNKS:
                ktw = (ktc.astype(F32) * w_ref[cs, :]).astype(BF16)
                dec = dec_ref[cs, :]
                dec = jnp.concatenate([dec, dec[:, :LANES]], axis=1)
                states[hh] = dec * state + jnp.dot(ktw, v1, preferred_element_type=F32)


def _mlstm(mall, kt, gate_stats):
    blk = (SEQ, M_HPS * M_HEAD_DIM)
    groups = M_HEADS // M_HPS

    def seg(k):
        return pl.BlockSpec(blk, lambda b, h, k=k: (b, k * groups + h))

    stat = pl.BlockSpec((M_HPS * N_CHUNKS, M_CHUNK), lambda b, h: (b * groups + h, 0))
    return pl.pallas_call(
        _mlstm_kernel,
        grid=(BATCH, groups),
        in_specs=[
            seg(0),
            pl.BlockSpec((1, M_HPS, N_CHUNKS, M_HEAD_DIM, M_CHUNK), lambda b, h: (b, h, 0, 0, 0)),
            seg(2), seg(3), seg(4),
        ] + [stat] * 6,
        out_specs=pl.BlockSpec(blk, lambda b, h: (b, h)),
        out_shape=jax.ShapeDtypeStruct((BATCH * SEQ, M_WIDTH), BF16),
        compiler_params=pltpu.CompilerParams(dimension_semantics=("parallel", "parallel"),
                                             vmem_limit_bytes=VMEM_LIMIT),
        name="mlstm",
    )(mall, kt, mall, mall, mall, *gate_stats)


GATHERED = (2,)
GATHER_PITCH = A_BLOCK + 8
ATTN_WAVE = 12


def _attn_block_order():
    order = [(r, 2) for r in range(A_GROUPS[2][1])]
    nblk1 = SEQ // A_GROUPS[1][1] // A_BLOCK
    per_span = SEQ // A_BLOCK // nblk1
    for n in range(nblk1):
        order += [(r * nblk1 + n, 1) for r in range(A_GROUPS[1][1])]
        order += [(n * per_span + i, 0) for i in range(per_span)]
    return order


def _attn_kernel(q0, k0, v0, q1, k1, v1, q2, k2, v2, out_ref,
                 a0, a1, a2, m0, m1, m2, l0, l1, l2):
    BLK = A_BLOCK
    ti = lax.broadcasted_iota(jnp.int32, (BLK, BLK), 0)
    ji = lax.broadcasted_iota(jnp.int32, (BLK, BLK), 1)
    cur_ok = ji <= ti
    ti2 = lax.broadcasted_iota(jnp.int32, (BLK, 2 * BLK), 0)
    krel = lax.broadcasted_iota(jnp.int32, (BLK, 2 * BLK), 1) - BLK
    band_ok = jnp.logical_and(krel <= ti2, krel >= ti2 - BLK)

    qs = (q0, q1, q2)
    ks = (k0, k1, k2)
    vs = (v0, v1, v2)
    accs = (a0, a1, a2)
    ms = (m0, m1, m2)
    ls = (l0, l1, l2)

    RB = 256

    def merge(rb):
        sl = slice(rb * RB, (rb + 1) * RB)

        def rows(refs, g):
            if g in GATHERED:
                dil = A_GROUPS[g][1]
                j0 = rb * RB // dil
                return jnp.concatenate([refs[g][pl.ds(j0 + jj, dil, stride=GATHER_PITCH), :]
                                        for jj in range(RB // dil)], axis=0)
            return refs[g][sl, :]

        mg = [rows(ms, g) for g in range(N_GROUPS)]
        mmax = jnp.maximum(jnp.maximum(mg[0], mg[1]), mg[2])
        num = None
        den = None
        for g in range(N_GROUPS):
            e = jnp.exp2(mg[g] - mmax)
            num = e * rows(accs, g) if num is None else num + e * rows(accs, g)
            den = e * rows(ls, g) if den is None else den + e * rows(ls, g)
        out_ref[sl, :] = (num * (1.0 / den)).astype(BF16)

    blocks = _attn_block_order()
    done_rows = [set() for _ in range(N_GROUPS)]
    merged = 0
    for w0 in range(0, len(blocks), ATTN_WAVE):
        wave = []
        for i, g in blocks[w0:w0 + ATTN_WAVE]:
            dil = A_GROUPS[g][1]
            r, n = divmod(i, SEQ // dil // BLK)
            rows = slice(i * BLK, (i + 1) * BLK)
            keys = slice((i - 1) * BLK, (i + 1) * BLK) if n > 0 else rows
            valid = band_ok if n > 0 else cur_ok
            s = lax.dot_general(qs[g][0, 0, rows, :], ks[g][0, 0, keys, :], (((1,), (1,)), ((), ())),
                                preferred_element_type=F32)
            s = jnp.where(valid, s, NEG_INF)
            m = jnp.max(s, axis=1, keepdims=True)
            tok0 = n * BLK * dil + r
            if g in GATHERED:
                idx = pl.ds(r * GATHER_PITCH, BLK)
            else:
                idx = pl.ds(tok0, BLK, stride=dil) if dil > 1 else pl.ds(tok0, BLK)
            wave.append((g, keys, valid, idx, s, m))
        for g, keys, valid, idx, s, m in wave:
            p = jnp.where(valid, jnp.exp2(s - m), 0.0)
            nkeys = keys.stop - keys.start
            v1 = jnp.concatenate([vs[g][0, 0, keys, :], jnp.ones((nkeys, LANES), BF16)], axis=1)
            acc = jnp.dot(p.astype(BF16), v1, preferred_element_type=F32)
            accs[g][idx, :] = acc[:, :LANES]
            ms[g][idx, :] = jnp.broadcast_to(m, (BLK, LANES))
            ls[g][idx, :] = acc[:, LANES:]
        for i, g in blocks[w0:w0 + ATTN_WAVE]:
            dil = A_GROUPS[g][1]
            r, n = divmod(i, SEQ // dil // BLK)
            span = BLK * dil
            if r == dil - 1:
                done_rows[g].update(range(n * span // RB, (n + 1) * span // RB))
        while merged < SEQ // RB and all(merged in d for d in done_rows):
            merge(merged)
            merged += 1


def _attn(qkv):
    in_specs = []
    args = []
    for g in range(N_GROUPS):
        for t in range(3):
            in_specs.append(pl.BlockSpec((1, 1, SEQ, A_HEAD_DIM), lambda b, h, t=t: (b, t, 0, h)))
            args.append(qkv[g])
    return pl.pallas_call(
        _attn_kernel,
        grid=(BATCH, A_HEADS),
        in_specs=in_specs,
        out_specs=pl.BlockSpec((SEQ, A_HEAD_DIM), lambda b, h: (b, h)),
        out_shape=jax.ShapeDtypeStruct((BATCH * SEQ, A_OUT_WIDTH), BF16),
        scratch_shapes=[pltpu.VMEM((A_GROUPS[g][1] * GATHER_PITCH if g in GATHERED else SEQ, LANES), F32)
                        for _ in range(3) for g in range(N_GROUPS)],
        compiler_params=pltpu.CompilerParams(dimension_semantics=("parallel", "parallel"),
                                             vmem_limit_bytes=VMEM_LIMIT),
        name="attn",
    )(*args)


O_TM = 1024


def _out_kernel(x_ref, prew_ref, hm_ref, att_ref, wg_ref, wpm_ref, wpa_ref, wout_ref, postw_ref,
                y_ref, wgb_ref):
    @pl.when(pl.program_id(0) == 0)
    def _():
        wgb_ref[...] = wg_ref[...].astype(BF16)

    x = x_ref[...]
    ms = jnp.mean(x * x, axis=-1, keepdims=True)
    h = (x * lax.rsqrt(ms + NORM_EPS) * prew_ref[...]).astype(BF16)
    gates = _dot_nt(h, wgb_ref[...])
    za = _silu(gates[:, :A_OUT_WIDTH])
    sgm = _sigmoid(gates[:, A_OUT_WIDTH:A_OUT_WIDTH + D_MODEL])
    sga = _sigmoid(gates[:, A_OUT_WIDTH + D_MODEL:])
    a = (att_ref[...].astype(F32) * za).astype(BF16)
    bm = jnp.dot(hm_ref[...], wpm_ref[...], preferred_element_type=F32)
    ba = jnp.dot(a, wpa_ref[...], preferred_element_type=F32)
    merged = (sgm * bm + sga * ba).astype(BF16)
    y = jnp.dot(merged, wout_ref[...], preferred_element_type=F32)
    ms2 = jnp.mean(y * y, axis=-1, keepdims=True)
    y_ref[...] = x + y * lax.rsqrt(ms2 + NORM_EPS) * postw_ref[...]


def _out(x2d, pre_w, hm, att, wt, wpm, wpa, wout, post_w):
    n = x2d.shape[0]
    const = lambda i: (0, 0)
    return pl.pallas_call(
        _out_kernel,
        grid=(n // O_TM,),
        in_specs=[
            pl.BlockSpec((O_TM, D_MODEL), lambda i: (i, 0)),
            pl.BlockSpec((1, D_MODEL), const),
            pl.BlockSpec((O_TM, M_WIDTH), lambda i: (i, 0)),
            pl.BlockSpec((O_TM, A_OUT_WIDTH), lambda i: (i, 0)),
            pl.BlockSpec((pl.Element(IN_WIDTH - OFF_ZA), pl.Element(D_MODEL)), lambda i: (OFF_ZA, 0),
                         pipeline_mode=pl.Buffered(1)),
            pl.BlockSpec((M_WIDTH, D_MODEL), const),
            pl.BlockSpec((A_OUT_WIDTH, D_MODEL), const),
            pl.BlockSpec((D_MODEL, D_MODEL), const),
            pl.BlockSpec((1, D_MODEL), const),
        ],
        out_specs=pl.BlockSpec((O_TM, D_MODEL), lambda i: (i, 0)),
        out_shape=jax.ShapeDtypeStruct((n, D_MODEL), F32),
        scratch_shapes=[pltpu.VMEM((IN_WIDTH - OFF_ZA, D_MODEL), BF16)],
        compiler_params=pltpu.CompilerParams(dimension_semantics=("arbitrary",),
                                             vmem_limit_bytes=VMEM_LIMIT),
        name="outproj",
    )(x2d, pre_w.reshape(1, D_MODEL), hm, att, wt, wpm, wpa, wout, post_w.reshape(1, D_MODEL))


def _rope_tables():
    pos = np.arange(SEQ, dtype=np.float64)
    inv_freq = ROPE_THETA ** (-np.arange(0, ROPE_DIM, 2, dtype=np.float64) / ROPE_DIM)
    ang = pos[:, None] * inv_freq[None, :]
    cos, sin = np.cos(ang), np.sin(ang)
    half = ROPE_DIM // 2
    pad = A_HEAD_DIM - ROPE_DIM
    cos_t = np.concatenate([cos, cos, np.ones((SEQ, pad))], axis=1)
    sin_up = np.concatenate([np.zeros((SEQ, half)), sin, np.zeros((SEQ, pad))], axis=1)
    sin_dn = np.concatenate([-sin, np.zeros((SEQ, half + pad))], axis=1)
    tab = np.stack([cos_t, sin_up, sin_dn]).astype(np.float32)
    out = []
    for _, dil in A_GROUPS:
        ls = SEQ // dil
        out.append(tab.reshape(3, ls, dil, A_HEAD_DIM).transpose(0, 2, 1, 3).reshape(3, SEQ, A_HEAD_DIM))
    return jnp.asarray(np.stack(out))


def _layer(x, pre_w, w_in, b_if, conv_w, conv_b, m_norm_w, w_pm, w_pa, w_out, post_w):
    x2d = x.reshape(BATCH * SEQ, D_MODEL)
    wt = w_in.T
    wg = jnp.pad(wt[OFF_GATES:OFF_QA], ((0, LANES - 2 * M_HEADS), (0, 0))).astype(BF16)
    tabs = _rope_tables()

    mall, kt, gt, h = _mproj(x2d, pre_w, wt, wg, b_if.reshape(2 * M_HEADS, 1), conv_w,
                             conv_b.reshape(1, 2 * M_WIDTH))
    qkv = [_aproj(h, wt, tabs[g], g) for g in range(N_GROUPS)]
    hm = _mlstm(mall, kt, _gates(gt))
    att = _attn(qkv)
    w_pm_n = (m_norm_w[:, None] * w_pm).astype(BF16)
    y = _out(x2d, pre_w, hm, att, wt, w_pm_n, w_pa.astype(BF16), w_out.astype(BF16), post_w)
    return y.reshape(BATCH, SEQ, D_MODEL)


@jax.jit
def kernel(x, pre_norm_w, w_in, b_if, conv_w, conv_b, m_norm_w, w_proj_m, w_proj_a, w_out, post_norm_w):
    for layer in range(pre_norm_w.shape[0]):
        x = _layer(x, pre_norm_w[layer], w_in[layer], b_if[layer], conv_w[layer], conv_b[layer],
                   m_norm_w[layer], w_proj_m[layer], w_proj_a[layer], w_out[layer], post_norm_w[layer])
    return x
```

```python
import functools

import jax
import jax.numpy as jnp
import numpy as np
from jax import lax
from jax.experimental import pallas as pl
from jax.experimental.pallas import tpu as pltpu

D_MODEL = 1024
BATCH = 8
SEQ = 2048
M_WIDTH = 1024
M_HEADS = 4
M_HEAD_DIM = 256
CONV_WIDTH = 4
A_GROUPS = ((128, 1), (512, 4), (2048, 16))
N_GROUPS = 3
A_HEADS = 4
A_HEAD_DIM = 128
A_OUT_WIDTH = 512
A_QKV_WIDTH = 1536
A_BLOCK = 128
ROPE_DIM = 32
ROPE_THETA = 500000.0
NORM_EPS = 1e-6
NEG_INF = -1e30
LOG2E = 1.4426950408889634

OFF_GATES = 5 * M_WIDTH
OFF_QA = OFF_GATES + 2 * M_HEADS
OFF_KA = OFF_QA + A_QKV_WIDTH
OFF_VA = OFF_KA + A_QKV_WIDTH
OFF_ZA = OFF_VA + A_QKV_WIDTH
OFF_GM = OFF_ZA + A_OUT_WIDTH
IN_WIDTH = OFF_GM + 2 * D_MODEL

LANES = 128
M_CHUNK = 256
N_CHUNKS = SEQ // M_CHUNK
V7X_VMEM_BYTES = 64 * 1024 * 1024
VMEM_LIMIT = V7X_VMEM_BYTES * 3 // 4
MPROJ_VMEM_LIMIT = V7X_VMEM_BYTES * 7 // 8

F32 = jnp.float32
BF16 = jnp.bfloat16


def _dot_nt(a, bt):
    return lax.dot_general(a, bt, (((1,), (1,)), ((), ())), preferred_element_type=F32)


def _sigmoid(x):
    return 1.0 / (1.0 + jnp.exp(-x))


def _silu(x):
    return x * _sigmoid(x)


M_TN = 512


SUBLANES = 8


def _shift_rows(x3, s, sub, carry):
    rot = pltpu.roll(x3, s, axis=1)
    prev = jnp.concatenate([carry, rot[:-1]], axis=0)
    return jnp.where(sub < s, prev, rot), rot[-1:]


def _conv_tile(h_ref, w, cw, cb, emit):
    cols = w.shape[0]
    sub = lax.broadcasted_iota(jnp.int32, (1, SUBLANES, cols), 1)
    zero = jnp.zeros((1, SUBLANES, cols), F32)
    carry = [zero, zero]

    def matmul(p):
        return _dot_nt(h_ref[p * M_CHUNK:(p + 1) * M_CHUNK, :], w)

    def conv(acc):
        x3 = acc.reshape(M_CHUNK // SUBLANES, SUBLANES, cols)
        d1, carry[0] = _shift_rows(x3, 1, sub, carry[0])
        near = x3 * cw[3] + d1 * cw[2] + cb[0]
        far = x3 * cw[1] + d1 * cw[0]
        d2, carry[1] = _shift_rows(far, 2, sub, carry[1])
        return (near + d2).reshape(M_CHUNK, cols)

    acc = matmul(0)
    for p in range(N_CHUNKS):
        nxt = matmul(p + 1) if p + 1 < N_CHUNKS else None
        emit(p, conv(acc))
        acc = nxt


M_TILES = 5 * M_WIDTH // M_TN


def _mproj_kernel(x_hbm, prew_ref, w_ref, wg_ref, bif_ref, cw_ref, cb_ref,
                  mall_ref, kt_ref, gt_ref, h_ref, xbuf_ref, xsem_ref, wc_ref):
    b = pl.program_id(0)
    j = pl.program_id(1)

    @pl.when(b == 0)
    def _():
        wc_ref[j] = w_ref[...].astype(BF16)

    def x_copy(bb):
        rows = pl.ds(pl.multiple_of(bb * SEQ, SEQ), SEQ)
        return pltpu.make_async_copy(x_hbm.at[rows], xbuf_ref.at[bb % 2], xsem_ref.at[bb % 2])

    def matmul():
        return _dot_nt(h_ref[...], wc_ref[j])

    @pl.when(jnp.logical_and(b == 0, j == 0))
    def _():
        x_copy(b).start()

    @pl.when(j == 0)
    def _():
        x_copy(b).wait()

        @pl.when(b + 1 < pl.num_programs(0))
        def _():
            x_copy(b + 1).start()

        x = xbuf_ref[b % 2]
        ms = jnp.mean(x * x, axis=-1, keepdims=True)
        h_ref[...] = (x * lax.rsqrt(ms + NORM_EPS) * prew_ref[...]).astype(BF16)
        g = _dot_nt(h_ref[...], wg_ref[...])
        gt = g.T[:2 * M_HEADS, :] + bif_ref[...]
        for c in range(N_CHUNKS):
            gt_ref[0, :, c, :] = gt[:, c * M_CHUNK:(c + 1) * M_CHUNK]

    @pl.when(j < 2)
    def _():
        def emit(p, y):
            mall_ref[p * M_CHUNK:(p + 1) * M_CHUNK, :] = _silu(y).astype(BF16)
        _conv_tile(h_ref, wc_ref[j], cw_ref[...], cb_ref[...], emit)

    @pl.when(jnp.logical_and(j >= 2, j < 4))
    def _():
        def emit(p, y):
            y = _silu(y) * (M_HEAD_DIM ** -0.5)
            mall_ref[p * M_CHUNK:(p + 1) * M_CHUNK, :] = y.astype(BF16)
            for hh in range(M_TN // M_HEAD_DIM):
                kt_ref[0, hh, p] = y[:, hh * M_HEAD_DIM:(hh + 1) * M_HEAD_DIM].T.astype(BF16)
        _conv_tile(h_ref, wc_ref[j], cw_ref[...], cb_ref[...], emit)

    @pl.when(jnp.logical_and(j >= 4, j < 6))
    def _():
        mall_ref[...] = matmul().astype(BF16)

    @pl.when(jnp.logical_and(j >= 6, j < 8))
    def _():
        mall_ref[...] = _sigmoid(matmul()).astype(BF16)

    @pl.when(j >= 8)
    def _():
        mall_ref[...] = _silu(matmul()).astype(BF16)


def _mproj(x2d, pre_w, wt, wg, bif, conv_w, conv_b):
    return pl.pallas_call(
        _mproj_kernel,
        grid=(BATCH, M_TILES),
        in_specs=[
            pl.BlockSpec(memory_space=pl.ANY),
            pl.BlockSpec((1, D_MODEL), lambda i, j: (0, 0)),
            pl.BlockSpec((M_TN, D_MODEL), lambda i, j: (jnp.where(i == 0, j, M_TILES - 1), 0)),
            pl.BlockSpec((LANES, D_MODEL), lambda i, j: (0, 0)),
            pl.BlockSpec((2 * M_HEADS, 1), lambda i, j: (0, 0)),
            pl.BlockSpec((CONV_WIDTH, M_TN), lambda i, j: (0, jnp.minimum(j, 3))),
            pl.BlockSpec((1, M_TN), lambda i, j: (0, jnp.minimum(j, 3))),
        ],
        out_specs=[
            pl.BlockSpec((SEQ, M_TN), lambda i, j: (i, j)),
            pl.BlockSpec((1, M_TN // M_HEAD_DIM, N_CHUNKS, M_HEAD_DIM, M_CHUNK),
                         lambda i, j: (i, jnp.clip(j - 2, 0, 1), 0, 0, 0)),
            pl.BlockSpec((1, 2 * M_HEADS, N_CHUNKS, M_CHUNK), lambda i, j: (i, 0, 0, 0)),
            pl.BlockSpec((SEQ, D_MODEL), lambda i, j: (i, 0)),
        ],
        out_shape=[
            jax.ShapeDtypeStruct((BATCH * SEQ, 5 * M_WIDTH), BF16),
            jax.ShapeDtypeStruct((BATCH, M_HEADS, N_CHUNKS, M_HEAD_DIM, M_CHUNK), BF16),
            jax.ShapeDtypeStruct((BATCH, 2 * M_HEADS, N_CHUNKS, M_CHUNK), F32),
            jax.ShapeDtypeStruct((BATCH * SEQ, D_MODEL), BF16),
        ],
        scratch_shapes=[pltpu.VMEM((2, SEQ, D_MODEL), F32), pltpu.SemaphoreType.DMA((2,)),
                        pltpu.VMEM((M_TILES, M_TN, D_MODEL), BF16)],
        compiler_params=pltpu.CompilerParams(dimension_semantics=("arbitrary", "arbitrary"),
                                             vmem_limit_bytes=MPROJ_VMEM_LIMIT),
        name="mproj",
    )(x2d, pre_w.reshape(1, D_MODEL), wt, wg, bif, conv_w, conv_b)


A_TILES = 3
EASY_STRIDE = 4


def _aproj_kernel(dil, h_hbm, w_ref, tab_ref, o_ref, acc0_ref, acc1_ref, tmp_ref, hbuf_ref, hsem_ref):
    s = pl.program_id(0)
    last = BATCH * A_TILES
    pt = (jnp.maximum(s, 1) - 1) % A_TILES

    accs = (acc0_ref, acc1_ref)

    b = jnp.minimum(s, last - 1) // A_TILES

    def h_copy(bb):
        rows = pl.ds(pl.multiple_of(bb * SEQ, SEQ), SEQ)
        return pltpu.make_async_copy(h_hbm.at[rows], hbuf_ref.at[bb % 2], hsem_ref.at[bb % 2])

    @pl.when(s == 0)
    def _():
        h_copy(b).start()

    @pl.when(jnp.logical_and(s % A_TILES == 0, s < last))
    def _():
        h_copy(b).wait()

        @pl.when(b + 1 < BATCH)
        def _():
            h_copy(b + 1).start()

    def matmul(slot):
        res = _dot_nt(hbuf_ref[b % 2], w_ref[...].astype(BF16))
        for hd in range(A_HEADS):
            accs[slot][hd] = res[:, hd * A_HEAD_DIM:(hd + 1) * A_HEAD_DIM]

    ls = SEQ // dil

    two_pass = dil == EASY_STRIDE * EASY_STRIDE

    def stage(slot, hd):
        if two_pass:
            n4 = SEQ // EASY_STRIDE
            for r4 in range(EASY_STRIDE):
                tmp_ref[hd, r4 * n4:(r4 + 1) * n4, :] = accs[slot][hd, pl.ds(r4, n4, stride=EASY_STRIDE), :]

    def residue_rows(slot, hd, r):
        if dil == 1:
            return accs[slot][hd]
        if two_pass:
            q, r4 = divmod(r, EASY_STRIDE)
            return tmp_ref[hd, pl.ds(r4 * (SEQ // EASY_STRIDE) + q, ls, stride=EASY_STRIDE), :]
        return accs[slot][hd, pl.ds(r, ls, stride=dil), :]

    def rope_epilogue(slot):
        scale = jnp.where(pt == 0, A_HEAD_DIM ** -0.5 * LOG2E, 1.0).astype(F32)
        half = ROPE_DIM // 2
        for hd in range(A_HEADS):
            stage(slot, hd)
            for r in range(dil):
                rows = slice(r * ls, (r + 1) * ls)
                xh = residue_rows(slot, hd, r)
                y = (xh * tab_ref[0, rows, :] + pltpu.roll(xh, half, axis=1) * tab_ref[1, rows, :]
                     + pltpu.roll(xh, A_HEAD_DIM - half, axis=1) * tab_ref[2, rows, :])
                o_ref[0, 0, rows, hd * A_HEAD_DIM:(hd + 1) * A_HEAD_DIM] = (y * scale).astype(BF16)

    def cast_epilogue(slot):
        for hd in range(A_HEADS):
            stage(slot, hd)
            for r in range(dil):
                rows = slice(r * ls, (r + 1) * ls)
                o_ref[0, 0, rows, hd * A_HEAD_DIM:(hd + 1) * A_HEAD_DIM] = \
                    residue_rows(slot, hd, r).astype(BF16)

    mid = jnp.logical_and(s > 0, s < last)

    @pl.when(s == 0)
    def _():
        matmul(0)

    for slot in range(2):
        mine = jnp.logical_and(mid, s % 2 == slot)

        @pl.when(jnp.logical_and(mine, pt < 2))
        def _(slot=slot):
            rope_epilogue(1 - slot)
            matmul(slot)

        @pl.when(jnp.logical_and(mine, pt == 2))
        def _(slot=slot):
            cast_epilogue(1 - slot)
            matmul(slot)

    @pl.when(s == last)
    def _():
        cast_epilogue((last - 1) % 2)


def _aproj(h, wt, tabs_g, g):
    dil = A_GROUPS[g][1]

    def w_row(s):
        return OFF_QA + (cur(s) % A_TILES) * A_QKV_WIDTH + g * A_OUT_WIDTH
    last = BATCH * A_TILES

    def cur(s):
        return jnp.minimum(s, last - 1)

    def prev(s):
        return jnp.maximum(s, 1) - 1

    return pl.pallas_call(
        functools.partial(_aproj_kernel, dil),
        grid=(last + 1,),
        in_specs=[
            pl.BlockSpec(memory_space=pl.ANY),
            pl.BlockSpec((pl.Element(A_OUT_WIDTH), pl.Element(D_MODEL)),
                         lambda s: (pl.multiple_of(w_row(s), SUBLANES), 0)),
            pl.BlockSpec((3, SEQ, LANES), lambda s: (0, 0, 0)),
        ],
        out_specs=pl.BlockSpec((1, 1, SEQ, A_OUT_WIDTH),
                               lambda s: (prev(s) // A_TILES, prev(s) % A_TILES, 0, 0)),
        out_shape=jax.ShapeDtypeStruct((BATCH, A_TILES, SEQ, A_OUT_WIDTH), BF16),
        scratch_shapes=[pltpu.VMEM((A_HEADS, SEQ, A_HEAD_DIM), F32) for _ in range(3)]
        + [pltpu.VMEM((2, SEQ, D_MODEL), BF16), pltpu.SemaphoreType.DMA((2,))],
        compiler_params=pltpu.CompilerParams(dimension_semantics=("arbitrary",),
                                             vmem_limit_bytes=VMEM_LIMIT),
        name="aproj_d%d" % dil,
    )(h, wt, tabs_g)


def _lane_scan(x, op, fill):
    lane = lax.broadcasted_iota(jnp.int32, x.shape, 1)
    sh = 1
    while sh < x.shape[1]:
        shifted = jnp.where(lane >= sh, pltpu.roll(x, sh, axis=1), fill)
        x = op(x, shifted)
        sh *= 2
    return x


def _gates_kernel(ig_ref, fg_ref, u_ref, g_ref, mt_ref, mp_ref, w_ref, dec_ref):
    L = M_CHUNK
    ig = ig_ref[...]
    fg = fg_ref[...]
    logf = -(jnp.maximum(-fg, 0.0) + jnp.log1p(jnp.exp(-jnp.abs(fg))))
    b = _lane_scan(logf, jnp.add, 0.0)
    u = ig - b
    cm = _lane_scan(u, jnp.maximum, NEG_INF)
    lane = lax.broadcasted_iota(jnp.int32, ig.shape, 1)
    btot = jnp.broadcast_to(jnp.sum(jnp.where(lane == L - 1, b, 0.0), axis=1, keepdims=True), ig.shape)
    umax = jnp.broadcast_to(jnp.max(cm, axis=1, keepdims=True), ig.shape)
    chunk = lax.broadcasted_iota(jnp.int32, ig.shape, 0) % N_CHUNKS
    m = jnp.zeros(ig.shape, F32)
    for c in range(N_CHUNKS - 1):
        nxt = pltpu.roll(btot + jnp.maximum(m, umax), 1, axis=0)
        m = jnp.where(chunk == c + 1, nxt, m)
    g = jnp.maximum(m, cm)
    glast = jnp.max(g, axis=1, keepdims=True)
    u_ref[...] = u * LOG2E
    g_ref[...] = g * LOG2E
    mt_ref[...] = (b + g) * LOG2E
    mp_ref[...] = m * LOG2E
    w_ref[...] = jnp.exp(u - glast)
    dec_ref[...] = jnp.broadcast_to(jnp.exp(m - glast), ig.shape)


def _gates(gt):
    rows = BATCH * M_HEADS * N_CHUNKS
    ig = gt[:, :M_HEADS].reshape(rows, M_CHUNK)
    fg = gt[:, M_HEADS:].reshape(rows, M_CHUNK)
    spec = pl.BlockSpec((rows, M_CHUNK), lambda: (0, 0))
    return pl.pallas_call(
        _gates_kernel,
        in_specs=[spec, spec],
        out_specs=[spec] * 6,
        out_shape=[jax.ShapeDtypeStruct((rows, M_CHUNK), F32)] * 6,
        name="gates",
    )(ig, fg)


M_HPS = 2


def _mlstm_kernel(q_ref, kt_ref, v_ref, o_ref, z_ref, u_ref, g_ref, mt_ref, mp_ref, w_ref, dec_ref,
                  out_ref):
    L = M_CHUNK
    ti = lax.broadcasted_iota(jnp.int32, (L, L), 0)
    si = lax.broadcasted_iota(jnp.int32, (L, L), 1)
    causal = ti >= si
    ones = jnp.ones((L, LANES), BF16)

    states = [jnp.zeros((M_HEAD_DIM, M_HEAD_DIM + LANES), F32) for _ in range(M_HPS)]
    for c in range(N_CHUNKS):
        rs = slice(c * L, (c + 1) * L)
        for hh in range(M_HPS):
            hs = slice(hh * M_HEAD_DIM, (hh + 1) * M_HEAD_DIM)
            cs = slice(hh * N_CHUNKS + c, hh * N_CHUNKS + c + 1)
            state = states[hh]
            qc = q_ref[rs, hs]
            v1 = jnp.concatenate([v_ref[rs, hs], ones], axis=1)
            ktc = kt_ref[0, hh, c]
            u_r = u_ref[cs, :]
            gcol = jnp.broadcast_to(g_ref[cs, :], (LANES, L)).T
            mcol = jnp.broadcast_to(mt_ref[cs, :], (LANES, L)).T
            dmat = jnp.where(causal, jnp.exp2(u_r - jnp.concatenate([gcol, gcol], axis=1)), 0.0)

            s = jnp.dot(qc, ktc, preferred_element_type=F32) * dmat
            tot = jnp.dot(s.astype(BF16), v1, preferred_element_type=F32)
            if c > 0:
                inter = jnp.exp2(mp_ref[cs, :LANES] - gcol)
                tot = tot + jnp.concatenate([inter, inter, inter], axis=1) \
                    * jnp.dot(qc, state.astype(BF16), preferred_element_type=F32)
            den = tot[:, M_HEAD_DIM:]
            rinv = 1.0 / jnp.maximum(jnp.abs(den), jnp.exp2(-mcol))
            hval = tot[:, :M_HEAD_DIM] * jnp.concatenate([rinv, rinv], axis=1)

            hg = hval * o_ref[rs, hs].astype(F32)
            ms = jnp.mean(hg * hg, axis=1, keepdims=True)
            hn = hg * lax.rsqrt(ms + NORM_EPS)
            out_ref[rs, hs] = (hn * z_ref[rs, hs].astype(F32)).astype(BF16)

            if c + 1 < N_CHUNKS:
                ktw = (ktc.astype(F32) * w_ref[cs, :]).astype(BF16)
                dec = dec_ref[cs, :]
                dec = jnp.concatenate([dec, dec[:, :LANES]], axis=1)
                states[hh] = dec * state + jnp.dot(ktw, v1, preferred_element_type=F32)


def _mlstm(mall, kt, gate_stats):
    blk = (SEQ, M_HPS * M_HEAD_DIM)
    groups = M_HEADS // M_HPS

    def seg(k):
        return pl.BlockSpec(blk, lambda b, h, k=k: (b, k * groups + h))

    stat = pl.BlockSpec((M_HPS * N_CHUNKS, M_CHUNK), lambda b, h: (b * groups + h, 0))
    return pl.pallas_call(
        _mlstm_kernel,
        grid=(BATCH, groups),
        in_specs=[
            seg(0),
            pl.BlockSpec((1, M_HPS, N_CHUNKS, M_HEAD_DIM, M_CHUNK), lambda b, h: (b, h, 0, 0, 0)),
            seg(2), seg(3), seg(4),
        ] + [stat] * 6,
        out_specs=pl.BlockSpec(blk, lambda b, h: (b, h)),
        out_shape=jax.ShapeDtypeStruct((BATCH * SEQ, M_WIDTH), BF16),
        compiler_params=pltpu.CompilerParams(dimension_semantics=("parallel", "parallel"),
                                             vmem_limit_bytes=VMEM_LIMIT),
        name="mlstm",
    )(mall, kt, mall, mall, mall, *gate_stats)


GATHERED = (2,)
GATHER_PITCH = A_BLOCK + 8
ATTN_WAVE = 12


def _attn_block_order():
    order = [(r, 2) for r in range(A_GROUPS[2][1])]
    nblk1 = SEQ // A_GROUPS[1][1] // A_BLOCK
    per_span = SEQ // A_BLOCK // nblk1
    for n in range(nblk1):
        order += [(r * nblk1 + n, 1) for r in range(A_GROUPS[1][1])]
        order += [(n * per_span + i, 0) for i in range(per_span)]
    return order


def _attn_kernel(q0, k0, v0, q1, k1, v1, q2, k2, v2, out_ref,
                 a0, a1, a2, m0, m1, m2, l0, l1, l2):
    BLK = A_BLOCK
    ti = lax.broadcasted_iota(jnp.int32, (BLK, BLK), 0)
    ji = lax.broadcasted_iota(jnp.int32, (BLK, BLK), 1)
    cur_ok = ji <= ti
    ti2 = lax.broadcasted_iota(jnp.int32, (BLK, 2 * BLK), 0)
    krel = lax.broadcasted_iota(jnp.int32, (BLK, 2 * BLK), 1) - BLK
    band_ok = jnp.logical_and(krel <= ti2, krel >= ti2 - BLK)

    qs = (q0, q1, q2)
    ks = (k0, k1, k2)
    vs = (v0, v1, v2)
    accs = (a0, a1, a2)
    ms = (m0, m1, m2)
    ls = (l0, l1, l2)

    RB = 256

    def merge(rb):
        sl = slice(rb * RB, (rb + 1) * RB)

        def rows(refs, g):
            if g in GATHERED:
                dil = A_GROUPS[g][1]
                j0 = rb * RB // dil
                return jnp.concatenate([refs[g][pl.ds(j0 + jj, dil, stride=GATHER_PITCH), :]
                                        for jj in range(RB // dil)], axis=0)
            return refs[g][sl, :]

        mg = [rows(ms, g) for g in range(N_GROUPS)]
        mmax = jnp.maximum(jnp.maximum(mg[0], mg[1]), mg[2])
        num = None
        den = None
        for g in range(N_GROUPS):
            e = jnp.exp2(mg[g] - mmax)
            num = e * rows(accs, g) if num is None else num + e * rows(accs, g)
            den = e * rows(ls, g) if den is None else den + e * rows(ls, g)
        out_ref[sl, :] = (num * (1.0 / den)).astype(BF16)

    blocks = _attn_block_order()
    done_rows = [set() for _ in range(N_GROUPS)]
    merged = 0
    for w0 in range(0, len(blocks), ATTN_WAVE):
        wave = []
        for i, g in blocks[w0:w0 + ATTN_WAVE]:
            dil = A_GROUPS[g][1]
            r, n = divmod(i, SEQ // dil // BLK)
            rows = slice(i * BLK, (i + 1) * BLK)
            keys = slice((i - 1) * BLK, (i + 1) * BLK) if n > 0 else rows
            valid = band_ok if n > 0 else cur_ok
            s = lax.dot_general(qs[g][0, 0, rows, :], ks[g][0, 0, keys, :], (((1,), (1,)), ((), ())),
                                preferred_element_type=F32)
            s = jnp.where(valid, s, NEG_INF)
            m = jnp.max(s, axis=1, keepdims=True)
            tok0 = n * BLK * dil + r
            if g in GATHERED:
                idx = pl.ds(r * GATHER_PITCH, BLK)
            else:
                idx = pl.ds(tok0, BLK, stride=dil) if dil > 1 else pl.ds(tok0, BLK)
            wave.append((g, keys, valid, idx, s, m))
        for g, keys, valid, idx, s, m in wave:
            p = jnp.where(valid, jnp.exp2(s - m), 0.0)
            nkeys = keys.stop - keys.start
            v1 = jnp.concatenate([vs[g][0, 0, keys, :], jnp.ones((nkeys, LANES), BF16)], axis=1)
            acc = jnp.dot(p.astype(BF16), v1, preferred_element_type=F32)
            accs[g][idx, :] = acc[:, :LANES]
            ms[g][idx, :] = jnp.broadcast_to(m, (BLK, LANES))
            ls[g][idx, :] = acc[:, LANES:]
        for i, g in blocks[w0:w0 + ATTN_WAVE]:
            dil = A_GROUPS[g][1]
            r, n = divmod(i, SEQ // dil // BLK)
            span = BLK * dil
            if r == dil - 1:
                done_rows[g].update(range(n * span // RB, (n + 1) * span // RB))
        while merged < SEQ // RB and all(merged in d for d in done_rows):
            merge(merged)
            merged += 1


def _attn(qkv):
    in_specs = []
    args = []
    for g in range(N_GROUPS):
        for t in range(3):
            in_specs.append(pl.BlockSpec((1, 1, SEQ, A_HEAD_DIM), lambda b, h, t=t: (b, t, 0, h)))
            args.append(qkv[g])
    return pl.pallas_call(
        _attn_kernel,
        grid=(BATCH, A_HEADS),
        in_specs=in_specs,
        out_specs=pl.BlockSpec((SEQ, A_HEAD_DIM), lambda b, h: (b, h)),
        out_shape=jax.ShapeDtypeStruct((BATCH * SEQ, A_OUT_WIDTH), BF16),
        scratch_shapes=[pltpu.VMEM((A_GROUPS[g][1] * GATHER_PITCH if g in GATHERED else SEQ, LANES), F32)
                        for _ in range(3) for g in range(N_GROUPS)],
        compiler_params=pltpu.CompilerParams(dimension_semantics=("parallel", "parallel"),
                                             vmem_limit_bytes=VMEM_LIMIT),
        name="attn",
    )(*args)


O_TM = 1024


def _out_kernel(x_ref, prew_ref, hm_ref, att_ref, wg_ref, wpm_ref, wpa_ref, wout_ref, postw_ref,
                y_ref, wgb_ref):
    @pl.when(pl.program_id(0) == 0)
    def _():
        wgb_ref[...] = wg_ref[...].astype(BF16)

    x = x_ref[...]
    ms = jnp.mean(x * x, axis=-1, keepdims=True)
    h = (x * lax.rsqrt(ms + NORM_EPS) * prew_ref[...]).astype(BF16)
    gates = _dot_nt(h, wgb_ref[...])
    za = _silu(gates[:, :A_OUT_WIDTH])
    sgm = _sigmoid(gates[:, A_OUT_WIDTH:A_OUT_WIDTH + D_MODEL])
    sga = _sigmoid(gates[:, A_OUT_WIDTH + D_MODEL:])
    a = (att_ref[...].astype(F32) * za).astype(BF16)
    bm = jnp.dot(hm_ref[...], wpm_ref[...], preferred_element_type=F32)
    ba = jnp.dot(a, wpa_ref[...], preferred_element_type=F32)
    merged = (sgm * bm + sga * ba).astype(BF16)
    y = jnp.dot(merged, wout_ref[...], preferred_element_type=F32)
    ms2 = jnp.mean(y * y, axis=-1, keepdims=True)
    y_ref[...] = x + y * lax.rsqrt(ms2 + NORM_EPS) * postw_ref[...]


def _out(x2d, pre_w, hm, att, wt, wpm, wpa, wout, post_w):
    n = x2d.shape[0]
    const = lambda i: (0, 0)
    return pl.pallas_call(
        _out_kernel,
        grid=(n // O_TM,),
        in_specs=[
            pl.BlockSpec((O_TM, D_MODEL), lambda i: (i, 0)),
            pl.BlockSpec((1, D_MODEL), const),
            pl.BlockSpec((O_TM, M_WIDTH), lambda i: (i, 0)),
            pl.BlockSpec((O_TM, A_OUT_WIDTH), lambda i: (i, 0)),
            pl.BlockSpec((pl.Element(IN_WIDTH - OFF_ZA), pl.Element(D_MODEL)), lambda i: (OFF_ZA, 0),
                         pipeline_mode=pl.Buffered(1)),
            pl.BlockSpec((M_WIDTH, D_MODEL), const),
            pl.BlockSpec((A_OUT_WIDTH, D_MODEL), const),
            pl.BlockSpec((D_MODEL, D_MODEL), const),
            pl.BlockSpec((1, D_MODEL), const),
        ],
        out_specs=pl.BlockSpec((O_TM, D_MODEL), lambda i: (i, 0)),
        out_shape=jax.ShapeDtypeStruct((n, D_MODEL), F32),
        scratch_shapes=[pltpu.VMEM((IN_WIDTH - OFF_ZA, D_MODEL), BF16)],
        compiler_params=pltpu.CompilerParams(dimension_semantics=("arbitrary",),
                                             vmem_limit_bytes=VMEM_LIMIT),
        name="outproj",
    )(x2d, pre_w.reshape(1, D_MODEL), hm, att, wt, wpm, wpa, wout, post_w.reshape(1, D_MODEL))


def _rope_tables():
    pos = np.arange(SEQ, dtype=np.float64)
    inv_freq = ROPE_THETA ** (-np.arange(0, ROPE_DIM, 2, dtype=np.float64) / ROPE_DIM)
    ang = pos[:, None] * inv_freq[None, :]
    cos, sin = np.cos(ang), np.sin(ang)
    half = ROPE_DIM // 2
    pad = A_HEAD_DIM - ROPE_DIM
    cos_t = np.concatenate([cos, cos, np.ones((SEQ, pad))], axis=1)
    sin_up = np.concatenate([np.zeros((SEQ, half)), sin, np.zeros((SEQ, pad))], axis=1)
    sin_dn = np.concatenate([-sin, np.zeros((SEQ, half + pad))], axis=1)
    tab = np.stack([cos_t, sin_up, sin_dn]).astype(np.float32)
    out = []
    for _, dil in A_GROUPS:
        ls = SEQ // dil
        out.append(tab.reshape(3, ls, dil, A_HEAD_DIM).transpose(0, 2, 1, 3).reshape(3, SEQ, A_HEAD_DIM))
    return jnp.asarray(np.stack(out))


def _layer(x, pre_w, w_in, b_if, conv_w, conv_b, m_norm_w, w_pm, w_pa, w_out, post_w):
    x2d = x.reshape(BATCH * SEQ, D_MODEL)
    wt = w_in.T
    wg = jnp.pad(wt[OFF_GATES:OFF_QA], ((0, LANES - 2 * M_HEADS), (0, 0))).astype(BF16)
    tabs = _rope_tables()

    mall, kt, gt, h = _mproj(x2d, pre_w, wt, wg, b_if.reshape(2 * M_HEADS, 1), conv_w,
                             conv_b.reshape(1, 2 * M_WIDTH))
    qkv = [_aproj(h, wt, tabs[g], g) for g in range(N_GROUPS)]
    hm = _mlstm(mall, kt, _gates(gt))
    att = _attn(qkv)
    w_pm_n = (m_norm_w[:, None] * w_pm).astype(BF16)
    y = _out(x2d, pre_w, hm, att, wt, w_pm_n, w_pa.astype(BF16), w_out.astype(BF16), post_w)
    return y.reshape(BATCH, SEQ, D_MODEL)


@jax.jit
def kernel(x, pre_norm_w, w_in, b_if, conv_w, conv_b, m_norm_w, w_proj_m, w_proj_a, w_out, post_norm_w):
    for layer in range(pre_norm_w.shape[0]):
        x = _layer(x, pre_norm_w[layer], w_in[layer], b_if[layer], conv_w[layer], conv_b[layer],
                   m_norm_w[layer], w_proj_m[layer], w_proj_a[layer], w_out[layer], post_norm_w[layer])
    return x
```
